```python
import math
import jax
import jax.numpy as jnp
from jax import lax
import numpy as np

D_MODEL = 1024
BATCH = 8
SEQ = 4096
DEPTH = 2

GRID_W = 64
CTX_LEN = 256
ROPE_THETA = 10000.0
Q_BLOCK = 128
NORM_EPS = 1e-6
L2_EPS = 1e-6

DIFF_HEADS = 4
DIFF_QK_DIM = 32
DIFF_V_DIM = 2 * DIFF_QK_DIM
DIFF_WIDTH = DIFF_HEADS * DIFF_V_DIM
GQA_Q_HEADS = 6
GQA_KV_HEADS = 2
GQA_HEAD_DIM = 64
GQA_WIDTH = GQA_Q_HEADS * GQA_HEAD_DIM
GDN_HEADS = 6
GDN_HEAD_DIM = 64
GDN_WIDTH = GDN_HEADS * GDN_HEAD_DIM
GDN_CONV = 5
GDN_CHUNK = 64

MIX_WIDTH = DIFF_WIDTH + GQA_WIDTH + GDN_WIDTH
FFN_HIDDEN = -(-(8 * D_MODEL) // (3 * 256)) * 256

IN_SPLITS = (
    DIFF_HEADS * 2 * DIFF_QK_DIM,
    DIFF_HEADS * 2 * DIFF_QK_DIM,
    DIFF_WIDTH,
    GQA_Q_HEADS * GQA_HEAD_DIM,
    GQA_KV_HEADS * GQA_HEAD_DIM,
    GQA_KV_HEADS * GQA_HEAD_DIM,
    3 * GDN_WIDTH,
    GDN_WIDTH,
    2 * GDN_HEADS,
    2 * GDN_HEADS,
)
IN_DIM = sum(IN_SPLITS)

kernel_name = "hybrid_head_group_dit"


def rms_norm(x, g):
    xf = x.astype(jnp.float32)
    y = xf * lax.rsqrt(jnp.mean(xf * xf, axis=-1, keepdims=True) + NORM_EPS)
    return (y * g.astype(jnp.float32)).astype(x.dtype)


def l2_normalize(x):
    xf = x.astype(jnp.float32)
    return xf * lax.rsqrt(jnp.sum(xf * xf, axis=-1, keepdims=True) + L2_EPS)


def modulate(h, shift, scale):
    return h * (1.0 + scale) + shift


def axial_rope_tables(rows, rot_dim):
    row = jnp.repeat(jnp.arange(rows), GRID_W).astype(jnp.float32)
    col = jnp.tile(jnp.arange(GRID_W), rows).astype(jnp.float32)
    n_freq = rot_dim // 4
    inv_freq = ROPE_THETA ** (-jnp.arange(n_freq, dtype=jnp.float32) / n_freq)
    ang = jnp.stack([row[:, None] * inv_freq, col[:, None] * inv_freq], axis=1)
    return jnp.cos(ang), jnp.sin(ang)


def apply_axial_rope(x, cos, sin):
    shp = x.shape
    nf = cos.shape[-1]
    xr = x.reshape(shp[0], shp[1], -1, 2, 2, nf).astype(jnp.float32)
    x1, x2 = xr[..., 0, :], xr[..., 1, :]
    c, s = cos[:, None], sin[:, None]
    out = jnp.stack([x1 * c - x2 * s, x2 * c + x1 * s], axis=-2)
    return out.reshape(shp).astype(x.dtype)


def sweep_query_blocks(attend, queries):
    t = queries[0].shape[1]
    nb = t // Q_BLOCK
    blocks = tuple(jnp.moveaxis(q.reshape((q.shape[0], nb, Q_BLOCK) + q.shape[2:]), 1, 0) for q in queries)
    out = lax.map(lambda qb: attend(*qb), blocks)
    out = jnp.moveaxis(out, 0, 1)
    return out.reshape((out.shape[0], t) + out.shape[3:])


def centred_depthwise_conv(x, w):
    taps = w.shape[0]
    return lax.conv_general_dilated(
        x, w[:, None, :].astype(x.dtype), window_strides=(1,),
        padding=[(taps // 2, taps // 2)], dimension_numbers=('NWC', 'WIO', 'NWC'),
        feature_group_count=x.shape[-1])


def chunked_gated_delta(q, k, v, g, beta, s0):
    bsz, t, h, dk = q.shape
    dv = v.shape[-1]
    cs = GDN_CHUNK
    n = t // cs

    def chunks(a):
        a = a.reshape((bsz, n, cs, h) + a.shape[3:])
        return jnp.moveaxis(jnp.moveaxis(a, 1, 0), 2, 3)

    qc, kc, vc, gc, bc = (chunks(a) for a in (q, k, v, g, beta))
    gcum = jnp.cumsum(gc, axis=-1)
    idx = jnp.arange(cs)
    incl = idx[:, None] >= idx[None, :]
    strict = idx[:, None] > idx[None, :]
    decay = jnp.exp(jnp.where(incl, gcum[..., :, None] - gcum[..., None, :], -jnp.inf))
    kk = jnp.einsum('nbhid,nbhjd->nbhij', kc, kc)
    a_mat = jnp.where(strict, kk * decay * bc[..., :, None], 0.0) + jnp.eye(cs, dtype=jnp.float32)
    rhs = jnp.concatenate([vc * bc[..., None], kc * (bc * jnp.exp(gcum))[..., None]], axis=-1)
    sol = lax.linalg.triangular_solve(a_mat, rhs, left_side=True, lower=True, unit_diagonal=True)
    u, w = sol[..., :dv], sol[..., dv:]
    qk = jnp.einsum('nbhid,nbhjd->nbhij', qc, kc) * decay
    q_dec = qc * jnp.exp(gcum)[..., None]
    k_dec = kc * jnp.exp(gcum[..., -1:] - gcum)[..., None]
    g_last = jnp.exp(gcum[..., -1])

    def step(state, xs):
        u_i, w_i, qk_i, qd_i, kd_i, gl_i = xs
        v_new = u_i - jnp.einsum('bhck,bhkv->bhcv', w_i, state)
        o_i = jnp.einsum('bhck,bhkv->bhcv', qd_i, state) + jnp.einsum('bhij,bhjv->bhiv', qk_i, v_new)
        state = state * gl_i[..., None, None] + jnp.einsum('bhck,bhcv->bhkv', kd_i, v_new)
        return state, o_i

    s_final, o = lax.scan(step, s0, (u, w, qk, q_dec, k_dec, g_last))
    o = jnp.moveaxis(jnp.moveaxis(o, 3, 2), 0, 1).reshape(bsz, t, h, dv)
    return o, s_final


def diff_mixer(q_l, k_l, v_l, q_c, k_c, v_c, lam_vecs, norm_g, lambda_init, cos, sin, need_ctx):
    def heads(q, k, v):
        b, t = q.shape[:2]
        return (q.reshape(b, t, DIFF_HEADS, 2, DIFF_QK_DIM),
                k.reshape(b, t, DIFF_HEADS, 2, DIFF_QK_DIM),
                v.reshape(b, t, DIFF_HEADS, DIFF_V_DIM))

    ql, kl, vl = heads(q_l, k_l, v_l)
    qc, kc, vc = heads(q_c, k_c, v_c)
    ql = apply_axial_rope(ql, cos, sin)
    kl = apply_axial_rope(kl, cos, sin)
    lf = lam_vecs.astype(jnp.float32)
    lam = jnp.exp(jnp.sum(lf[0] * lf[1])) - jnp.exp(jnp.sum(lf[2] * lf[3])) + lambda_init
    scale = DIFF_QK_DIM ** -0.5

    def attend(q, k, v):
        s = jnp.einsum('bqhcd,bkhcd->bhcqk', q, k).astype(jnp.float32) * scale
        p = jax.nn.softmax(s, axis=-1)
        p = p[:, :, 0] - lam * p[:, :, 1]
        return jnp.einsum('bhqk,bkhe->bqhe', p.astype(v.dtype), v)

    def post(o):
        b, t = o.shape[:2]
        return (rms_norm(o, norm_g) * (1.0 - lambda_init)).reshape(b, t, DIFF_WIDTH)

    k_all = jnp.concatenate([kc, kl], axis=1)
    v_all = jnp.concatenate([vc, vl], axis=1)
    y_l = post(sweep_query_blocks(lambda qb: attend(qb, k_all, v_all), (ql,)))
    y_c = post(attend(qc, kc, vc)) if need_ctx else None
    return y_l, y_c


def gqa_attend(q, k, v):
    b, tq, h, dh = q.shape
    hkv = k.shape[2]
    qg = q.reshape(b, tq, hkv, h // hkv, dh)
    s = jnp.einsum('bqhgd,bkhd->bhgqk', qg, k).astype(jnp.float32) * dh ** -0.5
    p = jax.nn.softmax(s, axis=-1)
    o = jnp.einsum('bhgqk,bkhd->bqhgd', p.astype(v.dtype), v)
    return o.reshape(b, tq, h * dh)


def gqa_mixer(q_l, k_l, v_l, q_c, k_c, v_c, qn_g, kn_g, cos, sin, need_ctx):
    def heads(q, k, v):
        b, t = q.shape[:2]
        q = rms_norm(q.reshape(b, t, GQA_Q_HEADS, GQA_HEAD_DIM), qn_g)
        k = rms_norm(k.reshape(b, t, GQA_KV_HEADS, GQA_HEAD_DIM), kn_g)
        return q, k, v.reshape(b, t, GQA_KV_HEADS, GQA_HEAD_DIM)

    ql, kl, vl = heads(q_l, k_l, v_l)
    qc, kc, vc = heads(q_c, k_c, v_c)
    ql = apply_axial_rope(ql, cos, sin)
    kl = apply_axial_rope(kl, cos, sin)
    k_all = jnp.concatenate([kc, kl], axis=1)
    v_all = jnp.concatenate([vc, vl], axis=1)
    y_l = sweep_query_blocks(lambda qb: gqa_attend(qb, k_all, v_all), (ql,))
    y_c = gqa_attend(qc, kc, vc) if need_ctx else None
    return y_l, y_c


def gdn_mixer(qkv_l, z_l, a_l, b_l, qkv_c, z_c, a_c, b_c, conv_w, a_log, dt_bias, norm_g, need_ctx):
    def prep(qkv, a, b):
        bs, t = qkv.shape[:2]
        qkv = jax.nn.silu(centred_depthwise_conv(qkv, conv_w))
        q, k, v = jnp.split(qkv, 3, axis=-1)
        q = l2_normalize(q.reshape(bs, t, GDN_HEADS, GDN_HEAD_DIM)) * GDN_HEAD_DIM ** -0.5
        k = l2_normalize(k.reshape(bs, t, GDN_HEADS, GDN_HEAD_DIM))
        v = v.reshape(bs, t, GDN_HEADS, GDN_HEAD_DIM).astype(jnp.float32)
        a = a.reshape(bs, t, 2, GDN_HEADS).astype(jnp.float32)
        b = b.reshape(bs, t, 2, GDN_HEADS).astype(jnp.float32)
        g = -jnp.exp(a_log.astype(jnp.float32)) * jax.nn.softplus(a + dt_bias.astype(jnp.float32))
        return q, k, v, g, jax.nn.sigmoid(b)

    ql, kl, vl, gl, bl = prep(qkv_l, a_l, b_l)
    qc, kc, vc, gc, bc = prep(qkv_c, a_c, b_c)
    zero = jnp.zeros((ql.shape[0], GDN_HEADS, GDN_HEAD_DIM, GDN_HEAD_DIM), jnp.float32)
    flip = lambda a: a[:, ::-1]
    oc_f, sc_f = chunked_gated_delta(qc, kc, vc, gc[:, :, 0], bc[:, :, 0], zero)
    ol_f, _ = chunked_gated_delta(ql, kl, vl, gl[:, :, 0], bl[:, :, 0], sc_f)
    oc_b, sc_b = chunked_gated_delta(flip(qc), flip(kc), flip(vc), flip(gc[:, :, 1]), flip(bc[:, :, 1]), zero)
    ol_b, _ = chunked_gated_delta(flip(ql), flip(kl), flip(vl), flip(gl[:, :, 1]), flip(bl[:, :, 1]), sc_b)

    def readout(o, z):
        bs, t = z.shape[:2]
        zh = z.reshape(bs, t, GDN_HEADS, GDN_HEAD_DIM).astype(jnp.float32)
        y = rms_norm(o, norm_g) * jax.nn.silu(zh)
        return y.reshape(bs, t, GDN_WIDTH).astype(z.dtype)

    y_l = readout(ol_f + flip(ol_b), z_l)
    y_c = readout(oc_f + flip(oc_b), z_c) if need_ctx else None
    return y_l, y_c


def swiglu(h, w_gu, w_down):
    gate, up = jnp.split(h @ w_gu, 2, axis=-1)
    return (jax.nn.silu(gate) * up) @ w_down


def setup_inputs(seed: int = 0) -> dict:
    key = jax.random.key(seed)
    ks = jax.random.split(key, 21)
    f32 = jnp.float32

    def nrm(k, shape, scale):
        return jax.random.normal(k, shape, f32) * scale

    dt = jnp.exp(jax.random.uniform(ks[14], (DEPTH, 2, GDN_HEADS), f32, math.log(1e-3), math.log(1e-1)))
    return {
        'x': nrm(ks[0], (BATCH, SEQ, D_MODEL), 1.0),
        'c': nrm(ks[1], (BATCH, D_MODEL), 1.0),
        'ctx': nrm(ks[2], (BATCH, CTX_LEN, D_MODEL), 1.0),
        'c_ctx': nrm(ks[3], (D_MODEL,), 1.0),
        'norm1_g': 1.0 + nrm(ks[4], (DEPTH, D_MODEL), 0.1),
        'ada_w': nrm(ks[5], (DEPTH, D_MODEL, 6 * D_MODEL), 0.5 * D_MODEL ** -0.5),
        'ada_b': nrm(ks[6], (DEPTH, 6 * D_MODEL), 0.02),
        'w_in': nrm(ks[7], (DEPTH, D_MODEL, IN_DIM), D_MODEL ** -0.5),
        'diff_lambda': nrm(ks[8], (DEPTH, 4, DIFF_QK_DIM), 0.1),
        'diff_norm_g': 1.0 + nrm(ks[9], (DEPTH, DIFF_V_DIM), 0.1),
        'q_norm_g': 1.0 + nrm(ks[10], (DEPTH, GQA_HEAD_DIM), 0.1),
        'k_norm_g': 1.0 + nrm(ks[11], (DEPTH, GQA_HEAD_DIM), 0.1),
        'gdn_conv_w': nrm(ks[12], (DEPTH, GDN_CONV, 3 * GDN_WIDTH), GDN_CONV ** -0.5),
        'gdn_a_log': jnp.log(jax.random.uniform(ks[13], (DEPTH, 2, GDN_HEADS), f32, 1.0, 16.0)),
        'gdn_dt_bias': dt + jnp.log(-jnp.expm1(-dt)),
        'gdn_norm_g': 1.0 + nrm(ks[15], (DEPTH, GDN_HEAD_DIM), 0.1),
        'w_out': nrm(ks[16], (DEPTH, MIX_WIDTH, D_MODEL), MIX_WIDTH ** -0.5),
        'norm2_g': 1.0 + nrm(ks[17], (DEPTH, D_MODEL), 0.1),
        'ffn_w_gu': nrm(ks[18], (DEPTH, D_MODEL, 2 * FFN_HIDDEN), D_MODEL ** -0.5),
        'ffn_w_down': nrm(ks[19], (DEPTH, FFN_HIDDEN, D_MODEL), FFN_HIDDEN ** -0.5),
        'final_norm_g': 1.0 + nrm(ks[20], (D_MODEL,), 0.1),
    }


def reference(x, c, ctx, c_ctx, norm1_g, ada_w, ada_b, w_in, diff_lambda, diff_norm_g,
              q_norm_g, k_norm_g, gdn_conv_w, gdn_a_log, gdn_dt_bias, gdn_norm_g, w_out,
              norm2_g, ffn_w_gu, ffn_w_down, final_norm_g):
    ROWS = x.shape[1] // GRID_W
    rope_diff = axial_rope_tables(ROWS, DIFF_QK_DIM)
    rope_gqa = axial_rope_tables(ROWS, GQA_HEAD_DIM)
    offs = np.cumsum(IN_SPLITS)[:-1].tolist()
    silu_c = jax.nn.silu(c)
    silu_cc = jax.nn.silu(c_ctx)
    h = x
    hc = ctx
    for layer in range(DEPTH):
        need_ctx = layer < DEPTH - 1
        lambda_init = 0.8 - 0.6 * math.exp(-0.3 * layer)
        sh1, sc1, g1, sh2, sc2, g2 = jnp.split((silu_c @ ada_w[layer] + ada_b[layer])[:, None, :], 6, axis=-1)
        csh1, csc1, cg1, csh2, csc2, cg2 = jnp.split((silu_cc @ ada_w[layer] + ada_b[layer])[None, None, :], 6, axis=-1)

        a_l = modulate(rms_norm(h, norm1_g[layer]), sh1, sc1)
        a_c = modulate(rms_norm(hc, norm1_g[layer]), csh1, csc1)
        pl = jnp.split(a_l @ w_in[layer], offs, axis=-1)
        pc = jnp.split(a_c @ w_in[layer], offs, axis=-1)
        d_l, d_c = diff_mixer(pl[0], pl[1], pl[2], pc[0], pc[1], pc[2], diff_lambda[layer],
                              diff_norm_g[layer], lambda_init, rope_diff[0], rope_diff[1], need_ctx)
        q_l, q_c = gqa_mixer(pl[3], pl[4], pl[5], pc[3], pc[4], pc[5], q_norm_g[layer], k_norm_g[layer],
                             rope_gqa[0], rope_gqa[1], need_ctx)
        n_l, n_c = gdn_mixer(pl[6], pl[7], pl[8], pl[9], pc[6], pc[7], pc[8], pc[9], gdn_conv_w[layer],
                             gdn_a_log[layer], gdn_dt_bias[layer], gdn_norm_g[layer], need_ctx)
        h = h + g1 * (jnp.concatenate([d_l, q_l, n_l], axis=-1) @ w_out[layer])
        h = h + g2 * swiglu(modulate(rms_norm(h, norm2_g[layer]), sh2, sc2), ffn_w_gu[layer], ffn_w_down[layer])
        if need_ctx:
            hc = hc + cg1 * (jnp.concatenate([d_c, q_c, n_c], axis=-1) @ w_out[layer])
            hc = hc + cg2 * swiglu(modulate(rms_norm(hc, norm2_g[layer]), csh2, csc2), ffn_w_gu[layer], ffn_w_down[layer])
    return rms_norm(h, final_norm_g)
```

```python
import functools
import math

import jax
import jax.numpy as jnp
from jax import lax
from jax.experimental import pallas as pl
from jax.experimental.pallas import tpu as pltpu

F32 = jnp.float32
BF16 = jnp.bfloat16

LANES = 128
D_MODEL = 1024
GRID_W = 64
ROPE_THETA = 10000.0
NORM_EPS = 1e-6
L2_EPS = 1e-6

DIFF_HEADS = 4
DIFF_QK_DIM = 32
DIFF_V_DIM = 64
DIFF_WIDTH = DIFF_HEADS * DIFF_V_DIM
GQA_Q_HEADS = 6
GQA_KV_HEADS = 2
GQA_HEAD_DIM = 64
GQA_WIDTH = GQA_Q_HEADS * GQA_HEAD_DIM
GQA_KV_WIDTH = GQA_KV_HEADS * GQA_HEAD_DIM
GDN_HEADS = 6
GDN_HEAD_DIM = 64
GDN_WIDTH = GDN_HEADS * GDN_HEAD_DIM
GDN_CONV = 5
GDN_CHUNK = 64
GDN_PAIRS = GDN_HEADS // 2
FFN_HIDDEN = 2816

OFF_DQ = 0
OFF_DK = OFF_DQ + DIFF_WIDTH
OFF_DV = OFF_DK + DIFF_WIDTH
OFF_GQ = OFF_DV + DIFF_WIDTH
OFF_GK = OFF_GQ + GQA_WIDTH
OFF_GV = OFF_GK + GQA_KV_WIDTH
OFF_NQKV = OFF_GV + GQA_KV_WIDTH
OFF_NZ = OFF_NQKV + 3 * GDN_WIDTH
OFF_NA = OFF_NZ + GDN_WIDTH
OFF_NB = OFF_NA + LANES
IN_PAD = OFF_NB + LANES

GQA_Q_ORDER = (0, 3, 1, 4, 2, 5)

VMEM_LIMIT = 56 * 1024 * 1024


def _cparams(sem):
    return pltpu.CompilerParams(dimension_semantics=sem, vmem_limit_bytes=VMEM_LIMIT)


def _dot(a, b):
    return jnp.dot(a, b, preferred_element_type=F32)


def _dot_nt(a, b):
    return lax.dot_general(a, b, (((1,), (1,)), ((), ())), preferred_element_type=F32)


def _split_bf16(x):
    hi = x.astype(BF16)
    lo = (x - hi.astype(F32)).astype(BF16)
    return hi, lo


def _dot3(a, b):
    ah, al = _split_bf16(a)
    bh, bl = _split_bf16(b)
    return _dot(ah, bh) + _dot(ah, bl) + _dot(al, bh)


def _group_matrix(scale):
    r = lax.broadcasted_iota(jnp.int32, (LANES, LANES), 0) // 64
    c = lax.broadcasted_iota(jnp.int32, (LANES, LANES), 1) // 64
    return jnp.where(r == c, scale, 0.0).astype(BF16)


def _group_sum(x, gm):
    hi, lo = _split_bf16(x)
    return _dot(hi, gm) + _dot(lo, gm)


def _rope(x, c, sa, sb, half):
    return x * c + pltpu.roll(x, LANES - half, 1) * sa + pltpu.roll(x, half, 1) * sb


def _ada_kernel(c_ref, w_ref, b_ref, o_ref):
    s = jax.nn.silu(c_ref[...]).astype(BF16)
    o_ref[...] = _dot(s, w_ref[...].astype(BF16)) + b_ref[...]


def _ada(cond, ada_w, ada_b):
    depth, d, n = ada_w.shape
    rows = cond.shape[0]
    tn = 1536
    return pl.pallas_call(
        _ada_kernel,
        grid=(depth, n // tn),
        in_specs=[
            pl.BlockSpec((rows, d), lambda l, j: (0, 0)),
            pl.BlockSpec((None, d, tn), lambda l, j: (l, 0, j)),
            pl.BlockSpec((None, 1, tn), lambda l, j: (l, 0, j)),
        ],
        out_specs=pl.BlockSpec((None, rows, tn), lambda l, j: (l, 0, j)),
        out_shape=jax.ShapeDtypeStruct((depth, rows, n), F32),
        compiler_params=_cparams(("parallel", "parallel")),
        name="ada_mod",
    )(cond, ada_w, ada_b.reshape(depth, 1, n))


def _in_proj_kernel(*refs, use_rope):
    if use_rope:
        (x_ref, g_ref, sh_ref, sc_ref, w_ref, qg_ref, kg_ref,
         cd_ref, sad_ref, sbd_ref, cg_ref, sag_ref, sbg_ref,
         dq_ref, dk_ref, dv_ref, gq_ref, gk_ref, gv_ref, nqkv_ref, nz_ref, na_ref, nb_ref) = refs
    else:
        (x_ref, g_ref, sh_ref, sc_ref, w_ref, qg_ref, kg_ref,
         dq_ref, dk_ref, dv_ref, gq_ref, gk_ref, gv_ref, nqkv_ref, nz_ref, na_ref, nb_ref) = refs
    x = x_ref[...]
    y = x * lax.rsqrt(jnp.mean(x * x, axis=-1, keepdims=True) + NORM_EPS) * g_ref[...]
    a = (y * (1.0 + sc_ref[...]) + sh_ref[...]).astype(BF16)

    def proj(lo, width):
        return _dot(a, w_ref[:, lo:lo + width])

    gm = _group_matrix(1.0 / GQA_HEAD_DIM)

    for off, out in ((OFF_DQ, dq_ref), (OFF_DK, dk_ref)):
        for blk in range(DIFF_WIDTH // LANES):
            p = proj(off + blk * LANES, LANES)
            if use_rope:
                p = _rope(p, cd_ref[...], sad_ref[...], sbd_ref[...], DIFF_QK_DIM // 4)
            out[:, blk * LANES:(blk + 1) * LANES] = p.astype(BF16)
    dv_ref[...] = proj(OFF_DV, DIFF_WIDTH).astype(BF16)

    def qk_prep(p, gain, scale):
        ms = _group_sum(p * p, gm)
        p = p * lax.rsqrt(ms + NORM_EPS) * gain
        if use_rope:
            p = _rope(p, cg_ref[...], sag_ref[...], sbg_ref[...], GQA_HEAD_DIM // 4)
        return (p * scale).astype(BF16)

    for blk in range(GQA_WIDTH // LANES):
        p = proj(OFF_GQ + blk * LANES, LANES)
        gq_ref[:, blk * LANES:(blk + 1) * LANES] = qk_prep(p, qg_ref[...], GQA_HEAD_DIM ** -0.5)
    gk_ref[...] = qk_prep(proj(OFF_GK, LANES), kg_ref[...], 1.0)
    gv_ref[...] = proj(OFF_GV, LANES).astype(BF16)

    for blk in range(3 * GDN_WIDTH // 384):
        nqkv_ref[:, blk * 384:(blk + 1) * 384] = proj(OFF_NQKV + blk * 384, 384)
    nz_ref[...] = proj(OFF_NZ, GDN_WIDTH)
    na_ref[...] = proj(OFF_NA, LANES)
    nb_ref[...] = proj(OFF_NB, LANES)


def _in_proj(x, g, sh, sc, w, qg, kg, rope):
    b, t, d = x.shape
    tm = min(512, t)
    nb = sh.shape[0]
    mod_map = (lambda i, j: (i, 0, 0)) if nb == b else (lambda i, j: (0, 0, 0))
    tok = lambda width: pl.BlockSpec((None, tm, width), lambda i, j: (i, j, 0))
    const2 = lambda shape: pl.BlockSpec(shape, lambda i, j: (0, 0))
    in_specs = [
        tok(d), const2((1, d)),
        pl.BlockSpec((None, 1, d), mod_map), pl.BlockSpec((None, 1, d), mod_map),
        const2((d, IN_PAD)), const2((1, LANES)), const2((1, LANES)),
    ]
    args = [x, g, sh, sc, w, qg, kg]
    if rope is not None:
        in_specs += [pl.BlockSpec((tm, LANES), lambda i, j: (j, 0))] * 6
        args += list(rope)
    widths = (DIFF_WIDTH, DIFF_WIDTH, DIFF_WIDTH, GQA_WIDTH, GQA_KV_WIDTH, GQA_KV_WIDTH,
              3 * GDN_WIDTH, GDN_WIDTH, LANES, LANES)
    dtypes = (BF16,) * 6 + (F32,) * 4
    return pl.pallas_call(
        functools.partial(_in_proj_kernel, use_rope=rope is not None),
        grid=(b, t // tm),
        in_specs=in_specs,
        out_specs=[tok(wd) for wd in widths],
        out_shape=[jax.ShapeDtypeStruct((b, t, wd), dt) for wd, dt in zip(widths, dtypes)],
        compiler_params=_cparams(("parallel", "parallel")),
        name="in_proj",
    )(*args)


def _flash(qm, kc_ref, vc_ref, kl_ref, vl_ref, col, n_lat, tk, scale):
    cols = slice(col, col + LANES)

    def scores(k):
        s = _dot_nt(qm, k)
        return s if scale is None else s * scale

    s = scores(kc_ref[:, cols])
    m = jnp.max(s, axis=-1, keepdims=True)
    p = jnp.exp(s - m)
    l = jnp.sum(p, axis=-1, keepdims=True)
    acc = _dot(p.astype(BF16), vc_ref[:, cols])
    if n_lat == 0:
        return acc, l

    def body(j, carry):
        m, l, acc = carry
        r0 = pl.multiple_of(j * tk, tk)
        s = scores(kl_ref[pl.ds(r0, tk), cols])
        m_new = jnp.maximum(m, jnp.max(s, axis=-1, keepdims=True))
        alpha = jnp.exp(m - m_new)
        p = jnp.exp(s - m_new)
        l = alpha * l + jnp.sum(p, axis=-1, keepdims=True)
        acc = alpha * acc + _dot(p.astype(BF16), vl_ref[pl.ds(r0, tk), cols])
        return m_new, l, acc

    _, l, acc = lax.fori_loop(0, n_lat, body, (m, l, acc))
    return acc, l


def _gqa_kernel(*refs, n_lat, tk):
    if n_lat:
        q_ref, kc_ref, vc_ref, kl_ref, vl_ref, o_ref = refs
    else:
        q_ref, kc_ref, vc_ref, o_ref = refs
        kl_ref = vl_ref = None
    tq = q_ref.shape[0]
    lo = lax.broadcasted_iota(jnp.int32, (tq, LANES), 1) < GQA_HEAD_DIM
    for blk in range(GQA_WIDTH // LANES):
        qb = q_ref[:, blk * LANES:(blk + 1) * LANES]
        zero = jnp.zeros_like(qb)
        acc_a, l_a = _flash(jnp.where(lo, qb, zero), kc_ref, vc_ref, kl_ref, vl_ref, 0, n_lat, tk, None)
        acc_b, l_b = _flash(jnp.where(lo, zero, qb), kc_ref, vc_ref, kl_ref, vl_ref, 0, n_lat, tk, None)
        o = jnp.where(lo, acc_a / l_a, acc_b / l_b)
        o_ref[:, blk * LANES:(blk + 1) * LANES] = o.astype(BF16)


def _diff_kernel(*refs, n_lat, tk, lambda_init):
    if n_lat:
        q_ref, kc_ref, vc_ref, kl_ref, vl_ref, lam_ref, ng_ref, o_ref = refs
    else:
        q_ref, kc_ref, vc_ref, lam_ref, ng_ref, o_ref = refs
        kl_ref = vl_ref = None
    tq = q_ref.shape[0]
    lf = lam_ref[...]
    lam = (jnp.exp(jnp.sum(lf[0:1] * lf[1:2], axis=-1, keepdims=True))
           - jnp.exp(jnp.sum(lf[2:3] * lf[3:4], axis=-1, keepdims=True)) + lambda_init)
    lane = lax.broadcasted_iota(jnp.int32, (tq, LANES), 1)
    lo = lane < DIFF_V_DIM
    gm = _group_matrix(1.0 / DIFF_V_DIM)
    scale = DIFF_QK_DIM ** -0.5
    for blk in range(DIFF_WIDTH // LANES):
        col = blk * LANES
        qb = q_ref[:, col:col + LANES]
        zero = jnp.zeros_like(qb)
        halves = []
        for s in range(2):
            comps = []
            for c in range(2):
                sel = (lane // DIFF_QK_DIM) == (2 * s + c)
                acc, l = _flash(jnp.where(sel, qb, zero), kc_ref, vc_ref, kl_ref, vl_ref, col, n_lat, tk, scale)
                comps.append(acc / l)
            halves.append(comps[0] - lam * comps[1])
        o = jnp.where(lo, halves[0], halves[1])
        ms = _group_sum(o * o, gm)
        o = o * lax.rsqrt(ms + NORM_EPS) * ng_ref[...] * (1.0 - lambda_init)
        o_ref[:, col:col + LANES] = o.astype(BF16)


def _attention(kind, q, kc, vc, kl, vl, extra, lambda_init=None):
    b, t, w = q.shape
    tc, wk = kc.shape[1], kc.shape[2]
    tq = min(256, t)
    full = lambda n, width: pl.BlockSpec((None, n, width), lambda i, j: (i, 0, 0))
    in_specs = [pl.BlockSpec((None, tq, w), lambda i, j: (i, j, 0)), full(tc, wk), full(tc, wk)]
    args = [q, kc, vc]
    n_lat, tk = 0, 0
    if kl is not None:
        tl = kl.shape[1]
        tk = min(512, tl)
        n_lat = tl // tk
        in_specs += [full(tl, wk), full(tl, wk)]
        args += [kl, vl]
    for e in extra:
        in_specs.append(pl.BlockSpec(e.shape, lambda i, j: (0, 0)))
        args.append(e)
    if kind == "gqa":
        body = functools.partial(_gqa_kernel, n_lat=n_lat, tk=tk)
    else:
        body = functools.partial(_diff_kernel, n_lat=n_lat, tk=tk, lambda_init=lambda_init)
    return pl.pallas_call(
        body,
        grid=(b, t // tq),
        in_specs=in_specs,
        out_specs=pl.BlockSpec((None, tq, w), lambda i, j: (i, j, 0)),
        out_shape=jax.ShapeDtypeStruct((b, t, w), BF16),
        compiler_params=_cparams(("parallel", "parallel")),
        name=kind + "_attn",
    )(*args)


def _gdn_local_kernel(x_ref, xp_ref, xn_ref, cw_ref, na_ref, nb_ref, alog_ref, dtb_ref,
                      uf_ref, wf_ref, qdf_ref, kdf_ref, qkf_ref, glf_ref,
                      ub_ref, wb_ref, qdb_ref, kdb_ref, qkb_ref, glb_ref,
                      xbuf, q_s, k_s, v_s, g_s, b_s):
    tm = x_ref.shape[0]
    cs = GDN_CHUNK
    i = pl.program_id(1)
    nblk = pl.num_programs(1)
    halo = xp_ref.shape[0]
    xbuf[halo:halo + tm, :] = x_ref[...]
    xbuf[0:halo, :] = jnp.where(i > 0, xp_ref[...], 0.0)
    xbuf[halo + tm:2 * halo + tm, :] = jnp.where(i < nblk - 1, xn_ref[...], 0.0)
    gm = _group_matrix(1.0)
    for part, dst in enumerate((q_s, k_s, v_s)):
        cols = slice(part * GDN_WIDTH, (part + 1) * GDN_WIDTH)
        acc = None
        for j in range(GDN_CONV):
            start = halo - GDN_CONV // 2 + j
            term = xbuf[start:start + tm, cols] * cw_ref[j:j + 1, cols]
            acc = term if acc is None else acc + term
        y = jax.nn.silu(acc)
        if part < 2:
            scale = GDN_HEAD_DIM ** -0.5 if part == 0 else 1.0
            for blk in range(GDN_PAIRS):
                yb = y[:, blk * LANES:(blk + 1) * LANES]
                ss = _group_sum(yb * yb, gm)
                dst[:, blk * LANES:(blk + 1) * LANES] = yb * lax.rsqrt(ss + L2_EPS) * scale
        else:
            dst[...] = y
    xa = na_ref[...] + dtb_ref[...]
    softplus = jnp.maximum(xa, 0.0) + jnp.log(1.0 + jnp.exp(-jnp.abs(xa)))
    g_s[...] = -jnp.exp(alog_ref[...]) * softplus
    b_s[...] = jax.nn.sigmoid(nb_ref[...])

    ri = lax.broadcasted_iota(jnp.int32, (cs, LANES), 0)
    li = lax.broadcasted_iota(jnp.int32, (cs, LANES), 1)
    lo = li < 64
    tj = li % 64
    r64 = lax.broadcasted_iota(jnp.int32, (cs, cs), 0)
    c64 = lax.broadcasted_iota(jnp.int32, (cs, cs), 1)
    tri_lo = jnp.where(r64 >= c64, 1.0, 0.0).astype(BF16)
    tri_up = jnp.where(r64 <= c64, 1.0, 0.0).astype(BF16)
    r128 = lax.broadcasted_iota(jnp.int32, (LANES, LANES), 0)
    c128 = lax.broadcasted_iota(jnp.int32, (LANES, LANES), 1)
    eye = jnp.where(r128 == c128, 1.0, 0.0)
    lo1 = lax.broadcasted_iota(jnp.int32, (1, LANES), 1) < 64

    def stack(x, zero):
        return jnp.concatenate([jnp.where(lo, x, zero), jnp.where(lo, zero, x)], axis=0)

    def chunk(c, carry):
        r0 = pl.multiple_of(c * cs, cs)
        rows = pl.ds(r0, cs)
        g = g_s[rows, :]
        be = b_s[rows, :]
        g1, g2, g3 = None, None, None
        gh = g.astype(BF16)
        r1 = g - gh.astype(F32)
        gmid = r1.astype(BF16)
        glo = (r1 - gmid.astype(F32)).astype(BF16)
        cum_f = _dot(tri_lo, gh) + _dot(tri_lo, gmid) + _dot(tri_lo, glo)
        cum_b = _dot(tri_up, gh) + _dot(tri_up, gmid) + _dot(tri_up, glo)
        gc = jnp.where(li < GDN_HEADS, cum_f, cum_b)
        gt = jnp.concatenate([gc, gc], axis=0).T
        for p in range(GDN_PAIRS):
            cols = slice(p * LANES, (p + 1) * LANES)
            q128 = q_s[rows, cols]
            k128 = k_s[rows, cols]
            v128 = v_s[rows, cols]
            kb = k128.astype(BF16)
            kstack = stack(kb, jnp.zeros_like(kb))
            kk = _dot_nt(kb, kstack)
            qk = _dot_nt(q128.astype(BF16), kstack)
            for rev, (u_ref, w_ref, qd_ref, kd_ref, qk_ref, gl_ref) in enumerate((
                    (uf_ref, wf_ref, qdf_ref, kdf_ref, qkf_ref, glf_ref),
                    (ub_ref, wb_ref, qdb_ref, kdb_ref, qkb_ref, glb_ref))):
                la = rev * GDN_HEADS + 2 * p
                gca, gcb = gc[:, la:la + 1], gc[:, la + 1:la + 2]
                bca, bcb = be[:, la:la + 1], be[:, la + 1:la + 2]
                gcol = jnp.where(lo, gca, gcb)
                grow = jnp.where(lo1, gt[la:la + 1, :], gt[la + 1:la + 2, :])
                bcol = jnp.where(lo, bca, bcb)
                if rev:
                    incl, strict = ri <= tj, ri < tj
                    last = 0
                else:
                    incl, strict = ri >= tj, ri > tj
                    last = cs - 1
                decay = jnp.exp(jnp.where(incl, gcol - grow, -jnp.inf))
                a128 = jnp.where(strict, kk * decay * bcol, 0.0)
                n = -stack(a128, 0.0)
                t = eye + n
                pw = _dot3(n, n)
                for it in range(5):
                    t = t + _dot3(t, pw)
                    if it < 4:
                        pw = _dot3(pw, pw)
                ea, eb = jnp.exp(gca), jnp.exp(gcb)
                rhs_v = jnp.concatenate([v128 * bca, v128 * bcb], axis=0)
                rhs_k = jnp.concatenate([k128 * (bca * ea), k128 * (bcb * eb)], axis=0)
                su = _dot3(t, rhs_v)
                sw = _dot3(t, rhs_k)
                u_ref[rows, cols] = jnp.where(lo, su[0:cs], su[cs:2 * cs])
                w_ref[rows, cols] = jnp.where(lo, sw[0:cs], sw[cs:2 * cs]).astype(BF16)
                qd_ref[rows, cols] = (q128 * jnp.exp(gcol)).astype(BF16)
                glast = jnp.where(lo1, gc[last:last + 1, la:la + 1], gc[last:last + 1, la + 1:la + 2])
                kd_ref[rows, cols] = (k128 * jnp.exp(glast - gcol)).astype(BF16)
                qk_ref[rows, cols] = (qk * decay).astype(BF16)
                gl_ref[c, :, cols] = jnp.broadcast_to(jnp.exp(glast), (8, LANES))
        return carry

    lax.fori_loop(0, tm // cs, chunk, 0)


def _gdn_local(nqkv, na, nb, conv_w, alog, dtb):
    b, t, w3 = nqkv.shape
    tm = min(256, t)
    nblk = t // tm
    cpb = tm // GDN_CHUNK
    halo = 8
    hb = tm // halo
    tok = lambda width: pl.BlockSpec((None, tm, width), lambda i, j: (i, j, 0))
    const2 = lambda shape: pl.BlockSpec(shape, lambda i, j: (0, 0))
    in_specs = [
        tok(w3),
        pl.BlockSpec((None, halo, w3), lambda i, j: (i, jnp.maximum(j * hb - 1, 0), 0)),
        pl.BlockSpec((None, halo, w3), lambda i, j: (i, jnp.minimum((j + 1) * hb, t // halo - 1), 0)),
        const2((GDN_CONV, w3)), tok(LANES), tok(LANES), const2((1, LANES)), const2((1, LANES)),
    ]
    per_dir_specs = [tok(GDN_WIDTH)] * 5 + [pl.BlockSpec((None, cpb, 8, GDN_WIDTH), lambda i, j: (i, j, 0, 0))]
    per_dir_shapes = ([jax.ShapeDtypeStruct((b, t, GDN_WIDTH), F32)]
                      + [jax.ShapeDtypeStruct((b, t, GDN_WIDTH), BF16)] * 4
                      + [jax.ShapeDtypeStruct((b, t // GDN_CHUNK, 8, GDN_WIDTH), F32)])
    outs = pl.pallas_call(
        _gdn_local_kernel,
        grid=(b, nblk),
        in_specs=in_specs,
        out_specs=per_dir_specs * 2,
        out_shape=per_dir_shapes * 2,
        scratch_shapes=[
            pltpu.VMEM((tm + 2 * halo, w3), F32),
            pltpu.VMEM((tm, GDN_WIDTH), F32), pltpu.VMEM((tm, GDN_WIDTH), F32), pltpu.VMEM((tm, GDN_WIDTH), F32),
            pltpu.VMEM((tm, LANES), F32), pltpu.VMEM((tm, LANES), F32),
        ],
        compiler_params=_cparams(("parallel", "parallel")),
        name="gdn_local",
    )(nqkv, nqkv, nqkv, conv_w, na, nb, alog, dtb)
    return outs[:6], outs[6:]


def _gdn_scan_kernel(uf_ref, wf_ref, qdf_ref, kdf_ref, qkf_ref, glf_ref,
                     ub_ref, wb_ref, qdb_ref, kdb_ref, qkb_ref, glb_ref,
                     s0f_ref, s0b_ref,
                     of_ref, ob_ref, sff_ref, sfb_ref,
                     stf, stb):
    n = pl.program_id(1)
    cs = GDN_CHUNK
    cpb = uf_ref.shape[0] // cs

    @pl.when(n == 0)
    def _():
        stf[...] = s0f_ref[...]
        stb[...] = s0b_ref[...]

    lo = lax.broadcasted_iota(jnp.int32, (cs, LANES), 1) < 64
    r128 = lax.broadcasted_iota(jnp.int32, (LANES, LANES), 0) // 64
    c128 = lax.broadcasted_iota(jnp.int32, (LANES, LANES), 1) // 64
    bd = r128 == c128

    def step(c, u_ref, w_ref, qd_ref, kd_ref, qk_ref, gl_ref, o_ref, st):
        rows = slice(c * cs, (c + 1) * cs)
        for p in range(GDN_PAIRS):
            cols = slice(p * LANES, (p + 1) * LANES)
            s = st[p]
            sb = s.astype(BF16)
            vnew = u_ref[rows, cols] - _dot_nt(w_ref[rows, cols], sb)
            vb = vnew.astype(BF16)
            zero = jnp.zeros_like(vb)
            vstack = jnp.concatenate([jnp.where(lo, vb, zero), jnp.where(lo, zero, vb)], axis=0)
            o_ref[rows, cols] = _dot_nt(qd_ref[rows, cols], sb) + _dot(qk_ref[rows, cols], vstack)
            upd = _dot(vnew.T.astype(BF16), kd_ref[rows, cols])
            st[p] = s * gl_ref[c, 0:1, cols] + jnp.where(bd, upd, 0.0)

    for cc in range(cpb):
        step(cc, uf_ref, wf_ref, qdf_ref, kdf_ref, qkf_ref, glf_ref, of_ref, stf)
        step(cpb - 1 - cc, ub_ref, wb_ref, qdb_ref, kdb_ref, qkb_ref, glb_ref, ob_ref, stb)

    @pl.when(n == pl.num_programs(1) - 1)
    def _():
        sff_ref[...] = stf[...]
        sfb_ref[...] = stb[...]


def _gdn_scan(fwd, bwd, s0f, s0b):
    b, t, _ = fwd[0].shape
    tm = min(256, t)
    nblk = t // tm
    cpb = tm // GDN_CHUNK
    f_tok = pl.BlockSpec((None, tm, GDN_WIDTH), lambda i, j: (i, j, 0))
    b_tok = pl.BlockSpec((None, tm, GDN_WIDTH), lambda i, j: (i, nblk - 1 - j, 0))
    f_gl = pl.BlockSpec((None, cpb, 8, GDN_WIDTH), lambda i, j: (i, j, 0, 0))
    b_gl = pl.BlockSpec((None, cpb, 8, GDN_WIDTH), lambda i, j: (i, nblk - 1 - j, 0, 0))
    st_spec = pl.BlockSpec((None, GDN_PAIRS, LANES, LANES), lambda i, j: (i, 0, 0, 0))
    st_shape = jax.ShapeDtypeStruct((b, GDN_PAIRS, LANES, LANES), F32)
    o_shape = jax.ShapeDtypeStruct((b, t, GDN_WIDTH), F32)
    return pl.pallas_call(
        _gdn_scan_kernel,
        grid=(b, nblk),
        in_specs=[f_tok] * 5 + [f_gl] + [b_tok] * 5 + [b_gl] + [st_spec, st_spec],
        out_specs=[f_tok, b_tok, st_spec, st_spec],
        out_shape=[o_shape, o_shape, st_shape, st_shape],
        scratch_shapes=[pltpu.VMEM((GDN_PAIRS, LANES, LANES), F32)] * 2,
        compiler_params=_cparams(("parallel", "arbitrary")),
        name="gdn_scan",
    )(*fwd, *bwd, s0f, s0b)


def _out_proj_kernel(h_ref, d_ref, gq_ref, of_ref, ob_ref, z_ref, ng_ref, gate_ref, w_ref, o_ref):
    gm = _group_matrix(1.0 / GDN_HEAD_DIM)
    y = _dot(d_ref[...], w_ref[0:DIFF_WIDTH, :])
    y += _dot(gq_ref[...], w_ref[DIFF_WIDTH:DIFF_WIDTH + GQA_WIDTH, :])
    base = DIFF_WIDTH + GQA_WIDTH
    for blk in range(GDN_PAIRS):
        cols = slice(blk * LANES, (blk + 1) * LANES)
        o = of_ref[:, cols] + ob_ref[:, cols]
        ms = _group_sum(o * o, gm)
        r = o * lax.rsqrt(ms + NORM_EPS) * ng_ref[...] * jax.nn.silu(z_ref[:, cols])
        y += _dot(r.astype(BF16), w_ref[base + blk * LANES:base + (blk + 1) * LANES, :])
    o_ref[...] = h_ref[...] + gate_ref[...] * y


def _out_proj(h, d, gq, of, ob, z, ng, gate, w):
    b, t, dm = h.shape
    tm = min(512, t)
    nb = gate.shape[0]
    mod_map = (lambda i, j: (i, 0, 0)) if nb == b else (lambda i, j: (0, 0, 0))
    tok = lambda width: pl.BlockSpec((None, tm, width), lambda i, j: (i, j, 0))
    return pl.pallas_call(
        _out_proj_kernel,
        grid=(b, t // tm),
        in_specs=[tok(dm), tok(DIFF_WIDTH), tok(GQA_WIDTH), tok(GDN_WIDTH), tok(GDN_WIDTH), tok(GDN_WIDTH),
                  pl.BlockSpec((1, LANES), lambda i, j: (0, 0)),
                  pl.BlockSpec((None, 1, dm), mod_map),
                  pl.BlockSpec(w.shape, lambda i, j: (0, 0))],
        out_specs=tok(dm),
        out_shape=jax.ShapeDtypeStruct((b, t, dm), F32),
        compiler_params=_cparams(("parallel", "parallel")),
        name="out_proj",
    )(h, d, gq, of, ob, z, ng, gate, w)


def _ffn_kernel(h_ref, g_ref, sh_ref, sc_ref, gate_ref, wg_ref, wu_ref, wd_ref, fg_ref, o_ref, a_s, acc_s,
                *, final_norm):
    j = pl.program_id(2)

    @pl.when(j == 0)
    def _():
        x = h_ref[...]
        y = x * lax.rsqrt(jnp.mean(x * x, axis=-1, keepdims=True) + NORM_EPS) * g_ref[...]
        a_s[...] = (y * (1.0 + sc_ref[...]) + sh_ref[...]).astype(BF16)
        acc_s[...] = jnp.zeros_like(acc_s)

    a = a_s[...]
    act = jax.nn.silu(_dot(a, wg_ref[...])) * _dot(a, wu_ref[...])
    acc_s[...] += _dot(act.astype(BF16), wd_ref[...])

    @pl.when(j == pl.num_programs(2) - 1)
    def _():
        out = h_ref[...] + gate_ref[...] * acc_s[...]
        if final_norm:
            out = out * lax.rsqrt(jnp.mean(out * out, axis=-1, keepdims=True) + NORM_EPS) * fg_ref[...]
        o_ref[...] = out


def _ffn(h, g, sh, sc, gate, w_gu, w_down, fg, final_norm):
    b, t, dm = h.shape
    hidden = w_down.shape[0]
    tm = min(1024, t)
    th = 256
    nh = hidden // th
    nb = sh.shape[0]
    mod_map = (lambda i, k, j: (i, 0, 0)) if nb == b else (lambda i, k, j: (0, 0, 0))
    tok = pl.BlockSpec((None, tm, dm), lambda i, k, j: (i, k, 0))
    vec = pl.BlockSpec((1, dm), lambda i, k, j: (0, 0))
    mod = pl.BlockSpec((None, 1, dm), mod_map)
    return pl.pallas_call(
        functools.partial(_ffn_kernel, final_norm=final_norm),
        grid=(b, t // tm, nh),
        in_specs=[tok, vec, mod, mod, mod,
                  pl.BlockSpec((dm, th), lambda i, k, j: (0, j)),
                  pl.BlockSpec((dm, th), lambda i, k, j: (0, j + nh)),
                  pl.BlockSpec((th, dm), lambda i, k, j: (j, 0)),
                  vec],
        out_specs=tok,
        out_shape=jax.ShapeDtypeStruct((b, t, dm), F32),
        scratch_shapes=[pltpu.VMEM((tm, dm), BF16), pltpu.VMEM((tm, dm), F32)],
        compiler_params=_cparams(("parallel", "parallel", "arbitrary")),
        name="ffn",
    )(h, g, sh, sc, gate, w_gu, w_gu, w_down, fg)


def _rope_tables(t, rot_dim):
    nf = rot_dim // 4
    pos = jnp.arange(t)
    row = (pos // GRID_W).astype(F32)
    col = (pos % GRID_W).astype(F32)
    inv_freq = ROPE_THETA ** (-jnp.arange(nf, dtype=F32) / nf)
    ar, ac = row[:, None] * inv_freq, col[:, None] * inv_freq
    z = jnp.zeros_like(ar)
    cos = jnp.concatenate([jnp.cos(ar), jnp.cos(ar), jnp.cos(ac), jnp.cos(ac)], axis=1)
    sa = jnp.concatenate([-jnp.sin(ar), z, -jnp.sin(ac), z], axis=1)
    sb = jnp.concatenate([z, jnp.sin(ar), z, jnp.sin(ac)], axis=1)
    rep = LANES // rot_dim
    return tuple(jnp.tile(a, (1, rep)) for a in (cos, sa, sb))


def _pad_lanes(v):
    return jnp.pad(v.reshape(1, -1), ((0, 0), (0, LANES - v.size)))


def _relayout_w_in(w):
    o = 0
    pieces = []
    for width in (DIFF_WIDTH, DIFF_WIDTH, DIFF_WIDTH):
        pieces.append(w[:, o:o + width]); o += width
    gq = w[:, o:o + GQA_WIDTH].reshape(-1, GQA_Q_HEADS, GQA_HEAD_DIM); o += GQA_WIDTH
    pieces.append(jnp.take(gq, jnp.array(GQA_Q_ORDER), axis=1).reshape(-1, GQA_WIDTH))
    for width in (GQA_KV_WIDTH, GQA_KV_WIDTH, 3 * GDN_WIDTH, GDN_WIDTH):
        pieces.append(w[:, o:o + width]); o += width
    for width in (2 * GDN_HEADS, 2 * GDN_HEADS):
        pieces.append(jnp.pad(w[:, o:o + width], ((0, 0), (0, LANES - width)))); o += width
    return jnp.concatenate(pieces, axis=1).astype(BF16)


def _relayout_w_out(w):
    gq = w[DIFF_WIDTH:DIFF_WIDTH + GQA_WIDTH].reshape(GQA_Q_HEADS, GQA_HEAD_DIM, -1)
    gq = jnp.take(gq, jnp.array(GQA_Q_ORDER), axis=0).reshape(GQA_WIDTH, -1)
    return jnp.concatenate([w[:DIFF_WIDTH], gq, w[DIFF_WIDTH + GQA_WIDTH:]], axis=0).astype(BF16)


def kernel(x, c, ctx, c_ctx, norm1_g, ada_w, ada_b, w_in, diff_lambda, diff_norm_g, q_norm_g, k_norm_g,
           gdn_conv_w, gdn_a_log, gdn_dt_bias, gdn_norm_g, w_out, norm2_g, ffn_w_gu, ffn_w_down, final_norm_g):
    b, t, d = x.shape
    depth = w_in.shape[0]
    rope = _rope_tables(t, DIFF_QK_DIM) + _rope_tables(t, GQA_HEAD_DIM)

    cond = jnp.concatenate([c, c_ctx[None, :], jnp.zeros((16 - b - 1, d), F32)], axis=0)
    mod = _ada(cond, ada_w, ada_b).reshape(depth, 16, 6, d)

    tile2 = lambda v: jnp.tile(v.reshape(1, -1), (1, LANES // v.size))
    zeros_state = jnp.zeros((b, GDN_PAIRS, LANES, LANES), F32)

    h, hc = x, ctx
    for layer in range(depth):
        need_ctx = layer < depth - 1
        lambda_init = 0.8 - 0.6 * math.exp(-0.3 * layer)
        mod_l = [mod[layer, :b, k][:, None, :] for k in range(6)]
        mod_c = [mod[layer, b:b + 1, k][:, None, :] for k in range(6)]
        w_in_l = _relayout_w_in(w_in[layer])
        w_out_l = _relayout_w_out(w_out[layer])
        w_gu_l = ffn_w_gu[layer].astype(BF16)
        w_down_l = ffn_w_down[layer].astype(BF16)
        g1 = norm1_g[layer].reshape(1, d)
        g2 = norm2_g[layer].reshape(1, d)
        qg, kg = tile2(q_norm_g[layer]), tile2(k_norm_g[layer])
        dng, nng = tile2(diff_norm_g[layer]), tile2(gdn_norm_g[layer])
        alog, dtb = _pad_lanes(gdn_a_log[layer]), _pad_lanes(gdn_dt_bias[layer])
        lam = diff_lambda[layer]
        conv_w = gdn_conv_w[layer]

        pl_ = _in_proj(h, g1, mod_l[0], mod_l[1], w_in_l, qg, kg, rope)
        pc_ = _in_proj(hc, g1, mod_c[0], mod_c[1], w_in_l, qg, kg, None)
        dq_l, dk_l, dv_l, gq_l, gk_l, gv_l, nqkv_l, nz_l, na_l, nb_l = pl_
        dq_c, dk_c, dv_c, gq_c, gk_c, gv_c, nqkv_c, nz_c, na_c, nb_c = pc_

        d_l = _attention("diff", dq_l, dk_c, dv_c, dk_l, dv_l, (lam, dng), lambda_init)
        a_l = _attention("gqa", gq_l, gk_c, gv_c, gk_l, gv_l, ())

        fwd_c, bwd_c = _gdn_local(nqkv_c, na_c, nb_c, conv_w, alog, dtb)
        fwd_l, bwd_l = _gdn_local(nqkv_l, na_l, nb_l, conv_w, alog, dtb)
        ocf, ocb, scf, scb = _gdn_scan(fwd_c, bwd_c, zeros_state, zeros_state)
        olf, olb, _, _ = _gdn_scan(fwd_l, bwd_l, scf, scb)

        h = _out_proj(h, d_l, a_l, olf, olb, nz_l, nng, mod_l[2], w_out_l)
        last = layer == depth - 1
        h = _ffn(h, g2, mod_l[3], mod_l[4], mod_l[5], w_gu_l, w_down_l, final_norm_g.reshape(1, d), last)
        if need_ctx:
            d_c = _attention("diff", dq_c, dk_c, dv_c, None, None, (lam, dng), lambda_init)
            a_c = _attention("gqa", gq_c, gk_c, gv_c, None, None, ())
            hc = _out_proj(hc, d_c, a_c, ocf, ocb, nz_c, nng, mod_c[2], w_out_l)
            hc = _ffn(hc, g2, mod_c[3], mod_c[4], mod_c[5], w_gu_l, w_down_l, final_norm_g.reshape(1, d), False)
    return h
```

```python
import functools
import math

import jax
import jax.numpy as jnp
from jax import lax
from jax.experimental import pallas as pl
from jax.experimental.pallas import tpu as pltpu

F32 = jnp.float32
BF16 = jnp.bfloat16

LANES = 128
D_MODEL = 1024
GRID_W = 64
ROPE_THETA = 10000.0
NORM_EPS = 1e-6
L2_EPS = 1e-6
LOG2E = 1.4426950408889634

DIFF_HEADS = 4
DIFF_QK_DIM = 32
DIFF_V_DIM = 64
DIFF_WIDTH = DIFF_HEADS * DIFF_V_DIM
GQA_Q_HEADS = 6
GQA_KV_HEADS = 2
GQA_HEAD_DIM = 64
GQA_WIDTH = GQA_Q_HEADS * GQA_HEAD_DIM
GQA_KV_WIDTH = GQA_KV_HEADS * GQA_HEAD_DIM
GDN_HEADS = 6
GDN_HEAD_DIM = 64
GDN_WIDTH = GDN_HEADS * GDN_HEAD_DIM
GDN_CONV = 5
GDN_CHUNK = 64
GDN_PAIRS = GDN_HEADS // 2
FFN_HIDDEN = 2816

OFF_DQ = 0
OFF_DK = OFF_DQ + DIFF_WIDTH
OFF_DV = OFF_DK + DIFF_WIDTH
OFF_GQ = OFF_DV + DIFF_WIDTH
OFF_GK = OFF_GQ + GQA_WIDTH
OFF_GV = OFF_GK + GQA_KV_WIDTH
OFF_NQKV = OFF_GV + GQA_KV_WIDTH
OFF_NZ = OFF_NQKV + 3 * GDN_WIDTH
OFF_NA = OFF_NZ + GDN_WIDTH
OFF_NB = OFF_NA + LANES
IN_PAD = OFF_NB + LANES

GQA_Q_ORDER = (0, 3, 1, 4, 2, 5)

GDN_INVERSE_PASSES = 1
GDN_SOLVE_PASSES = 1

VMEM_LIMIT = 56 * 1024 * 1024


def _cparams(sem):
    return pltpu.CompilerParams(dimension_semantics=sem, vmem_limit_bytes=VMEM_LIMIT)


def _dot(a, b):
    return jnp.dot(a, b, preferred_element_type=F32)


def _dot_nt(a, b):
    return lax.dot_general(a, b, (((1,), (1,)), ((), ())), preferred_element_type=F32)


def _split_bf16(x):
    hi = x.astype(BF16)
    lo = (x - hi.astype(F32)).astype(BF16)
    return hi, lo


def _dot3(a, b):
    ah, al = _split_bf16(a)
    bh, bl = _split_bf16(b)
    return _dot(ah, bh) + _dot(ah, bl) + _dot(al, bh)


def _dotn(a, b, passes):
    if passes == 1:
        return _dot(a.astype(BF16), b.astype(BF16))
    assert passes == 3
    return _dot3(a, b)


def _group_matrix(scale):
    r = lax.broadcasted_iota(jnp.int32, (LANES, LANES), 0) // 64
    c = lax.broadcasted_iota(jnp.int32, (LANES, LANES), 1) // 64
    return jnp.where(r == c, scale, 0.0).astype(BF16)


def _group_sum(x, gm):
    hi, lo = _split_bf16(x)
    return _dot(hi, gm) + _dot(lo, gm)


def _rope(x, c, sa, sb, half):
    return x * c + pltpu.roll(x, LANES - half, 1) * sa + pltpu.roll(x, half, 1) * sb


def _ada_kernel(c_ref, w_ref, b_ref, o_ref):
    s = jax.nn.silu(c_ref[...]).astype(BF16)
    o_ref[...] = _dot(s, w_ref[...].astype(BF16)) + b_ref[...]


def _ada(cond, ada_w, ada_b):
    depth, d, n = ada_w.shape
    rows = cond.shape[0]
    tn = 1536
    return pl.pallas_call(
        _ada_kernel,
        grid=(depth, n // tn),
        in_specs=[
            pl.BlockSpec((rows, d), lambda l, j: (0, 0)),
            pl.BlockSpec((None, d, tn), lambda l, j: (l, 0, j)),
            pl.BlockSpec((None, 1, tn), lambda l, j: (l, 0, j)),
        ],
        out_specs=pl.BlockSpec((None, rows, tn), lambda l, j: (l, 0, j)),
        out_shape=jax.ShapeDtypeStruct((depth, rows, n), F32),
        compiler_params=_cparams(("parallel", "parallel")),
        name="ada_mod",
    )(cond, ada_w, ada_b.reshape(depth, 1, n))


def _in_proj_kernel(*refs, use_rope):
    if use_rope:
        (x_ref, g_ref, sh_ref, sc_ref, w_ref, qg_ref, kg_ref,
         cd_ref, sad_ref, sbd_ref, cg_ref, sag_ref, sbg_ref,
         dq_ref, dk_ref, dv_ref, gq_ref, gk_ref, gv_ref, nqkv_ref, nz_ref, na_ref, nb_ref) = refs
    else:
        (x_ref, g_ref, sh_ref, sc_ref, w_ref, qg_ref, kg_ref,
         dq_ref, dk_ref, dv_ref, gq_ref, gk_ref, gv_ref, nqkv_ref, nz_ref, na_ref, nb_ref) = refs
    x = x_ref[...]
    y = x * lax.rsqrt(jnp.mean(x * x, axis=-1, keepdims=True) + NORM_EPS) * g_ref[...]
    a = (y * (1.0 + sc_ref[...]) + sh_ref[...]).astype(BF16)

    def proj(lo, width):
        return _dot(a, w_ref[:, lo:lo + width])

    gm = _group_matrix(1.0 / GQA_HEAD_DIM)

    for off, out, scale in ((OFF_DQ, dq_ref, DIFF_QK_DIM ** -0.5 * LOG2E), (OFF_DK, dk_ref, 1.0)):
        for blk in range(DIFF_WIDTH // LANES):
            p = proj(off + blk * LANES, LANES)
            if use_rope:
                p = _rope(p, cd_ref[...], sad_ref[...], sbd_ref[...], DIFF_QK_DIM // 4)
            out[:, blk * LANES:(blk + 1) * LANES] = (p * scale).astype(BF16)
    dv_ref[...] = proj(OFF_DV, DIFF_WIDTH).astype(BF16)

    def qk_prep(p, gain, scale):
        ms = _group_sum(p * p, gm)
        p = p * lax.rsqrt(ms + NORM_EPS) * gain
        if use_rope:
            p = _rope(p, cg_ref[...], sag_ref[...], sbg_ref[...], GQA_HEAD_DIM // 4)
        return (p * scale).astype(BF16)

    for blk in range(GQA_WIDTH // LANES):
        p = proj(OFF_GQ + blk * LANES, LANES)
        gq_ref[:, blk * LANES:(blk + 1) * LANES] = qk_prep(p, qg_ref[...], GQA_HEAD_DIM ** -0.5 * LOG2E)
    gk_ref[...] = qk_prep(proj(OFF_GK, LANES), kg_ref[...], 1.0)
    gv_ref[...] = proj(OFF_GV, LANES).astype(BF16)

    for blk in range(3 * GDN_WIDTH // 384):
        nqkv_ref[:, blk * 384:(blk + 1) * 384] = proj(OFF_NQKV + blk * 384, 384)
    nz_ref[...] = proj(OFF_NZ, GDN_WIDTH)
    na_ref[...] = proj(OFF_NA, LANES)
    nb_ref[...] = proj(OFF_NB, LANES)


def _in_proj(x, g, sh, sc, w, qg, kg, rope):
    b, t, d = x.shape
    tm = min(512, t)
    nb = sh.shape[0]
    mod_map = (lambda i, j: (i, 0, 0)) if nb == b else (lambda i, j: (0, 0, 0))
    tok = lambda width: pl.BlockSpec((None, tm, width), lambda i, j: (i, j, 0))
    const2 = lambda shape: pl.BlockSpec(shape, lambda i, j: (0, 0))
    in_specs = [
        tok(d), const2((1, d)),
        pl.BlockSpec((None, 1, d), mod_map), pl.BlockSpec((None, 1, d), mod_map),
        const2((d, IN_PAD)), const2((1, LANES)), const2((1, LANES)),
    ]
    args = [x, g, sh, sc, w, qg, kg]
    if rope is not None:
        in_specs += [pl.BlockSpec((tm, LANES), lambda i, j: (j, 0))] * 6
        args += list(rope)
    widths = (DIFF_WIDTH, DIFF_WIDTH, DIFF_WIDTH, GQA_WIDTH, GQA_KV_WIDTH, GQA_KV_WIDTH,
              3 * GDN_WIDTH, GDN_WIDTH, LANES, LANES)
    dtypes = (BF16,) * 6 + (F32,) * 4
    return pl.pallas_call(
        functools.partial(_in_proj_kernel, use_rope=rope is not None),
        grid=(b, t // tm),
        in_specs=in_specs,
        out_specs=[tok(wd) for wd in widths],
        out_shape=[jax.ShapeDtypeStruct((b, t, wd), dt) for wd, dt in zip(widths, dtypes)],
        compiler_params=_cparams(("parallel", "parallel")),
        name="in_proj",
    )(*args)


def _flash_all(cols, qm_s, m_s, l_s, acc_s, kc_ref, vc_ref, kl_ref, vl_ref, n_lat, tk):
    nh = len(cols)
    m_s[...] = jnp.full(m_s.shape, -jnp.inf, F32)
    l_s[...] = jnp.zeros(l_s.shape, F32)
    acc_s[...] = jnp.zeros(acc_s.shape, F32)

    def chunk(k_ref, v_ref, rows):
        def scores(h):
            return _dot_nt(qm_s[h], k_ref[rows, cols[h]:cols[h] + LANES])

        s_next = scores(0)
        for h in range(nh):
            s = s_next
            if h + 1 < nh:
                s_next = scores(h + 1)
            m_old = m_s[h]
            m_new = jnp.maximum(m_old, jnp.max(s, axis=-1, keepdims=True))
            alpha = jnp.exp2(m_old - m_new)
            p = jnp.exp2(s - m_new)
            l_s[h] = alpha * l_s[h] + jnp.sum(p, axis=-1, keepdims=True)
            acc_s[h] = alpha * acc_s[h] + _dot(p.astype(BF16), v_ref[rows, cols[h]:cols[h] + LANES])
            m_s[h] = m_new

    chunk(kc_ref, vc_ref, slice(None))
    if n_lat:
        def body(j, carry):
            chunk(kl_ref, vl_ref, pl.ds(pl.multiple_of(j * tk, tk), tk))
            return carry

        lax.fori_loop(0, n_lat, body, 0)


def _gqa_kernel(*refs, n_lat, tk):
    if n_lat:
        q_ref, kc_ref, vc_ref, kl_ref, vl_ref, o_ref, qm_s, m_s, l_s, acc_s = refs
    else:
        q_ref, kc_ref, vc_ref, o_ref, qm_s, m_s, l_s, acc_s = refs
        kl_ref = vl_ref = None
    tq = q_ref.shape[0]
    nblk = GQA_WIDTH // LANES
    lo = lax.broadcasted_iota(jnp.int32, (tq, LANES), 1) < GQA_HEAD_DIM
    for blk in range(nblk):
        qb = q_ref[:, blk * LANES:(blk + 1) * LANES]
        zero = jnp.zeros_like(qb)
        qm_s[2 * blk] = jnp.where(lo, qb, zero)
        qm_s[2 * blk + 1] = jnp.where(lo, zero, qb)
    _flash_all((0,) * (2 * nblk), qm_s, m_s, l_s, acc_s, kc_ref, vc_ref, kl_ref, vl_ref, n_lat, tk)
    for blk in range(nblk):
        o = jnp.where(lo, acc_s[2 * blk] / l_s[2 * blk], acc_s[2 * blk + 1] / l_s[2 * blk + 1])
        o_ref[:, blk * LANES:(blk + 1) * LANES] = o.astype(BF16)


def _diff_kernel(*refs, n_lat, tk, lambda_init):
    if n_lat:
        q_ref, kc_ref, vc_ref, kl_ref, vl_ref, lam_ref, ng_ref, o_ref, qm_s, m_s, l_s, acc_s = refs
    else:
        q_ref, kc_ref, vc_ref, lam_ref, ng_ref, o_ref, qm_s, m_s, l_s, acc_s = refs
        kl_ref = vl_ref = None
    tq = q_ref.shape[0]
    nblk = DIFF_WIDTH // LANES
    lf = lam_ref[...]
    lam = (jnp.exp(jnp.sum(lf[0:1] * lf[1:2], axis=-1, keepdims=True))
           - jnp.exp(jnp.sum(lf[2:3] * lf[3:4], axis=-1, keepdims=True)) + lambda_init)
    lane = lax.broadcasted_iota(jnp.int32, (tq, LANES), 1)
    lo = lane < DIFF_V_DIM
    gm = _group_matrix(1.0 / DIFF_V_DIM)
    for blk in range(nblk):
        qb = q_ref[:, blk * LANES:(blk + 1) * LANES]
        zero = jnp.zeros_like(qb)
        for sc in range(4):
            qm_s[4 * blk + sc] = jnp.where((lane // DIFF_QK_DIM) == sc, qb, zero)
    cols = tuple(blk * LANES for blk in range(nblk) for _ in range(4))
    _flash_all(cols, qm_s, m_s, l_s, acc_s, kc_ref, vc_ref, kl_ref, vl_ref, n_lat, tk)
    for blk in range(nblk):
        halves = []
        for s in range(2):
            i0, i1 = 4 * blk + 2 * s, 4 * blk + 2 * s + 1
            halves.append(acc_s[i0] / l_s[i0] - lam * (acc_s[i1] / l_s[i1]))
        o = jnp.where(lo, halves[0], halves[1])
        ms = _group_sum(o * o, gm)
        o = o * lax.rsqrt(ms + NORM_EPS) * ng_ref[...] * (1.0 - lambda_init)
        o_ref[:, blk * LANES:(blk + 1) * LANES] = o.astype(BF16)


def _attention(kind, q, kc, vc, kl, vl, extra, lambda_init=None):
    b, t, w = q.shape
    tc, wk = kc.shape[1], kc.shape[2]
    tq = min(256, t)
    full = lambda n, width: pl.BlockSpec((None, n, width), lambda i, j: (i, 0, 0))
    in_specs = [pl.BlockSpec((None, tq, w), lambda i, j: (i, j, 0)), full(tc, wk), full(tc, wk)]
    args = [q, kc, vc]
    n_lat, tk = 0, 0
    if kl is not None:
        tl = kl.shape[1]
        tk = min(512, tl)
        n_lat = tl // tk
        in_specs += [full(tl, wk), full(tl, wk)]
        args += [kl, vl]
    for e in extra:
        in_specs.append(pl.BlockSpec(e.shape, lambda i, j: (0, 0)))
        args.append(e)
    if kind == "gqa":
        body = functools.partial(_gqa_kernel, n_lat=n_lat, tk=tk)
        nh = GQA_Q_HEADS
    else:
        body = functools.partial(_diff_kernel, n_lat=n_lat, tk=tk, lambda_init=lambda_init)
        nh = 2 * DIFF_HEADS
    return pl.pallas_call(
        body,
        grid=(b, t // tq),
        in_specs=in_specs,
        out_specs=pl.BlockSpec((None, tq, w), lambda i, j: (i, j, 0)),
        out_shape=jax.ShapeDtypeStruct((b, t, w), BF16),
        scratch_shapes=[pltpu.VMEM((nh, tq, LANES), BF16), pltpu.VMEM((nh, tq, 1), F32),
                        pltpu.VMEM((nh, tq, 1), F32), pltpu.VMEM((nh, tq, LANES), F32)],
        compiler_params=_cparams(("parallel", "parallel")),
        name=kind + "_attn",
    )(*args)


def _gdn_local_kernel(x_ref, xp_ref, xn_ref, cw_ref, na_ref, nb_ref, alog_ref, dtb_ref,
                      uf_ref, wf_ref, qdf_ref, kdf_ref, qkf_ref, glf_ref,
                      ub_ref, wb_ref, qdb_ref, kdb_ref, qkb_ref, glb_ref,
                      xbuf, q_s, k_s, v_s, g_s, b_s):
    tm = x_ref.shape[0]
    cs = GDN_CHUNK
    i = pl.program_id(1)
    nblk = pl.num_programs(1)
    halo = xp_ref.shape[0]
    xbuf[halo:halo + tm, :] = x_ref[...]
    xbuf[0:halo, :] = jnp.where(i > 0, xp_ref[...], 0.0)
    xbuf[halo + tm:2 * halo + tm, :] = jnp.where(i < nblk - 1, xn_ref[...], 0.0)
    gm = _group_matrix(1.0)
    for part, dst in enumerate((q_s, k_s, v_s)):
        cols = slice(part * GDN_WIDTH, (part + 1) * GDN_WIDTH)
        acc = None
        for j in range(GDN_CONV):
            start = halo - GDN_CONV // 2 + j
            term = xbuf[start:start + tm, cols] * cw_ref[j:j + 1, cols]
            acc = term if acc is None else acc + term
        y = jax.nn.silu(acc)
        if part < 2:
            scale = GDN_HEAD_DIM ** -0.5 if part == 0 else 1.0
            for blk in range(GDN_PAIRS):
                yb = y[:, blk * LANES:(blk + 1) * LANES]
                ss = _group_sum(yb * yb, gm)
                dst[:, blk * LANES:(blk + 1) * LANES] = yb * lax.rsqrt(ss + L2_EPS) * scale
        else:
            dst[...] = y
    xa = na_ref[...] + dtb_ref[...]
    softplus = jnp.maximum(xa, 0.0) + jnp.log(1.0 + jnp.exp(-jnp.abs(xa)))
    g_s[...] = -jnp.exp(alog_ref[...]) * softplus
    b_s[...] = jax.nn.sigmoid(nb_ref[...])

    ri = lax.broadcasted_iota(jnp.int32, (cs, LANES), 0)
    li = lax.broadcasted_iota(jnp.int32, (cs, LANES), 1)
    lo = li < 64
    tj = li % 64
    r64 = lax.broadcasted_iota(jnp.int32, (cs, cs), 0)
    c64 = lax.broadcasted_iota(jnp.int32, (cs, cs), 1)
    tri_lo = jnp.where(r64 >= c64, 1.0, 0.0).astype(BF16)
    tri_up = jnp.where(r64 <= c64, 1.0, 0.0).astype(BF16)
    r128 = lax.broadcasted_iota(jnp.int32, (LANES, LANES), 0)
    c128 = lax.broadcasted_iota(jnp.int32, (LANES, LANES), 1)
    eye = jnp.where(r128 == c128, 1.0, 0.0)
    same8 = (r128 // 8) == (c128 // 8)
    level_masks = [((r128 // (2 * m)) == (c128 // (2 * m))) & ((r128 // m) != (c128 // m)) for m in (8, 16, 32)]
    lo1 =lax.broadcasted_iota(jnp.int32, (1, LANES), 1) < 64

    def stack(x, zero):
        return jnp.concatenate([jnp.where(lo, x, zero), jnp.where(lo, zero, x)], axis=0)

    dirs = ((uf_ref, wf_ref, qdf_ref, kdf_ref, qkf_ref, glf_ref),
            (ub_ref, wb_ref, qdb_ref, kdb_ref, qkb_ref, glb_ref))

    def chunk(c, carry):
        r0 = pl.multiple_of(c * cs, cs)
        rows = pl.ds(r0, cs)
        g = g_s[rows, :]
        be = b_s[rows, :]
        gh = g.astype(BF16)
        r1 = g - gh.astype(F32)
        gmid = r1.astype(BF16)
        glo = (r1 - gmid.astype(F32)).astype(BF16)
        cum_f = _dot(tri_lo, gh) + _dot(tri_lo, gmid) + _dot(tri_lo, glo)
        cum_b = _dot(tri_up, gh) + _dot(tri_up, gmid) + _dot(tri_up, glo)
        gc = jnp.where(li < GDN_HEADS, cum_f, cum_b)
        gt = jnp.concatenate([gc, gc], axis=0).T
        chains = []
        for p in range(GDN_PAIRS):
            cols = slice(p * LANES, (p + 1) * LANES)
            q128 = q_s[rows, cols]
            k128 = k_s[rows, cols]
            v128 = v_s[rows, cols]
            kb = k128.astype(BF16)
            kstack = stack(kb, jnp.zeros_like(kb))
            kk = _dot_nt(kb, kstack)
            qk = _dot_nt(q128.astype(BF16), kstack)
            for rev in range(2):
                la = rev * GDN_HEADS + 2 * p
                gca, gcb = gc[:, la:la + 1], gc[:, la + 1:la + 2]
                bca, bcb = be[:, la:la + 1], be[:, la + 1:la + 2]
                gcol = jnp.where(lo, gca, gcb)
                grow = jnp.where(lo1, gt[la:la + 1, :], gt[la + 1:la + 2, :])
                bcol = jnp.where(lo, bca, bcb)
                if rev:
                    incl, strict = ri <= tj, ri < tj
                    last = 0
                else:
                    incl, strict = ri >= tj, ri > tj
                    last = cs - 1
                decay = jnp.exp(jnp.where(incl, gcol - grow, -jnp.inf))
                a128 = jnp.where(strict, kk * decay * bcol, 0.0)
                n = -stack(a128, 0.0)
                ea, eb = jnp.exp(gca), jnp.exp(gcb)
                rhs = jnp.concatenate([
                    jnp.concatenate([v128 * bca, v128 * bcb], axis=0),
                    jnp.concatenate([k128 * (bca * ea), k128 * (bcb * eb)], axis=0)], axis=1)
                glast = jnp.where(lo1, gc[last:last + 1, la:la + 1], gc[last:last + 1, la + 1:la + 2])
                u_ref, w_ref, qd_ref, kd_ref, qk_ref, gl_ref = dirs[rev]
                qd_ref[rows, cols] = (q128 * jnp.exp(gcol)).astype(BF16)
                kd_ref[rows, cols] = (k128 * jnp.exp(glast - gcol)).astype(BF16)
                qk_ref[rows, cols] = (qk * decay).astype(BF16)
                gl_ref[c, :, cols] = jnp.broadcast_to(jnp.exp(glast), (8, LANES))
                chains.append(dict(n=n, rhs=rhs, cols=cols, rev=rev))
        for ch in chains:
            d0 = jnp.where(same8, ch["n"], 0.0)
            ch["t"] = eye + d0
            ch["pw"] = _dotn(d0, d0, GDN_INVERSE_PASSES)
        for ch in chains:
            both = _dotn(jnp.concatenate([ch["t"], ch["pw"]], axis=0), ch["pw"], GDN_INVERSE_PASSES)
            ch["t"] = ch["t"] + both[0:LANES]
            ch["pw"] = both[LANES:2 * LANES]
        for ch in chains:
            ch["t"] = ch["t"] + _dotn(ch["t"], ch["pw"], GDN_INVERSE_PASSES)
        for off_mask in level_masks:
            for ch in chains:
                ch["x"] = _dotn(jnp.where(off_mask, ch["n"], 0.0), ch["t"], GDN_INVERSE_PASSES)
            for ch in chains:
                ch["t"] = ch["t"] + _dotn(ch["t"], ch["x"], GDN_INVERSE_PASSES)
        for ch in chains:
            sol = _dotn(ch["t"], ch["rhs"], GDN_SOLVE_PASSES)
            u_ref, w_ref = dirs[ch["rev"]][0:2]
            u_ref[rows, ch["cols"]] = jnp.where(lo, sol[0:cs, 0:LANES], sol[cs:2 * cs, 0:LANES])
            w_ref[rows, ch["cols"]] = jnp.where(lo, sol[0:cs, LANES:], sol[cs:2 * cs, LANES:]).astype(BF16)
        return carry

    lax.fori_loop(0, tm // cs, chunk, 0)


def _gdn_local(nqkv, na, nb, conv_w, alog, dtb):
    b, t, w3 = nqkv.shape
    tm = min(256, t)
    nblk = t // tm
    cpb = tm // GDN_CHUNK
    halo = 8
    hb = tm // halo
    tok = lambda width: pl.BlockSpec((None, tm, width), lambda i, j: (i, j, 0))
    const2 = lambda shape: pl.BlockSpec(shape, lambda i, j: (0, 0))
    in_specs = [
        tok(w3),
        pl.BlockSpec((None, halo, w3), lambda i, j: (i, jnp.maximum(j * hb - 1, 0), 0)),
        pl.BlockSpec((None, halo, w3), lambda i, j: (i, jnp.minimum((j + 1) * hb, t // halo - 1), 0)),
        const2((GDN_CONV, w3)), tok(LANES), tok(LANES), const2((1, LANES)), const2((1, LANES)),
    ]
    per_dir_specs = [tok(GDN_WIDTH)] * 5 + [pl.BlockSpec((None, cpb, 8, GDN_WIDTH), lambda i, j: (i, j, 0, 0))]
    per_dir_shapes = ([jax.ShapeDtypeStruct((b, t, GDN_WIDTH), F32)]
                      + [jax.ShapeDtypeStruct((b, t, GDN_WIDTH), BF16)] * 4
                      + [jax.ShapeDtypeStruct((b, t // GDN_CHUNK, 8, GDN_WIDTH), F32)])
    outs = pl.pallas_call(
        _gdn_local_kernel,
        grid=(b, nblk),
        in_specs=in_specs,
        out_specs=per_dir_specs * 2,
        out_shape=per_dir_shapes * 2,
        scratch_shapes=[
            pltpu.VMEM((tm + 2 * halo, w3), F32),
            pltpu.VMEM((tm, GDN_WIDTH), F32), pltpu.VMEM((tm, GDN_WIDTH), F32), pltpu.VMEM((tm, GDN_WIDTH), F32),
            pltpu.VMEM((tm, LANES), F32), pltpu.VMEM((tm, LANES), F32),
        ],
        compiler_params=_cparams(("parallel", "parallel")),
        name="gdn_local",
    )(nqkv, nqkv, nqkv, conv_w, na, nb, alog, dtb)
    return outs[:6], outs[6:]


def _gdn_scan_kernel(uf_ref, wf_ref, qdf_ref, kdf_ref, qkf_ref, glf_ref,
                     ub_ref, wb_ref, qdb_ref, kdb_ref, qkb_ref, glb_ref,
                     s0f_ref, s0b_ref,
                     of_ref, ob_ref, sff_ref, sfb_ref,
                     stf, stb):
    n = pl.program_id(1)
    cs = GDN_CHUNK
    cpb = uf_ref.shape[0] // cs

    @pl.when(n == 0)
    def _():
        stf[...] = s0f_ref[...]
        stb[...] = s0b_ref[...]

    lo = lax.broadcasted_iota(jnp.int32, (cs, LANES), 1) < 64
    r128 = lax.broadcasted_iota(jnp.int32, (LANES, LANES), 0) // 64
    c128 = lax.broadcasted_iota(jnp.int32, (LANES, LANES), 1) // 64
    bd = r128 == c128

    def step(c, u_ref, w_ref, qd_ref, kd_ref, qk_ref, gl_ref, o_ref, st):
        rows = slice(c * cs, (c + 1) * cs)
        for p in range(GDN_PAIRS):
            cols = slice(p * LANES, (p + 1) * LANES)
            s = st[p]
            sb = s.astype(BF16)
            vnew = u_ref[rows, cols] - _dot_nt(w_ref[rows, cols], sb)
            vb = vnew.astype(BF16)
            zero = jnp.zeros_like(vb)
            vstack = jnp.concatenate([jnp.where(lo, vb, zero), jnp.where(lo, zero, vb)], axis=0)
            o_ref[rows, cols] = _dot_nt(qd_ref[rows, cols], sb) + _dot(qk_ref[rows, cols], vstack)
            upd = _dot(vnew.T.astype(BF16), kd_ref[rows, cols])
            st[p] = s * gl_ref[c, 0:1, cols] + jnp.where(bd, upd, 0.0)

    for cc in range(cpb):
        step(cc, uf_ref, wf_ref, qdf_ref, kdf_ref, qkf_ref, glf_ref, of_ref, stf)
        step(cpb - 1 - cc, ub_ref, wb_ref, qdb_ref, kdb_ref, qkb_ref, glb_ref, ob_ref, stb)

    @pl.when(n == pl.num_programs(1) - 1)
    def _():
        sff_ref[...] = stf[...]
        sfb_ref[...] = stb[...]


def _gdn_scan(fwd, bwd, s0f, s0b):
    b, t, _ = fwd[0].shape
    tm = min(256, t)
    nblk = t // tm
    cpb = tm // GDN_CHUNK
    f_tok = pl.BlockSpec((None, tm, GDN_WIDTH), lambda i, j: (i, j, 0))
    b_tok = pl.BlockSpec((None, tm, GDN_WIDTH), lambda i, j: (i, nblk - 1 - j, 0))
    f_gl = pl.BlockSpec((None, cpb, 8, GDN_WIDTH), lambda i, j: (i, j, 0, 0))
    b_gl = pl.BlockSpec((None, cpb, 8, GDN_WIDTH), lambda i, j: (i, nblk - 1 - j, 0, 0))
    st_spec = pl.BlockSpec((None, GDN_PAIRS, LANES, LANES), lambda i, j: (i, 0, 0, 0))
    st_shape = jax.ShapeDtypeStruct((b, GDN_PAIRS, LANES, LANES), F32)
    o_shape = jax.ShapeDtypeStruct((b, t, GDN_WIDTH), F32)
    return pl.pallas_call(
        _gdn_scan_kernel,
        grid=(b, nblk),
        in_specs=[f_tok] * 5 + [f_gl] + [b_tok] * 5 + [b_gl] + [st_spec, st_spec],
        out_specs=[f_tok, b_tok, st_spec, st_spec],
        out_shape=[o_shape, o_shape, st_shape, st_shape],
        scratch_shapes=[pltpu.VMEM((GDN_PAIRS, LANES, LANES), F32)] * 2,
        compiler_params=_cparams(("parallel", "arbitrary")),
        name="gdn_scan",
    )(*fwd, *bwd, s0f, s0b)


def _out_proj_kernel(h_ref, d_ref, gq_ref, of_ref, ob_ref, z_ref, ng_ref, gate_ref, w_ref, o_ref):
    gm = _group_matrix(1.0 / GDN_HEAD_DIM)
    y = _dot(d_ref[...], w_ref[0:DIFF_WIDTH, :])
    y += _dot(gq_ref[...], w_ref[DIFF_WIDTH:DIFF_WIDTH + GQA_WIDTH, :])
    base = DIFF_WIDTH + GQA_WIDTH
    for blk in range(GDN_PAIRS):
        cols = slice(blk * LANES, (blk + 1) * LANES)
        o = of_ref[:, cols] + ob_ref[:, cols]
        ms = _group_sum(o * o, gm)
        r = o * lax.rsqrt(ms + NORM_EPS) * ng_ref[...] * jax.nn.silu(z_ref[:, cols])
        y += _dot(r.astype(BF16), w_ref[base + blk * LANES:base + (blk + 1) * LANES, :])
    o_ref[...] = h_ref[...] + gate_ref[...] * y


def _out_proj(h, d, gq, of, ob, z, ng, gate, w):
    b, t, dm = h.shape
    tm = min(512, t)
    nb = gate.shape[0]
    mod_map = (lambda i, j: (i, 0, 0)) if nb == b else (lambda i, j: (0, 0, 0))
    tok = lambda width: pl.BlockSpec((None, tm, width), lambda i, j: (i, j, 0))
    return pl.pallas_call(
        _out_proj_kernel,
        grid=(b, t // tm),
        in_specs=[tok(dm), tok(DIFF_WIDTH), tok(GQA_WIDTH), tok(GDN_WIDTH), tok(GDN_WIDTH), tok(GDN_WIDTH),
                  pl.BlockSpec((1, LANES), lambda i, j: (0, 0)),
                  pl.BlockSpec((None, 1, dm), mod_map),
                  pl.BlockSpec(w.shape, lambda i, j: (0, 0))],
        out_specs=tok(dm),
        out_shape=jax.ShapeDtypeStruct((b, t, dm), F32),
        compiler_params=_cparams(("parallel", "parallel")),
        name="out_proj",
    )(h, d, gq, of, ob, z, ng, gate, w)


def _ffn_kernel(h_ref, g_ref, sh_ref, sc_ref, gate_ref, wg_ref, wu_ref, wd_ref, fg_ref, o_ref, a_s, acc_s,
                *, final_norm):
    j = pl.program_id(2)

    @pl.when(j == 0)
    def _():
        x = h_ref[...]
        y = x * lax.rsqrt(jnp.mean(x * x, axis=-1, keepdims=True) + NORM_EPS) * g_ref[...]
        a_s[...] = (y * (1.0 + sc_ref[...]) + sh_ref[...]).astype(BF16)
        acc_s[...] = jnp.zeros_like(acc_s)

    a = a_s[...]
    act = jax.nn.silu(_dot(a, wg_ref[...])) * _dot(a, wu_ref[...])
    acc_s[...] += _dot(act.astype(BF16), wd_ref[...])

    @pl.when(j == pl.num_programs(2) - 1)
    def _():
        out = h_ref[...] + gate_ref[...] * acc_s[...]
        if final_norm:
            out = out * lax.rsqrt(jnp.mean(out * out, axis=-1, keepdims=True) + NORM_EPS) * fg_ref[...]
        o_ref[...] = out


def _ffn(h, g, sh, sc, gate, w_gu, w_down, fg, final_norm):
    b, t, dm = h.shape
    hidden = w_down.shape[0]
    tm = min(1024, t)
    th = 256
    nh = hidden // th
    nb = sh.shape[0]
    mod_map = (lambda i, k, j: (i, 0, 0)) if nb == b else (lambda i, k, j: (0, 0, 0))
    tok = pl.BlockSpec((None, tm, dm), lambda i, k, j: (i, k, 0))
    vec = pl.BlockSpec((1, dm), lambda i, k, j: (0, 0))
    mod = pl.BlockSpec((None, 1, dm), mod_map)
    return pl.pallas_call(
        functools.partial(_ffn_kernel, final_norm=final_norm),
        grid=(b, t // tm, nh),
        in_specs=[tok, vec, mod, mod, mod,
                  pl.BlockSpec((dm, th), lambda i, k, j: (0, j)),
                  pl.BlockSpec((dm, th), lambda i, k, j: (0, j + nh)),
                  pl.BlockSpec((th, dm), lambda i, k, j: (j, 0)),
                  vec],
        out_specs=tok,
        out_shape=jax.ShapeDtypeStruct((b, t, dm), F32),
        scratch_shapes=[pltpu.VMEM((tm, dm), BF16), pltpu.VMEM((tm, dm), F32)],
        compiler_params=_cparams(("parallel", "parallel", "arbitrary")),
        name="ffn",
    )(h, g, sh, sc, gate, w_gu, w_gu, w_down, fg)


def _rope_tables(t, rot_dim):
    nf = rot_dim // 4
    pos = jnp.arange(t)
    row = (pos // GRID_W).astype(F32)
    col = (pos % GRID_W).astype(F32)
    inv_freq = ROPE_THETA ** (-jnp.arange(nf, dtype=F32) / nf)
    ar, ac = row[:, None] * inv_freq, col[:, None] * inv_freq
    z = jnp.zeros_like(ar)
    cos = jnp.concatenate([jnp.cos(ar), jnp.cos(ar), jnp.cos(ac), jnp.cos(ac)], axis=1)
    sa = jnp.concatenate([-jnp.sin(ar), z, -jnp.sin(ac), z], axis=1)
    sb = jnp.concatenate([z, jnp.sin(ar), z, jnp.sin(ac)], axis=1)
    rep = LANES // rot_dim
    return tuple(jnp.tile(a, (1, rep)) for a in (cos, sa, sb))


def _pad_lanes(v):
    return jnp.pad(v.reshape(1, -1), ((0, 0), (0, LANES - v.size)))


def _relayout_w_in(w):
    o = 0
    pieces = []
    for width in (DIFF_WIDTH, DIFF_WIDTH, DIFF_WIDTH):
        pieces.append(w[:, o:o + width]); o += width
    gq = w[:, o:o + GQA_WIDTH].reshape(-1, GQA_Q_HEADS, GQA_HEAD_DIM); o += GQA_WIDTH
    pieces.append(jnp.take(gq, jnp.array(GQA_Q_ORDER), axis=1).reshape(-1, GQA_WIDTH))
    for width in (GQA_KV_WIDTH, GQA_KV_WIDTH, 3 * GDN_WIDTH, GDN_WIDTH):
        pieces.append(w[:, o:o + width]); o += width
    for width in (2 * GDN_HEADS, 2 * GDN_HEADS):
        pieces.append(jnp.pad(w[:, o:o + width], ((0, 0), (0, LANES - width)))); o += width
    return jnp.concatenate(pieces, axis=1).astype(BF16)


def _relayout_w_out(w):
    gq = w[DIFF_WIDTH:DIFF_WIDTH + GQA_WIDTH].reshape(GQA_Q_HEADS, GQA_HEAD_DIM, -1)
    gq = jnp.take(gq, jnp.array(GQA_Q_ORDER), axis=0).reshape(GQA_WIDTH, -1)
    return jnp.concatenate([w[:DIFF_WIDTH], gq, w[DIFF_WIDTH + GQA_WIDTH:]], axis=0).astype(BF16)


def kernel(x, c, ctx, c_ctx, norm1_g, ada_w, ada_b, w_in, diff_lambda, diff_norm_g, q_norm_g, k_norm_g,
           gdn_conv_w, gdn_a_log, gdn_dt_bias, gdn_norm_g, w_out, norm2_g, ffn_w_gu, ffn_w_down, final_norm_g):
    b, t, d = x.shape
    depth = w_in.shape[0]
    rope = _rope_tables(t, DIFF_QK_DIM) + _rope_tables(t, GQA_HEAD_DIM)

    cond = jnp.concatenate([c, c_ctx[None, :], jnp.zeros((16 - b - 1, d), F32)], axis=0)
    mod = _ada(cond, ada_w, ada_b).reshape(depth, 16, 6, d)

    tile2 = lambda v: jnp.tile(v.reshape(1, -1), (1, LANES // v.size))
    zeros_state = jnp.zeros((b, GDN_PAIRS, LANES, LANES), F32)

    h, hc = x, ctx
    for layer in range(depth):
        need_ctx = layer < depth - 1
        lambda_init = 0.8 - 0.6 * math.exp(-0.3 * layer)
        mod_l = [mod[layer, :b, k][:, None, :] for k in range(6)]
        mod_c = [mod[layer, b:b + 1, k][:, None, :] for k in range(6)]
        w_in_l = _relayout_w_in(w_in[layer])
        w_out_l = _relayout_w_out(w_out[layer])
        w_gu_l = ffn_w_gu[layer].astype(BF16)
        w_down_l = ffn_w_down[layer].astype(BF16)
        g1 = norm1_g[layer].reshape(1, d)
        g2 = norm2_g[layer].reshape(1, d)
        qg, kg = tile2(q_norm_g[layer]), tile2(k_norm_g[layer])
        dng, nng = tile2(diff_norm_g[layer]), tile2(gdn_norm_g[layer])
        alog, dtb = _pad_lanes(gdn_a_log[layer]), _pad_lanes(gdn_dt_bias[layer])
        lam = diff_lambda[layer]
        conv_w = gdn_conv_w[layer]

        pl_ = _in_proj(h, g1, mod_l[0], mod_l[1], w_in_l, qg, kg, rope)
        pc_ = _in_proj(hc, g1, mod_c[0], mod_c[1], w_in_l, qg, kg, None)
        dq_l, dk_l, dv_l, gq_l, gk_l, gv_l, nqkv_l, nz_l, na_l, nb_l = pl_
        dq_c, dk_c, dv_c, gq_c, gk_c, gv_c, nqkv_c, nz_c, na_c, nb_c = pc_

        d_l = _attention("diff", dq_l, dk_c, dv_c, dk_l, dv_l, (lam, dng), lambda_init)
        a_l = _attention("gqa", gq_l, gk_c, gv_c, gk_l, gv_l, ())

        fwd_c, bwd_c = _gdn_local(nqkv_c, na_c, nb_c, conv_w, alog, dtb)
        fwd_l, bwd_l = _gdn_local(nqkv_l, na_l, nb_l, conv_w, alog, dtb)
        ocf, ocb, scf, scb = _gdn_scan(fwd_c, bwd_c, zeros_state, zeros_state)
        olf, olb, _, _ = _gdn_scan(fwd_l, bwd_l, scf, scb)

        h = _out_proj(h, d_l, a_l, olf, olb, nz_l, nng, mod_l[2], w_out_l)
        last = layer == depth - 1
        h = _ffn(h, g2, mod_l[3], mod_l[4], mod_l[5], w_gu_l, w_down_l, final_norm_g.reshape(1, d), last)
        if need_ctx:
            d_c = _attention("diff", dq_c, dk_c, dv_c, None, None, (lam, dng), lambda_init)
            a_c = _attention("gqa", gq_c, gk_c, gv_c, None, None, ())
            hc = _out_proj(hc, d_c, a_c, ocf, ocb, nz_c, nng, mod_c[2], w_out_l)
            hc = _ffn(hc, g2, mod_c[3], mod_c[4], mod_c[5], w_gu_l, w_down_l, final_norm_g.reshape(1, d), False)
    return h
```

```python
import functools
import math

import jax
import jax.numpy as jnp
from jax import lax
from jax.experimental import pallas as pl
from jax.experimental.pallas import tpu as pltpu

F32 = jnp.float32
BF16 = jnp.bfloat16

LANES = 128
D_MODEL = 1024
GRID_W = 64
ROPE_THETA = 10000.0
NORM_EPS = 1e-6
L2_EPS = 1e-6
LOG2E = 1.4426950408889634

DIFF_HEADS = 4
DIFF_QK_DIM = 32
DIFF_V_DIM = 64
DIFF_WIDTH = DIFF_HEADS * DIFF_V_DIM
GQA_Q_HEADS = 6
GQA_KV_HEADS = 2
GQA_HEAD_DIM = 64
GQA_WIDTH = GQA_Q_HEADS * GQA_HEAD_DIM
GQA_KV_WIDTH = GQA_KV_HEADS * GQA_HEAD_DIM
GDN_HEADS = 6
GDN_HEAD_DIM = 64
GDN_WIDTH = GDN_HEADS * GDN_HEAD_DIM
GDN_CONV = 5
GDN_CHUNK = 64
GDN_PAIRS = GDN_HEADS // 2
FFN_HIDDEN = 2816

OFF_DQ = 0
OFF_DK = OFF_DQ + DIFF_WIDTH
OFF_DV = OFF_DK + DIFF_WIDTH
OFF_GQ = OFF_DV + DIFF_WIDTH
OFF_GK = OFF_GQ + GQA_WIDTH
OFF_GV = OFF_GK + GQA_KV_WIDTH
OFF_NQKV = OFF_GV + GQA_KV_WIDTH
OFF_NZ = OFF_NQKV + 3 * GDN_WIDTH
OFF_NA = OFF_NZ + GDN_WIDTH
OFF_NB = OFF_NA + LANES
IN_PAD = OFF_NB + LANES

GQA_Q_ORDER = (0, 3, 1, 4, 2, 5)

GDN_INVERSE_PASSES = 1
GDN_SOLVE_PASSES = 1
GDN_CHUNKS_PER_ITER = 4

ATTN_SKEW = 3
ATTN_CHUNKS_PER_ITER = 4

VMEM_LIMIT = 56 * 1024 * 1024


def _cparams(sem):
    return pltpu.CompilerParams(dimension_semantics=sem, vmem_limit_bytes=VMEM_LIMIT)


def _dot(a, b):
    return jnp.dot(a, b, preferred_element_type=F32)


def _dot_nt(a, b):
    return lax.dot_general(a, b, (((1,), (1,)), ((), ())), preferred_element_type=F32)


def _split_bf16(x):
    hi = x.astype(BF16)
    lo = (x - hi.astype(F32)).astype(BF16)
    return hi, lo


def _dot3(a, b):
    ah, al = _split_bf16(a)
    bh, bl = _split_bf16(b)
    return _dot(ah, bh) + _dot(ah, bl) + _dot(al, bh)


def _dotn(a, b, passes):
    if passes == 1:
        return _dot(a.astype(BF16), b.astype(BF16))
    assert passes == 3
    return _dot3(a, b)


def _group_matrix(scale):
    r = lax.broadcasted_iota(jnp.int32, (LANES, LANES), 0) // 64
    c = lax.broadcasted_iota(jnp.int32, (LANES, LANES), 1) // 64
    return jnp.where(r == c, scale, 0.0).astype(BF16)


def _group_sum(x, gm):
    hi, lo = _split_bf16(x)
    return _dot(hi, gm) + _dot(lo, gm)


def _rope(x, c, sa, sb, half):
    return x * c + pltpu.roll(x, LANES - half, 1) * sa + pltpu.roll(x, half, 1) * sb


def _ada_kernel(c_ref, w_ref, b_ref, o_ref):
    s = jax.nn.silu(c_ref[...]).astype(BF16)
    o_ref[...] = _dot(s, w_ref[...].astype(BF16)) + b_ref[...]


def _ada(cond, ada_w, ada_b):
    depth, d, n = ada_w.shape
    rows = cond.shape[0]
    tn = 1536
    return pl.pallas_call(
        _ada_kernel,
        grid=(depth, n // tn),
        in_specs=[
            pl.BlockSpec((rows, d), lambda l, j: (0, 0)),
            pl.BlockSpec((None, d, tn), lambda l, j: (l, 0, j)),
            pl.BlockSpec((None, 1, tn), lambda l, j: (l, 0, j)),
        ],
        out_specs=pl.BlockSpec((None, rows, tn), lambda l, j: (l, 0, j)),
        out_shape=jax.ShapeDtypeStruct((depth, rows, n), F32),
        compiler_params=_cparams(("parallel", "parallel")),
        name="ada_mod",
    )(cond, ada_w, ada_b.reshape(depth, 1, n))


def _in_proj_kernel(*refs, use_rope):
    if use_rope:
        (x_ref, g_ref, sh_ref, sc_ref, w_ref, qg_ref, kg_ref,
         cd_ref, sad_ref, sbd_ref, cg_ref, sag_ref, sbg_ref,
         dq_ref, dk_ref, dv_ref, gq_ref, gk_ref, gv_ref, nqkv_ref, nz_ref, na_ref, nb_ref) = refs
    else:
        (x_ref, g_ref, sh_ref, sc_ref, w_ref, qg_ref, kg_ref,
         dq_ref, dk_ref, dv_ref, gq_ref, gk_ref, gv_ref, nqkv_ref, nz_ref, na_ref, nb_ref) = refs
    x = x_ref[...]
    y = x * lax.rsqrt(jnp.mean(x * x, axis=-1, keepdims=True) + NORM_EPS) * g_ref[...]
    a = (y * (1.0 + sc_ref[...]) + sh_ref[...]).astype(BF16)

    def proj(lo, width):
        return _dot(a, w_ref[:, lo:lo + width])

    gm = _group_matrix(1.0 / GQA_HEAD_DIM)

    for off, out, scale in ((OFF_DQ, dq_ref, DIFF_QK_DIM ** -0.5 * LOG2E), (OFF_DK, dk_ref, 1.0)):
        for blk in range(DIFF_WIDTH // LANES):
            p = proj(off + blk * LANES, LANES)
            if use_rope:
                p = _rope(p, cd_ref[...], sad_ref[...], sbd_ref[...], DIFF_QK_DIM // 4)
            out[:, blk * LANES:(blk + 1) * LANES] = (p * scale).astype(BF16)
    for blk in range(DIFF_WIDTH // LANES):
        dv_ref[blk * LANES:(blk + 1) * LANES, :] = proj(OFF_DV + blk * LANES, LANES).T.astype(BF16)

    def qk_prep(p, gain, scale):
        ms = _group_sum(p * p, gm)
        p = p * lax.rsqrt(ms + NORM_EPS) * gain
        if use_rope:
            p = _rope(p, cg_ref[...], sag_ref[...], sbg_ref[...], GQA_HEAD_DIM // 4)
        return (p * scale).astype(BF16)

    for blk in range(GQA_WIDTH // LANES):
        p = proj(OFF_GQ + blk * LANES, LANES)
        gq_ref[:, blk * LANES:(blk + 1) * LANES] = qk_prep(p, qg_ref[...], GQA_HEAD_DIM ** -0.5 * LOG2E)
    gk_ref[...] = qk_prep(proj(OFF_GK, LANES), kg_ref[...], 1.0)
    gv_ref[...] = proj(OFF_GV, LANES).T.astype(BF16)

    for blk in range(3 * GDN_WIDTH // 384):
        nqkv_ref[:, blk * 384:(blk + 1) * 384] = proj(OFF_NQKV + blk * 384, 384)
    nz_ref[...] = proj(OFF_NZ, GDN_WIDTH)
    na_ref[...] = proj(OFF_NA, LANES)
    nb_ref[...] = proj(OFF_NB, LANES)


def _in_proj(x, g, sh, sc, w, qg, kg, rope):
    b, t, d = x.shape
    tm = min(512, t)
    nb = sh.shape[0]
    mod_map = (lambda i, j: (i, 0, 0)) if nb == b else (lambda i, j: (0, 0, 0))
    tok = lambda width: pl.BlockSpec((None, tm, width), lambda i, j: (i, j, 0))
    const2 = lambda shape: pl.BlockSpec(shape, lambda i, j: (0, 0))
    in_specs = [
        tok(d), const2((1, d)),
        pl.BlockSpec((None, 1, d), mod_map), pl.BlockSpec((None, 1, d), mod_map),
        const2((d, IN_PAD)), const2((1, LANES)), const2((1, LANES)),
    ]
    args = [x, g, sh, sc, w, qg, kg]
    if rope is not None:
        in_specs += [pl.BlockSpec((tm, LANES), lambda i, j: (j, 0))] * 6
        args += list(rope)
    widths = (DIFF_WIDTH, DIFF_WIDTH, DIFF_WIDTH, GQA_WIDTH, GQA_KV_WIDTH, GQA_KV_WIDTH,
              3 * GDN_WIDTH, GDN_WIDTH, LANES, LANES)
    dtypes = (BF16,) * 6 + (F32,) * 4
    transposed = (2, 5)
    out_specs = [pl.BlockSpec((None, None, wd, tm), lambda i, j: (i, j, 0, 0)) if k in transposed else tok(wd)
                 for k, wd in enumerate(widths)]
    out_shape = [jax.ShapeDtypeStruct((b, t // tm, wd, tm) if k in transposed else (b, t, wd), dt)
                 for k, (wd, dt) in enumerate(zip(widths, dtypes))]
    return pl.pallas_call(
        functools.partial(_in_proj_kernel, use_rope=rope is not None),
        grid=(b, t // tm),
        in_specs=in_specs,
        out_specs=out_specs,
        out_shape=out_shape,
        compiler_params=_cparams(("parallel", "parallel")),
        name="in_proj",
    )(*args)


def _flash_all(cols, qm_s, m_s, l_s, acc_s, kc_ref, vc_ref, kl_ref, vl_ref, n_lat):
    nh = len(cols)
    m_s[...] = jnp.full(m_s.shape, -jnp.inf, F32)
    l_s[...] = jnp.zeros(l_s.shape, F32)
    acc_s[...] = jnp.zeros(acc_s.shape, F32)

    def chunks(k_ref, v_ref, chunk_ids, tk):
        items = [(j, h) for j in chunk_ids for h in range(nh)]

        def scores(item):
            j, h = item
            rows = slice(None) if tk is None else pl.ds(pl.multiple_of(j * tk, tk), tk)
            return _dot_nt(k_ref[rows, cols[h]:cols[h] + LANES], qm_s[h])

        pending = [scores(it) for it in items[:ATTN_SKEW]]
        for n, (j, h) in enumerate(items):
            s = pending.pop(0)
            if n + ATTN_SKEW < len(items):
                pending.append(scores(items[n + ATTN_SKEW]))
            m_old = m_s[h]
            m_new = jnp.maximum(m_old, jnp.max(s, axis=0, keepdims=True))
            alpha = jnp.exp2(m_old - m_new)
            p = jnp.exp2(s - m_new)
            l_s[h] = alpha * l_s[h] + jnp.sum(p, axis=0, keepdims=True)
            acc_s[h] = alpha * acc_s[h] + _dot(v_ref[j, cols[h]:cols[h] + LANES, :], p.astype(BF16))
            m_s[h] = m_new

    chunks(kc_ref, vc_ref, (0,), None)
    if n_lat:
        tk = vl_ref.shape[-1]
        per_iter = math.gcd(ATTN_CHUNKS_PER_ITER, n_lat)

        def body(i, carry):
            chunks(kl_ref, vl_ref, tuple(i * per_iter + jj for jj in range(per_iter)), tk)
            return carry

        lax.fori_loop(0, n_lat // per_iter, body, 0)


def _head_out(acc_s, l_s, h):
    return (acc_s[h] / l_s[h]).T


def _gqa_kernel(*refs, n_lat):
    if n_lat:
        q_ref, kc_ref, vc_ref, kl_ref, vl_ref, o_ref, qm_s, m_s, l_s, acc_s = refs
    else:
        q_ref, kc_ref, vc_ref, o_ref, qm_s, m_s, l_s, acc_s = refs
        kl_ref = vl_ref = None
    tq = q_ref.shape[0]
    nblk = GQA_WIDTH // LANES
    lo = lax.broadcasted_iota(jnp.int32, (tq, LANES), 1) < GQA_HEAD_DIM
    for blk in range(nblk):
        qb = q_ref[:, blk * LANES:(blk + 1) * LANES]
        zero = jnp.zeros_like(qb)
        qm_s[2 * blk] = jnp.where(lo, qb, zero)
        qm_s[2 * blk + 1] = jnp.where(lo, zero, qb)
    _flash_all((0,) * (2 * nblk), qm_s, m_s, l_s, acc_s, kc_ref, vc_ref, kl_ref, vl_ref, n_lat)
    for blk in range(nblk):
        o = jnp.where(lo, _head_out(acc_s, l_s, 2 * blk), _head_out(acc_s, l_s, 2 * blk + 1))
        o_ref[:, blk * LANES:(blk + 1) * LANES] = o.astype(BF16)


def _diff_kernel(*refs, n_lat, lambda_init):
    if n_lat:
        q_ref, kc_ref, vc_ref, kl_ref, vl_ref, lam_ref, ng_ref, o_ref, qm_s, m_s, l_s, acc_s = refs
    else:
        q_ref, kc_ref, vc_ref, lam_ref, ng_ref, o_ref, qm_s, m_s, l_s, acc_s = refs
        kl_ref = vl_ref = None
    tq = q_ref.shape[0]
    nblk = DIFF_WIDTH // LANES
    lf = lam_ref[...]
    lam = (jnp.exp(jnp.sum(lf[0:1] * lf[1:2], axis=-1, keepdims=True))
           - jnp.exp(jnp.sum(lf[2:3] * lf[3:4], axis=-1, keepdims=True)) + lambda_init)
    lane = lax.broadcasted_iota(jnp.int32, (tq, LANES), 1)
    lo = lane < DIFF_V_DIM
    gm = _group_matrix(1.0 / DIFF_V_DIM)
    for blk in range(nblk):
        qb = q_ref[:, blk * LANES:(blk + 1) * LANES]
        zero = jnp.zeros_like(qb)
        for sc in range(4):
            qm_s[4 * blk + sc] = jnp.where((lane // DIFF_QK_DIM) == sc, qb, zero)
    cols = tuple(blk * LANES for blk in range(nblk) for _ in range(4))
    _flash_all(cols, qm_s, m_s, l_s, acc_s, kc_ref, vc_ref, kl_ref, vl_ref, n_lat)
    for blk in range(nblk):
        halves = []
        for s in range(2):
            i0, i1 = 4 * blk + 2 * s, 4 * blk + 2 * s + 1
            halves.append((acc_s[i0] / l_s[i0] - lam * (acc_s[i1] / l_s[i1])).T)
        o = jnp.where(lo, halves[0], halves[1])
        ms = _group_sum(o * o, gm)
        o = o * lax.rsqrt(ms + NORM_EPS) * ng_ref[...] * (1.0 - lambda_init)
        o_ref[:, blk * LANES:(blk + 1) * LANES] = o.astype(BF16)


def _attention(kind, q, kc, vc, kl, vl, extra, lambda_init=None):
    b, t, w = q.shape
    tq = min(256, t)
    full = lambda a: pl.BlockSpec((None,) + a.shape[1:], lambda i, j: (i,) + (0,) * (a.ndim - 1))
    in_specs = [pl.BlockSpec((None, tq, w), lambda i, j: (i, j, 0)), full(kc), full(vc)]
    args = [q, kc, vc]
    n_lat = 0
    if kl is not None:
        n_lat = vl.shape[1]
        in_specs += [full(kl), full(vl)]
        args += [kl, vl]
    for e in extra:
        in_specs.append(pl.BlockSpec(e.shape, lambda i, j: (0, 0)))
        args.append(e)
    if kind == "gqa":
        body = functools.partial(_gqa_kernel, n_lat=n_lat)
        nh = GQA_Q_HEADS
    else:
        body = functools.partial(_diff_kernel, n_lat=n_lat, lambda_init=lambda_init)
        nh = 2 * DIFF_HEADS
    return pl.pallas_call(
        body,
        grid=(b, t // tq),
        in_specs=in_specs,
        out_specs=pl.BlockSpec((None, tq, w), lambda i, j: (i, j, 0)),
        out_shape=jax.ShapeDtypeStruct((b, t, w), BF16),
        scratch_shapes=[pltpu.VMEM((nh, tq, LANES), BF16), pltpu.VMEM((nh, 1, tq), F32),
                        pltpu.VMEM((nh, 1, tq), F32), pltpu.VMEM((nh, LANES, tq), F32)],
        compiler_params=_cparams(("parallel", "parallel")),
        name=kind + "_attn",
    )(*args)


def _gdn_local_kernel(x_ref, xp_ref, xn_ref, cw_ref, na_ref, nb_ref, alog_ref, dtb_ref,
                      uf_ref, wf_ref, qdf_ref, kdf_ref, qkf_ref, glf_ref,
                      ub_ref, wb_ref, qdb_ref, kdb_ref, qkb_ref, glb_ref,
                      xbuf, q_s, k_s, v_s, g_s, b_s):
    tm = x_ref.shape[0]
    cs = GDN_CHUNK
    i = pl.program_id(1)
    nblk = pl.num_programs(1)
    halo = xp_ref.shape[0]
    xbuf[halo:halo + tm, :] = x_ref[...]
    xbuf[0:halo, :] = jnp.where(i > 0, xp_ref[...], 0.0)
    xbuf[halo + tm:2 * halo + tm, :] = jnp.where(i < nblk - 1, xn_ref[...], 0.0)
    gm = _group_matrix(1.0)
    for part, dst in enumerate((q_s, k_s, v_s)):
        cols = slice(part * GDN_WIDTH, (part + 1) * GDN_WIDTH)
        acc = None
        xall = xbuf[:, cols]
        for j in range(GDN_CONV):
            d = j - GDN_CONV // 2
            xs = xall if d == 0 else pltpu.roll(xall, (-d) % (tm + 2 * halo), 0)
            term = xs[halo:halo + tm] * cw_ref[j:j + 1, cols]
            acc = term if acc is None else acc + term
        y = jax.nn.silu(acc)
        if part < 2:
            scale = GDN_HEAD_DIM ** -0.5 if part == 0 else 1.0
            for blk in range(GDN_PAIRS):
                yb = y[:, blk * LANES:(blk + 1) * LANES]
                ss = _group_sum(yb * yb, gm)
                dst[:, blk * LANES:(blk + 1) * LANES] = yb * lax.rsqrt(ss + L2_EPS) * scale
        else:
            dst[...] = y
    xa = na_ref[...] + dtb_ref[...]
    softplus = jnp.maximum(xa, 0.0) + jnp.log(1.0 + jnp.exp(-jnp.abs(xa)))
    g_s[...] = -jnp.exp(alog_ref[...]) * softplus
    b_s[...] = jax.nn.sigmoid(nb_ref[...])

    ri = lax.broadcasted_iota(jnp.int32, (cs, LANES), 0)
    li = lax.broadcasted_iota(jnp.int32, (cs, LANES), 1)
    lo = li < 64
    tj = li % 64
    r64 = lax.broadcasted_iota(jnp.int32, (cs, cs), 0)
    c64 = lax.broadcasted_iota(jnp.int32, (cs, cs), 1)
    tri_lo = jnp.where(r64 >= c64, 1.0, 0.0).astype(BF16)
    tri_up = jnp.where(r64 <= c64, 1.0, 0.0).astype(BF16)
    r128 = lax.broadcasted_iota(jnp.int32, (LANES, LANES), 0)
    c128 = lax.broadcasted_iota(jnp.int32, (LANES, LANES), 1)
    eye = jnp.where(r128 == c128, 1.0, 0.0)
    same8 = (r128 // 8) == (c128 // 8)
    level_masks = [((r128 // (2 * m)) == (c128 // (2 * m))) & ((r128 // m) != (c128 // m)) for m in (8, 16, 32)]
    lo1 =lax.broadcasted_iota(jnp.int32, (1, LANES), 1) < 64

    def stack(x, zero):
        return jnp.concatenate([jnp.where(lo, x, zero), jnp.where(lo, zero, x)], axis=0)

    dirs = ((uf_ref, wf_ref, qdf_ref, kdf_ref, qkf_ref, glf_ref),
            (ub_ref, wb_ref, qdb_ref, kdb_ref, qkb_ref, glb_ref))

    def setup(c, chains):
        r0 = pl.multiple_of(c * cs, cs)
        rows = pl.ds(r0, cs)
        g = g_s[rows, :]
        be = b_s[rows, :]
        gh = g.astype(BF16)
        r1 = g - gh.astype(F32)
        gmid = r1.astype(BF16)
        glo = (r1 - gmid.astype(F32)).astype(BF16)
        cum_f = _dot(tri_lo, gh) + _dot(tri_lo, gmid) + _dot(tri_lo, glo)
        cum_b = _dot(tri_up, gh) + _dot(tri_up, gmid) + _dot(tri_up, glo)
        gc = jnp.where(li < GDN_HEADS, cum_f, cum_b)
        gt = jnp.concatenate([gc, gc], axis=0).T
        for p in range(GDN_PAIRS):
            cols = slice(p * LANES, (p + 1) * LANES)
            q128 = q_s[rows, cols]
            k128 = k_s[rows, cols]
            v128 = v_s[rows, cols]
            kb = k128.astype(BF16)
            kstack = stack(kb, jnp.zeros_like(kb))
            kk = _dot_nt(kb, kstack)
            qk = _dot_nt(q128.astype(BF16), kstack)
            for rev in range(2):
                la = rev * GDN_HEADS + 2 * p
                gca, gcb = gc[:, la:la + 1], gc[:, la + 1:la + 2]
                bca, bcb = be[:, la:la + 1], be[:, la + 1:la + 2]
                gcol = jnp.where(lo, gca, gcb)
                grow = jnp.where(lo1, gt[la:la + 1, :], gt[la + 1:la + 2, :])
                bcol = jnp.where(lo, bca, bcb)
                if rev:
                    incl, strict = ri <= tj, ri < tj
                    last = 0
                else:
                    incl, strict = ri >= tj, ri > tj
                    last = cs - 1
                decay = jnp.exp(jnp.where(incl, gcol - grow, -jnp.inf))
                a128 = jnp.where(strict, kk * decay * bcol, 0.0)
                n = -stack(a128, 0.0)
                ea, eb = jnp.exp(gca), jnp.exp(gcb)
                rhs = jnp.concatenate([
                    jnp.concatenate([v128 * bca, v128 * bcb], axis=0),
                    jnp.concatenate([k128 * (bca * ea), k128 * (bcb * eb)], axis=0)], axis=1)
                glast = jnp.where(lo1, gc[last:last + 1, la:la + 1], gc[last:last + 1, la + 1:la + 2])
                u_ref, w_ref, qd_ref, kd_ref, qk_ref, gl_ref = dirs[rev]
                qd_ref[rows, cols] = (q128 * jnp.exp(gcol)).astype(BF16)
                kd_ref[rows, cols] = (k128 * jnp.exp(glast - gcol)).astype(BF16)
                qk_ref[rows, cols] = (qk * decay).astype(BF16)
                gl_ref[c, :, cols] = jnp.broadcast_to(jnp.exp(glast), (8, LANES))
                chains.append(dict(n=n, rhs=rhs, rows=rows, cols=cols, rev=rev))

    def group(i, carry):
        chains = []
        for gi in range(per_iter):
            setup(i * per_iter + gi, chains)
        for ch in chains:
            d0 = jnp.where(same8, ch["n"], 0.0)
            ch["t"] = eye + d0
            ch["pw"] = _dotn(d0, d0, GDN_INVERSE_PASSES)
        for ch in chains:
            both = _dotn(jnp.concatenate([ch["t"], ch["pw"]], axis=0), ch["pw"], GDN_INVERSE_PASSES)
            ch["t"] = ch["t"] + both[0:LANES]
            ch["pw"] = both[LANES:2 * LANES]
        for ch in chains:
            ch["t"] = ch["t"] + _dotn(ch["t"], ch["pw"], GDN_INVERSE_PASSES)
        for off_mask in level_masks:
            for ch in chains:
                ch["x"] = _dotn(jnp.where(off_mask, ch["n"], 0.0), ch["t"], GDN_INVERSE_PASSES)
            for ch in chains:
                ch["t"] = ch["t"] + _dotn(ch["t"], ch["x"], GDN_INVERSE_PASSES)
        for ch in chains:
            sol = _dotn(ch["t"], ch["rhs"], GDN_SOLVE_PASSES)
            u_ref, w_ref = dirs[ch["rev"]][0:2]
            u_ref[ch["rows"], ch["cols"]] = jnp.where(lo, sol[0:cs, 0:LANES], sol[cs:2 * cs, 0:LANES])
            w_ref[ch["rows"], ch["cols"]] = jnp.where(lo, sol[0:cs, LANES:], sol[cs:2 * cs, LANES:]).astype(BF16)
        return carry

    per_iter = math.gcd(GDN_CHUNKS_PER_ITER, tm // cs)
    lax.fori_loop(0, tm // (cs * per_iter), group, 0)


def _gdn_local(nqkv, na, nb, conv_w, alog, dtb):
    b, t, w3 = nqkv.shape
    tm = min(256, t)
    nblk = t // tm
    cpb = tm // GDN_CHUNK
    halo = 8
    hb = tm // halo
    tok = lambda width: pl.BlockSpec((None, tm, width), lambda i, j: (i, j, 0))
    const2 = lambda shape: pl.BlockSpec(shape, lambda i, j: (0, 0))
    in_specs = [
        tok(w3),
        pl.BlockSpec((None, halo, w3), lambda i, j: (i, jnp.maximum(j * hb - 1, 0), 0)),
        pl.BlockSpec((None, halo, w3), lambda i, j: (i, jnp.minimum((j + 1) * hb, t // halo - 1), 0)),
        const2((GDN_CONV, w3)), tok(LANES), tok(LANES), const2((1, LANES)), const2((1, LANES)),
    ]
    per_dir_specs = [tok(GDN_WIDTH)] * 5 + [pl.BlockSpec((None, cpb, 8, GDN_WIDTH), lambda i, j: (i, j, 0, 0))]
    per_dir_shapes = ([jax.ShapeDtypeStruct((b, t, GDN_WIDTH), F32)]
                      + [jax.ShapeDtypeStruct((b, t, GDN_WIDTH), BF16)] * 4
                      + [jax.ShapeDtypeStruct((b, t // GDN_CHUNK, 8, GDN_WIDTH), F32)])
    outs = pl.pallas_call(
        _gdn_local_kernel,
        grid=(b, nblk),
        in_specs=in_specs,
        out_specs=per_dir_specs * 2,
        out_shape=per_dir_shapes * 2,
        scratch_shapes=[
            pltpu.VMEM((tm + 2 * halo, w3), F32),
            pltpu.VMEM((tm, GDN_WIDTH), F32), pltpu.VMEM((tm, GDN_WIDTH), F32), pltpu.VMEM((tm, GDN_WIDTH), F32),
            pltpu.VMEM((tm, LANES), F32), pltpu.VMEM((tm, LANES), F32),
        ],
        compiler_params=_cparams(("parallel", "parallel")),
        name="gdn_local",
    )(nqkv, nqkv, nqkv, conv_w, na, nb, alog, dtb)
    return outs[:6], outs[6:]


def _gdn_scan_kernel(uf_ref, wf_ref, qdf_ref, kdf_ref, qkf_ref, glf_ref,
                     ub_ref, wb_ref, qdb_ref, kdb_ref, qkb_ref, glb_ref,
                     s0f_ref, s0b_ref,
                     of_ref, ob_ref, sff_ref, sfb_ref,
                     stf, stb):
    n = pl.program_id(1)
    cs = GDN_CHUNK
    cpb = uf_ref.shape[0] // cs

    @pl.when(n == 0)
    def _():
        stf[...] = s0f_ref[...]
        stb[...] = s0b_ref[...]

    lo = lax.broadcasted_iota(jnp.int32, (cs, LANES), 1) < 64
    r128 = lax.broadcasted_iota(jnp.int32, (LANES, LANES), 0) // 64
    c128 = lax.broadcasted_iota(jnp.int32, (LANES, LANES), 1) // 64
    bd = r128 == c128

    fwd = (uf_ref, wf_ref, qdf_ref, kdf_ref, qkf_ref, glf_ref, of_ref, stf)
    bwd = (ub_ref, wb_ref, qdb_ref, kdb_ref, qkb_ref, glb_ref, ob_ref, stb)

    for cc in range(cpb):
        chains = []
        for c, (u_ref, w_ref, qd_ref, kd_ref, qk_ref, gl_ref, o_ref, st) in ((cc, fwd), (cpb - 1 - cc, bwd)):
            rows = slice(c * cs, (c + 1) * cs)
            for p in range(GDN_PAIRS):
                cols = slice(p * LANES, (p + 1) * LANES)
                s = st[p]
                sb = s.astype(BF16)
                ws = _dot_nt(w_ref[rows, cols], sb)
                qs = _dot_nt(qd_ref[rows, cols], sb)
                chains.append((c, rows, cols, p, s, ws, qs, u_ref, kd_ref, qk_ref, gl_ref, o_ref, st))
        for c, rows, cols, p, s, ws, qs, u_ref, kd_ref, qk_ref, gl_ref, o_ref, st in chains:
            vnew = u_ref[rows, cols] - ws
            vb = vnew.astype(BF16)
            zero = jnp.zeros_like(vb)
            vstack = jnp.concatenate([jnp.where(lo, vb, zero), jnp.where(lo, zero, vb)], axis=0)
            o_ref[rows, cols] = qs + _dot(qk_ref[rows, cols], vstack)
            upd = _dot(vnew.T.astype(BF16), kd_ref[rows, cols])
            st[p] = s * gl_ref[c, 0:1, cols] + jnp.where(bd, upd, 0.0)

    @pl.when(n == pl.num_programs(1) - 1)
    def _():
        sff_ref[...] = stf[...]
        sfb_ref[...] = stb[...]


def _gdn_scan(fwd, bwd, s0f, s0b):
    b, t, _ = fwd[0].shape
    tm = min(256, t)
    nblk = t // tm
    cpb = tm // GDN_CHUNK
    f_tok = pl.BlockSpec((None, tm, GDN_WIDTH), lambda i, j: (i, j, 0))
    b_tok = pl.BlockSpec((None, tm, GDN_WIDTH), lambda i, j: (i, nblk - 1 - j, 0))
    f_gl = pl.BlockSpec((None, cpb, 8, GDN_WIDTH), lambda i, j: (i, j, 0, 0))
    b_gl = pl.BlockSpec((None, cpb, 8, GDN_WIDTH), lambda i, j: (i, nblk - 1 - j, 0, 0))
    st_spec = pl.BlockSpec((None, GDN_PAIRS, LANES, LANES), lambda i, j: (i, 0, 0, 0))
    st_shape = jax.ShapeDtypeStruct((b, GDN_PAIRS, LANES, LANES), F32)
    o_shape = jax.ShapeDtypeStruct((b, t, GDN_WIDTH), F32)
    return pl.pallas_call(
        _gdn_scan_kernel,
        grid=(b, nblk),
        in_specs=[f_tok] * 5 + [f_gl] + [b_tok] * 5 + [b_gl] + [st_spec, st_spec],
        out_specs=[f_tok, b_tok, st_spec, st_spec],
        out_shape=[o_shape, o_shape, st_shape, st_shape],
        scratch_shapes=[pltpu.VMEM((GDN_PAIRS, LANES, LANES), F32)] * 2,
        compiler_params=_cparams(("parallel", "arbitrary")),
        name="gdn_scan",
    )(*fwd, *bwd, s0f, s0b)


def _out_proj_kernel(h_ref, d_ref, gq_ref, of_ref, ob_ref, z_ref, ng_ref, gate_ref, w_ref, o_ref):
    gm = _group_matrix(1.0 / GDN_HEAD_DIM)
    y = _dot(d_ref[...], w_ref[0:DIFF_WIDTH, :])
    y += _dot(gq_ref[...], w_ref[DIFF_WIDTH:DIFF_WIDTH + GQA_WIDTH, :])
    base = DIFF_WIDTH + GQA_WIDTH
    for blk in range(GDN_PAIRS):
        cols = slice(blk * LANES, (blk + 1) * LANES)
        o = of_ref[:, cols] + ob_ref[:, cols]
        ms = _group_sum(o * o, gm)
        r = o * lax.rsqrt(ms + NORM_EPS) * ng_ref[...] * jax.nn.silu(z_ref[:, cols])
        y += _dot(r.astype(BF16), w_ref[base + blk * LANES:base + (blk + 1) * LANES, :])
    o_ref[...] = h_ref[...] + gate_ref[...] * y


def _out_proj(h, d, gq, of, ob, z, ng, gate, w):
    b, t, dm = h.shape
    tm = min(512, t)
    nb = gate.shape[0]
    mod_map = (lambda i, j: (i, 0, 0)) if nb == b else (lambda i, j: (0, 0, 0))
    tok = lambda width: pl.BlockSpec((None, tm, width), lambda i, j: (i, j, 0))
    return pl.pallas_call(
        _out_proj_kernel,
        grid=(b, t // tm),
        in_specs=[tok(dm), tok(DIFF_WIDTH), tok(GQA_WIDTH), tok(GDN_WIDTH), tok(GDN_WIDTH), tok(GDN_WIDTH),
                  pl.BlockSpec((1, LANES), lambda i, j: (0, 0)),
                  pl.BlockSpec((None, 1, dm), mod_map),
                  pl.BlockSpec(w.shape, lambda i, j: (0, 0))],
        out_specs=tok(dm),
        out_shape=jax.ShapeDtypeStruct((b, t, dm), F32),
        compiler_params=_cparams(("parallel", "parallel")),
        name="out_proj",
    )(h, d, gq, of, ob, z, ng, gate, w)


def _ffn_kernel(h_ref, g_ref, sh_ref, sc_ref, gate_ref, wg_ref, wu_ref, wd_ref, fg_ref, o_ref, a_s, acc_s,
                *, final_norm):
    j = pl.program_id(2)

    @pl.when(j == 0)
    def _():
        x = h_ref[...]
        y = x * lax.rsqrt(jnp.mean(x * x, axis=-1, keepdims=True) + NORM_EPS) * g_ref[...]
        a_s[...] = (y * (1.0 + sc_ref[...]) + sh_ref[...]).astype(BF16)
        acc_s[...] = jnp.zeros_like(acc_s)

    a = a_s[...]
    act = jax.nn.silu(_dot(a, wg_ref[...])) * _dot(a, wu_ref[...])
    acc_s[...] += _dot(act.astype(BF16), wd_ref[...])

    @pl.when(j == pl.num_programs(2) - 1)
    def _():
        out = h_ref[...] + gate_ref[...] * acc_s[...]
        if final_norm:
            out = out * lax.rsqrt(jnp.mean(out * out, axis=-1, keepdims=True) + NORM_EPS) * fg_ref[...]
        o_ref[...] = out


def _ffn(h, g, sh, sc, gate, w_gu, w_down, fg, final_norm):
    b, t, dm = h.shape
    hidden = w_down.shape[0]
    tm = min(1024, t)
    th = 256
    nh = hidden // th
    nb = sh.shape[0]
    mod_map = (lambda i, k, j: (i, 0, 0)) if nb == b else (lambda i, k, j: (0, 0, 0))
    tok = pl.BlockSpec((None, tm, dm), lambda i, k, j: (i, k, 0))
    vec = pl.BlockSpec((1, dm), lambda i, k, j: (0, 0))
    mod = pl.BlockSpec((None, 1, dm), mod_map)
    return pl.pallas_call(
        functools.partial(_ffn_kernel, final_norm=final_norm),
        grid=(b, t // tm, nh),
        in_specs=[tok, vec, mod, mod, mod,
                  pl.BlockSpec((dm, th), lambda i, k, j: (0, j)),
                  pl.BlockSpec((dm, th), lambda i, k, j: (0, j + nh)),
                  pl.BlockSpec((th, dm), lambda i, k, j: (j, 0)),
                  vec],
        out_specs=tok,
        out_shape=jax.ShapeDtypeStruct((b, t, dm), F32),
        scratch_shapes=[pltpu.VMEM((tm, dm), BF16), pltpu.VMEM((tm, dm), F32)],
        compiler_params=_cparams(("parallel", "parallel", "arbitrary")),
        name="ffn",
    )(h, g, sh, sc, gate, w_gu, w_gu, w_down, fg)


def _rope_tables(t, rot_dim):
    nf = rot_dim // 4
    pos = jnp.arange(t)
    row = (pos // GRID_W).astype(F32)
    col = (pos % GRID_W).astype(F32)
    inv_freq = ROPE_THETA ** (-jnp.arange(nf, dtype=F32) / nf)
    ar, ac = row[:, None] * inv_freq, col[:, None] * inv_freq
    z = jnp.zeros_like(ar)
    cos = jnp.concatenate([jnp.cos(ar), jnp.cos(ar), jnp.cos(ac), jnp.cos(ac)], axis=1)
    sa = jnp.concatenate([-jnp.sin(ar), z, -jnp.sin(ac), z], axis=1)
    sb = jnp.concatenate([z, jnp.sin(ar), z, jnp.sin(ac)], axis=1)
    rep = LANES // rot_dim
    return tuple(jnp.tile(a, (1, rep)) for a in (cos, sa, sb))


def _pad_lanes(v):
    return jnp.pad(v.reshape(1, -1), ((0, 0), (0, LANES - v.size)))


def _relayout_w_in(w):
    o = 0
    pieces = []
    for width in (DIFF_WIDTH, DIFF_WIDTH, DIFF_WIDTH):
        pieces.append(w[:, o:o + width]); o += width
    gq = w[:, o:o + GQA_WIDTH].reshape(-1, GQA_Q_HEADS, GQA_HEAD_DIM); o += GQA_WIDTH
    pieces.append(jnp.take(gq, jnp.array(GQA_Q_ORDER), axis=1).reshape(-1, GQA_WIDTH))
    for width in (GQA_KV_WIDTH, GQA_KV_WIDTH, 3 * GDN_WIDTH, GDN_WIDTH):
        pieces.append(w[:, o:o + width]); o += width
    for width in (2 * GDN_HEADS, 2 * GDN_HEADS):
        pieces.append(jnp.pad(w[:, o:o + width], ((0, 0), (0, LANES - width)))); o += width
    return jnp.concatenate(pieces, axis=1).astype(BF16)


def _relayout_w_out(w):
    gq = w[DIFF_WIDTH:DIFF_WIDTH + GQA_WIDTH].reshape(GQA_Q_HEADS, GQA_HEAD_DIM, -1)
    gq = jnp.take(gq, jnp.array(GQA_Q_ORDER), axis=0).reshape(GQA_WIDTH, -1)
    return jnp.concatenate([w[:DIFF_WIDTH], gq, w[DIFF_WIDTH + GQA_WIDTH:]], axis=0).astype(BF16)


def kernel(x, c, ctx, c_ctx, norm1_g, ada_w, ada_b, w_in, diff_lambda, diff_norm_g, q_norm_g, k_norm_g,
           gdn_conv_w, gdn_a_log, gdn_dt_bias, gdn_norm_g, w_out, norm2_g, ffn_w_gu, ffn_w_down, final_norm_g):
    b, t, d = x.shape
    depth = w_in.shape[0]
    rope = _rope_tables(t, DIFF_QK_DIM) + _rope_tables(t, GQA_HEAD_DIM)

    cond = jnp.concatenate([c, c_ctx[None, :], jnp.zeros((16 - b - 1, d), F32)], axis=0)
    mod = _ada(cond, ada_w, ada_b).reshape(depth, 16, 6, d)

    tile2 = lambda v: jnp.tile(v.reshape(1, -1), (1, LANES // v.size))
    zeros_state = jnp.zeros((b, GDN_PAIRS, LANES, LANES), F32)

    h, hc = x, ctx
    for layer in range(depth):
        need_ctx = layer < depth - 1
        lambda_init = 0.8 - 0.6 * math.exp(-0.3 * layer)
        mod_l = [mod[layer, :b, k][:, None, :] for k in range(6)]
        mod_c = [mod[layer, b:b + 1, k][:, None, :] for k in range(6)]
        w_in_l = _relayout_w_in(w_in[layer])
        w_out_l = _relayout_w_out(w_out[layer])
        w_gu_l = ffn_w_gu[layer].astype(BF16)
        w_down_l = ffn_w_down[layer].astype(BF16)
        g1 = norm1_g[layer].reshape(1, d)
        g2 = norm2_g[layer].reshape(1, d)
        qg, kg = tile2(q_norm_g[layer]), tile2(k_norm_g[layer])
        dng, nng = tile2(diff_norm_g[layer]), tile2(gdn_norm_g[layer])
        alog, dtb = _pad_lanes(gdn_a_log[layer]), _pad_lanes(gdn_dt_bias[layer])
        lam = diff_lambda[layer]
        conv_w = gdn_conv_w[layer]

        pl_ = _in_proj(h, g1, mod_l[0], mod_l[1], w_in_l, qg, kg, rope)
        pc_ = _in_proj(hc, g1, mod_c[0], mod_c[1], w_in_l, qg, kg, None)
        dq_l, dk_l, dv_l, gq_l, gk_l, gv_l, nqkv_l, nz_l, na_l, nb_l = pl_
        dq_c, dk_c, dv_c, gq_c, gk_c, gv_c, nqkv_c, nz_c, na_c, nb_c = pc_

        d_l = _attention("diff", dq_l, dk_c, dv_c, dk_l, dv_l, (lam, dng), lambda_init)
        a_l = _attention("gqa", gq_l, gk_c, gv_c, gk_l, gv_l, ())

        fwd_c, bwd_c = _gdn_local(nqkv_c, na_c, nb_c, conv_w, alog, dtb)
        fwd_l, bwd_l = _gdn_local(nqkv_l, na_l, nb_l, conv_w, alog, dtb)
        ocf, ocb, scf, scb = _gdn_scan(fwd_c, bwd_c, zeros_state, zeros_state)
        olf, olb, _, _ = _gdn_scan(fwd_l, bwd_l, scf, scb)

        h = _out_proj(h, d_l, a_l, olf, olb, nz_l, nng, mod_l[2], w_out_l)
        last = layer == depth - 1
        h = _ffn(h, g2, mod_l[3], mod_l[4], mod_l[5], w_gu_l, w_down_l, final_norm_g.reshape(1, d), last)
        if need_ctx:
            d_c = _attention("diff", dq_c, dk_c, dv_c, None, None, (lam, dng), lambda_init)
            a_c = _attention("gqa", gq_c, gk_c, gv_c, None, None, ())
            hc = _out_proj(hc, d_c, a_c, ocf, ocb, nz_c, nng, mod_c[2], w_out_l)
            hc = _ffn(hc, g2, mod_c[3], mod_c[4], mod_c[5], w_gu_l, w_down_l, final_norm_g.reshape(1, d), False)
    return h
```

```python
import functools
import math

import jax
import jax.numpy as jnp
from jax import lax
from jax.experimental import pallas as pl
from jax.experimental.pallas import tpu as pltpu

F32 = jnp.float32
BF16 = jnp.bfloat16

LANES = 128
D_MODEL = 1024
GRID_W = 64
ROPE_THETA = 10000.0
NORM_EPS = 1e-6
L2_EPS = 1e-6
LOG2E = 1.4426950408889634

DIFF_HEADS = 4
DIFF_QK_DIM = 32
DIFF_V_DIM = 64
DIFF_WIDTH = DIFF_HEADS * DIFF_V_DIM
GQA_Q_HEADS = 6
GQA_KV_HEADS = 2
GQA_HEAD_DIM = 64
GQA_WIDTH = GQA_Q_HEADS * GQA_HEAD_DIM
GQA_KV_WIDTH = GQA_KV_HEADS * GQA_HEAD_DIM
GDN_HEADS = 6
GDN_HEAD_DIM = 64
GDN_WIDTH = GDN_HEADS * GDN_HEAD_DIM
GDN_CONV = 5
GDN_CHUNK = 64
GDN_PAIRS = GDN_HEADS // 2
FFN_HIDDEN = 2816

OFF_DQ = 0
OFF_DK = OFF_DQ + DIFF_WIDTH
OFF_DV = OFF_DK + DIFF_WIDTH
OFF_GQ = OFF_DV + DIFF_WIDTH
OFF_GK = OFF_GQ + GQA_WIDTH
OFF_GV = OFF_GK + GQA_KV_WIDTH
OFF_NQKV = OFF_GV + GQA_KV_WIDTH
OFF_NZ = OFF_NQKV + 3 * GDN_WIDTH
OFF_NA = OFF_NZ + GDN_WIDTH
OFF_NB = OFF_NA + LANES
IN_PAD = OFF_NB + LANES

GQA_Q_ORDER = (0, 3, 1, 4, 2, 5)
HEAD_V = 64
ONES_ROWS = 16
V_ROWS = HEAD_V + ONES_ROWS

GDN_INVERSE_PASSES = 1
GDN_SOLVE_PASSES = 1
FFN_CHUNK = 512
FFN_SKEW = 1
GDN_CHUNKS_PER_ITER = 4

ATTN_SKEW = 3
ATTN_CHUNKS_PER_ITER = 8

VMEM_LIMIT = 56 * 1024 * 1024


def _cparams(sem):
    return pltpu.CompilerParams(dimension_semantics=sem, vmem_limit_bytes=VMEM_LIMIT)


def _dot(a, b):
    return jnp.dot(a, b, preferred_element_type=F32)


def _dot_nt(a, b):
    return lax.dot_general(a, b, (((1,), (1,)), ((), ())), preferred_element_type=F32)


def _split_bf16(x):
    hi = x.astype(BF16)
    lo = (x - hi.astype(F32)).astype(BF16)
    return hi, lo


def _dot3(a, b):
    ah, al = _split_bf16(a)
    bh, bl = _split_bf16(b)
    return _dot(ah, bh) + _dot(ah, bl) + _dot(al, bh)


def _dotn(a, b, passes):
    if passes == 1:
        return _dot(a.astype(BF16), b.astype(BF16))
    assert passes == 3
    return _dot3(a, b)


def _group_matrix(scale):
    r = lax.broadcasted_iota(jnp.int32, (LANES, LANES), 0) // 64
    c = lax.broadcasted_iota(jnp.int32, (LANES, LANES), 1) // 64
    return jnp.where(r == c, scale, 0.0).astype(BF16)


def _group_sum(x, gm):
    hi, lo = _split_bf16(x)
    return _dot(hi, gm) + _dot(lo, gm)


def _rope(x, c, sa, sb, half):
    return x * c + pltpu.roll(x, LANES - half, 1) * sa + pltpu.roll(x, half, 1) * sb


def _ada_kernel(c_ref, w_ref, b_ref, o_ref):
    s = jax.nn.silu(c_ref[...]).astype(BF16)
    o_ref[...] = _dot(s, w_ref[...].astype(BF16)) + b_ref[...]


def _ada(cond, ada_w, ada_b):
    depth, d, n = ada_w.shape
    rows = cond.shape[0]
    tn = 1536
    return pl.pallas_call(
        _ada_kernel,
        grid=(depth, n // tn),
        in_specs=[
            pl.BlockSpec((rows, d), lambda l, j: (0, 0)),
            pl.BlockSpec((None, d, tn), lambda l, j: (l, 0, j)),
            pl.BlockSpec((None, 1, tn), lambda l, j: (l, 0, j)),
        ],
        out_specs=pl.BlockSpec((None, rows, tn), lambda l, j: (l, 0, j)),
        out_shape=jax.ShapeDtypeStruct((depth, rows, n), F32),
        compiler_params=_cparams(("parallel", "parallel")),
        name="ada_mod",
    )(cond, ada_w, ada_b.reshape(depth, 1, n))


def _store_values(ref, first_head, v):
    vt = v.T.astype(BF16)
    ones = jnp.ones((ONES_ROWS, vt.shape[1]), BF16)
    for s in range(2):
        r0 = (first_head + s) * V_ROWS
        ref[r0:r0 + HEAD_V, :] = vt[s * HEAD_V:(s + 1) * HEAD_V]
        ref[r0 + HEAD_V:r0 + V_ROWS, :] = ones


def _in_proj_kernel(*refs, use_rope):
    if use_rope:
        (x_ref, g_ref, sh_ref, sc_ref, w_ref, qg_ref, kg_ref,
         cd_ref, sad_ref, sbd_ref, cg_ref, sag_ref, sbg_ref,
         dq_ref, dk_ref, dv_ref, gq_ref, gk_ref, gv_ref, nqkv_ref, nz_ref, na_ref, nb_ref) = refs
    else:
        (x_ref, g_ref, sh_ref, sc_ref, w_ref, qg_ref, kg_ref,
         dq_ref, dk_ref, dv_ref, gq_ref, gk_ref, gv_ref, nqkv_ref, nz_ref, na_ref, nb_ref) = refs
    x = x_ref[...]
    y = x * lax.rsqrt(jnp.mean(x * x, axis=-1, keepdims=True) + NORM_EPS) * g_ref[...]
    a = (y * (1.0 + sc_ref[...]) + sh_ref[...]).astype(BF16)

    bounds = (OFF_DQ, OFF_DK, OFF_DV, OFF_GQ, OFF_GK, OFF_NQKV, OFF_NZ, OFF_NA, IN_PAD)
    groups = {}

    def proj(lo, width):
        g0 = max(b for b in bounds if b <= lo)
        g1 = min(b for b in bounds if b > lo)
        assert lo + width <= g1
        if g0 not in groups:
            groups[g0] = _dot(a, w_ref[:, g0:g1])
        return groups[g0][:, lo - g0:lo - g0 + width]

    gm = _group_matrix(1.0 / GQA_HEAD_DIM)

    for off, out, scale in ((OFF_DQ, dq_ref, DIFF_QK_DIM ** -0.5 * LOG2E), (OFF_DK, dk_ref, 1.0)):
        for blk in range(DIFF_WIDTH // LANES):
            p = proj(off + blk * LANES, LANES)
            if use_rope:
                p = _rope(p, cd_ref[...], sad_ref[...], sbd_ref[...], DIFF_QK_DIM // 4)
            out[:, blk * LANES:(blk + 1) * LANES] = (p * scale).astype(BF16)
    for blk in range(DIFF_WIDTH // LANES):
        _store_values(dv_ref, 2 * blk, proj(OFF_DV + blk * LANES, LANES))

    def qk_prep(p, gain, scale):
        ms = _group_sum(p * p, gm)
        p = p * lax.rsqrt(ms + NORM_EPS) * gain
        if use_rope:
            p = _rope(p, cg_ref[...], sag_ref[...], sbg_ref[...], GQA_HEAD_DIM // 4)
        return (p * scale).astype(BF16)

    for blk in range(GQA_WIDTH // LANES):
        p = proj(OFF_GQ + blk * LANES, LANES)
        gq_ref[:, blk * LANES:(blk + 1) * LANES] = qk_prep(p, qg_ref[...], GQA_HEAD_DIM ** -0.5 * LOG2E)
    gk_ref[...] = qk_prep(proj(OFF_GK, LANES), kg_ref[...], 1.0)
    _store_values(gv_ref, 0, proj(OFF_GV, LANES))

    for blk in range(3 * GDN_WIDTH // 384):
        nqkv_ref[:, blk * 384:(blk + 1) * 384] = proj(OFF_NQKV + blk * 384, 384)
    nz_ref[...] = proj(OFF_NZ, GDN_WIDTH)
    na_ref[...] = proj(OFF_NA, LANES)
    nb_ref[...] = proj(OFF_NB, LANES)


def _in_proj(x, g, sh, sc, w, qg, kg, rope):
    b, t, d = x.shape
    tm = min(512, t)
    nb = sh.shape[0]
    mod_map = (lambda i, j: (i, 0, 0)) if nb == b else (lambda i, j: (0, 0, 0))
    tok = lambda width: pl.BlockSpec((None, tm, width), lambda i, j: (i, j, 0))
    const2 = lambda shape: pl.BlockSpec(shape, lambda i, j: (0, 0))
    in_specs = [
        tok(d), const2((1, d)),
        pl.BlockSpec((None, 1, d), mod_map), pl.BlockSpec((None, 1, d), mod_map),
        const2((d, IN_PAD)), const2((1, LANES)), const2((1, LANES)),
    ]
    args = [x, g, sh, sc, w, qg, kg]
    if rope is not None:
        in_specs += [pl.BlockSpec((tm, LANES), lambda i, j: (j, 0))] * 6
        args += list(rope)
    widths = (DIFF_WIDTH, DIFF_WIDTH, DIFF_HEADS * V_ROWS, GQA_WIDTH, GQA_KV_WIDTH, GQA_KV_HEADS * V_ROWS,
              3 * GDN_WIDTH, GDN_WIDTH, LANES, LANES)
    dtypes = (BF16,) * 6 + (F32,) * 4
    transposed = (2, 5)
    out_specs = [pl.BlockSpec((None, None, wd, tm), lambda i, j: (i, j, 0, 0)) if k in transposed else tok(wd)
                 for k, wd in enumerate(widths)]
    out_shape = [jax.ShapeDtypeStruct((b, t // tm, wd, tm) if k in transposed else (b, t, wd), dt)
                 for k, (wd, dt) in enumerate(zip(widths, dtypes))]
    return pl.pallas_call(
        functools.partial(_in_proj_kernel, use_rope=rope is not None),
        grid=(b, t // tm),
        in_specs=in_specs,
        out_specs=out_specs,
        out_shape=out_shape,
        compiler_params=_cparams(("parallel", "parallel")),
        name="in_proj",
    )(*args)


def _flash_all(cols, vrows, qm_s, m_s, acc_s, kc_ref, vc_ref, kl_ref, vl_ref, n_lat):
    nh = len(cols)
    m_s[...] = jnp.full(m_s.shape, -jnp.inf, F32)
    acc_s[...] = jnp.zeros(acc_s.shape, F32)

    def pipeline(chunk_list):
        items = [(ck, h) for ck in chunk_list for h in range(nh)]

        def scores(item):
            (k_ref, _, tk, j), h = item
            if tk is None:
                rows = slice(None)
            elif isinstance(j, int):
                rows = slice(j * tk, (j + 1) * tk)
            else:
                rows = pl.ds(pl.multiple_of(j * tk, tk), tk)
            return _dot_nt(k_ref[rows, cols[h]:cols[h] + LANES], qm_s[h])

        pending = [scores(it) for it in items[:ATTN_SKEW]]
        for n, ((_, v_ref, _, j), h) in enumerate(items):
            s = pending.pop(0)
            if n + ATTN_SKEW < len(items):
                pending.append(scores(items[n + ATTN_SKEW]))
            m_old = m_s[h]
            m_new = jnp.maximum(m_old, jnp.max(s, axis=0, keepdims=True))
            alpha = jnp.exp2(m_old - m_new)
            p = jnp.exp2(s - m_new)
            acc_s[h] = alpha * acc_s[h] + _dot(v_ref[j, vrows[h]:vrows[h] + V_ROWS, :], p.astype(BF16))
            m_s[h] = m_new

    ctx_chunk = (kc_ref, vc_ref, None, 0)
    if not n_lat:
        pipeline([ctx_chunk])
        return
    tk = vl_ref.shape[-1]
    per_iter = math.gcd(ATTN_CHUNKS_PER_ITER, n_lat)
    if per_iter == n_lat:
        pipeline([ctx_chunk] + [(kl_ref, vl_ref, tk, j) for j in range(n_lat)])
        return
    pipeline([ctx_chunk])

    def body(i, carry):
        pipeline([(kl_ref, vl_ref, tk, i * per_iter + jj) for jj in range(per_iter)])
        return carry

    lax.fori_loop(0, n_lat // per_iter, body, 0)


def _normalised(acc_s, h):
    acc = acc_s[h]
    return acc[0:HEAD_V] / acc[HEAD_V:HEAD_V + 1]


def _pair_out(oa, ob):
    return jnp.concatenate([oa, ob], axis=0).T


def _gqa_kernel(*refs, n_lat):
    if n_lat:
        q_ref, kc_ref, vc_ref, kl_ref, vl_ref, o_ref, qm_s, m_s, acc_s = refs
    else:
        q_ref, kc_ref, vc_ref, o_ref, qm_s, m_s, acc_s = refs
        kl_ref = vl_ref = None
    tq = q_ref.shape[0]
    nblk = GQA_WIDTH // LANES
    lo = lax.broadcasted_iota(jnp.int32, (tq, LANES), 1) < GQA_HEAD_DIM
    for blk in range(nblk):
        qb = q_ref[:, blk * LANES:(blk + 1) * LANES]
        zero = jnp.zeros_like(qb)
        qm_s[2 * blk] = jnp.where(lo, qb, zero)
        qm_s[2 * blk + 1] = jnp.where(lo, zero, qb)
    _flash_all((0,) * (2 * nblk), (0, V_ROWS) * nblk, qm_s, m_s, acc_s, kc_ref, vc_ref, kl_ref, vl_ref, n_lat)
    for blk in range(nblk):
        o = _pair_out(_normalised(acc_s, 2 * blk), _normalised(acc_s, 2 * blk + 1))
        o_ref[:, blk * LANES:(blk + 1) * LANES] = o.astype(BF16)


def _diff_kernel(*refs, n_lat, lambda_init):
    if n_lat:
        q_ref, kc_ref, vc_ref, kl_ref, vl_ref, lam_ref, ng_ref, o_ref, qm_s, m_s, acc_s = refs
    else:
        q_ref, kc_ref, vc_ref, lam_ref, ng_ref, o_ref, qm_s, m_s, acc_s = refs
        kl_ref = vl_ref = None
    tq = q_ref.shape[0]
    nblk = DIFF_WIDTH // LANES
    lf = lam_ref[...]
    lam = (jnp.exp(jnp.sum(lf[0:1] * lf[1:2], axis=-1, keepdims=True))
           - jnp.exp(jnp.sum(lf[2:3] * lf[3:4], axis=-1, keepdims=True)) + lambda_init)
    lane = lax.broadcasted_iota(jnp.int32, (tq, LANES), 1)
    lo = lane < DIFF_V_DIM
    gm = _group_matrix(1.0 / DIFF_V_DIM)
    for blk in range(nblk):
        qb = q_ref[:, blk * LANES:(blk + 1) * LANES]
        zero = jnp.zeros_like(qb)
        for sc in range(4):
            qm_s[4 * blk + sc] = jnp.where((lane // DIFF_QK_DIM) == sc, qb, zero)
    cols = tuple(blk * LANES for blk in range(nblk) for _ in range(4))
    vrows = tuple((2 * blk + s) * V_ROWS for blk in range(nblk) for s in range(2) for _ in range(2))
    _flash_all(cols, vrows, qm_s, m_s, acc_s, kc_ref, vc_ref, kl_ref, vl_ref, n_lat)
    for blk in range(nblk):
        halves = []
        for s in range(2):
            i0, i1 = 4 * blk + 2 * s, 4 * blk + 2 * s + 1
            halves.append(_normalised(acc_s, i0) - lam * _normalised(acc_s, i1))
        o = _pair_out(halves[0], halves[1])
        ms = _group_sum(o * o, gm)
        o = o * lax.rsqrt(ms + NORM_EPS) * ng_ref[...] * (1.0 - lambda_init)
        o_ref[:, blk * LANES:(blk + 1) * LANES] = o.astype(BF16)


def _attention(kind, q, kc, vc, kl, vl, extra, lambda_init=None):
    b, t, w = q.shape
    tq = min(256, t)
    full = lambda a: pl.BlockSpec((None,) + a.shape[1:], lambda i, j: (i,) + (0,) * (a.ndim - 1))
    in_specs = [pl.BlockSpec((None, tq, w), lambda i, j: (i, j, 0)), full(kc), full(vc)]
    args = [q, kc, vc]
    n_lat = 0
    if kl is not None:
        n_lat = vl.shape[1]
        in_specs += [full(kl), full(vl)]
        args += [kl, vl]
    for e in extra:
        in_specs.append(pl.BlockSpec(e.shape, lambda i, j: (0, 0)))
        args.append(e)
    if kind == "gqa":
        body = functools.partial(_gqa_kernel, n_lat=n_lat)
        nh = GQA_Q_HEADS
    else:
        body = functools.partial(_diff_kernel, n_lat=n_lat, lambda_init=lambda_init)
        nh = 2 * DIFF_HEADS
    return pl.pallas_call(
        body,
        grid=(b, t // tq),
        in_specs=in_specs,
        out_specs=pl.BlockSpec((None, tq, w), lambda i, j: (i, j, 0)),
        out_shape=jax.ShapeDtypeStruct((b, t, w), BF16),
        scratch_shapes=[pltpu.VMEM((nh, tq, LANES), BF16), pltpu.VMEM((nh, 1, tq), F32),
                        pltpu.VMEM((nh, V_ROWS, tq), F32)],
        compiler_params=_cparams(("parallel", "parallel")),
        name=kind + "_attn",
    )(*args)


def _gdn_local_kernel(x_ref, xp_ref, xn_ref, cw_ref, na_ref, nb_ref, alog_ref, dtb_ref,
                      uf_ref, wf_ref, qdf_ref, kdf_ref, qkf_ref, glf_ref,
                      ub_ref, wb_ref, qdb_ref, kdb_ref, qkb_ref, glb_ref,
                      xbuf, q_s, k_s, v_s, g_s, b_s):
    tm = x_ref.shape[0]
    cs = GDN_CHUNK
    i = pl.program_id(1)
    nblk = pl.num_programs(1)
    halo = xp_ref.shape[0]
    xbuf[halo:halo + tm, :] = x_ref[...]
    xbuf[0:halo, :] = jnp.where(i > 0, xp_ref[...], 0.0)
    xbuf[halo + tm:2 * halo + tm, :] = jnp.where(i < nblk - 1, xn_ref[...], 0.0)
    gm = _group_matrix(1.0)
    for part, dst in enumerate((q_s, k_s, v_s)):
        cols = slice(part * GDN_WIDTH, (part + 1) * GDN_WIDTH)
        acc = None
        xall = xbuf[:, cols]
        for j in range(GDN_CONV):
            d = j - GDN_CONV // 2
            xs = xall if d == 0 else pltpu.roll(xall, (-d) % (tm + 2 * halo), 0)
            term = xs[halo:halo + tm] * cw_ref[j:j + 1, cols]
            acc = term if acc is None else acc + term
        y = jax.nn.silu(acc)
        if part < 2:
            scale = GDN_HEAD_DIM ** -0.5 if part == 0 else 1.0
            for blk in range(GDN_PAIRS):
                yb = y[:, blk * LANES:(blk + 1) * LANES]
                ss = _group_sum(yb * yb, gm)
                dst[:, blk * LANES:(blk + 1) * LANES] = yb * lax.rsqrt(ss + L2_EPS) * scale
        else:
            dst[...] = y
    xa = na_ref[...] + dtb_ref[...]
    softplus = jnp.maximum(xa, 0.0) + jnp.log(1.0 + jnp.exp(-jnp.abs(xa)))
    g_s[...] = -jnp.exp(alog_ref[...]) * softplus
    b_s[...] = jax.nn.sigmoid(nb_ref[...])

    ri = lax.broadcasted_iota(jnp.int32, (cs, LANES), 0)
    li = lax.broadcasted_iota(jnp.int32, (cs, LANES), 1)
    lo = li < 64
    tj = li % 64
    r64 = lax.broadcasted_iota(jnp.int32, (cs, cs), 0)
    c64 = lax.broadcasted_iota(jnp.int32, (cs, cs), 1)
    tri_lo = jnp.where(r64 >= c64, 1.0, 0.0).astype(BF16)
    tri_up = jnp.where(r64 <= c64, 1.0, 0.0).astype(BF16)
    r128 = lax.broadcasted_iota(jnp.int32, (LANES, LANES), 0)
    c128 = lax.broadcasted_iota(jnp.int32, (LANES, LANES), 1)
    eye = jnp.where(r128 == c128, 1.0, 0.0)
    same8 = (r128 // 8) == (c128 // 8)
    level_masks = [((r128 // (2 * m)) == (c128 // (2 * m))) & ((r128 // m) != (c128 // m)) for m in (8, 16, 32)]
    lo1 =lax.broadcasted_iota(jnp.int32, (1, LANES), 1) < 64

    def stack(x, zero):
        return jnp.concatenate([jnp.where(lo, x, zero), jnp.where(lo, zero, x)], axis=0)

    dirs = ((uf_ref, wf_ref, qdf_ref, kdf_ref, qkf_ref, glf_ref),
            (ub_ref, wb_ref, qdb_ref, kdb_ref, qkb_ref, glb_ref))

    def setup(c, chains):
        r0 = pl.multiple_of(c * cs, cs)
        rows = pl.ds(r0, cs)
        g = g_s[rows, :]
        be = b_s[rows, :]
        gh = g.astype(BF16)
        r1 = g - gh.astype(F32)
        gmid = r1.astype(BF16)
        glo = (r1 - gmid.astype(F32)).astype(BF16)
        cum_f = _dot(tri_lo, gh) + _dot(tri_lo, gmid) + _dot(tri_lo, glo)
        cum_b = _dot(tri_up, gh) + _dot(tri_up, gmid) + _dot(tri_up, glo)
        gc = jnp.where(li < GDN_HEADS, cum_f, cum_b)
        gt = jnp.concatenate([gc, gc], axis=0).T
        for p in range(GDN_PAIRS):
            cols = slice(p * LANES, (p + 1) * LANES)
            q128 = q_s[rows, cols]
            k128 = k_s[rows, cols]
            v128 = v_s[rows, cols]
            kb = k128.astype(BF16)
            kstack = stack(kb, jnp.zeros_like(kb))
            kk = _dot_nt(kb, kstack)
            qk = _dot_nt(q128.astype(BF16), kstack)
            for rev in range(2):
                la = rev * GDN_HEADS + 2 * p
                gca, gcb = gc[:, la:la + 1], gc[:, la + 1:la + 2]
                bca, bcb = be[:, la:la + 1], be[:, la + 1:la + 2]
                gcol = jnp.where(lo, gca, gcb)
                grow = jnp.where(lo1, gt[la:la + 1, :], gt[la + 1:la + 2, :])
                bcol = jnp.where(lo, bca, bcb)
                if rev:
                    incl, strict = ri <= tj, ri < tj
                    last = 0
                else:
                    incl, strict = ri >= tj, ri > tj
                    last = cs - 1
                decay = jnp.exp(jnp.where(incl, gcol - grow, -jnp.inf))
                a128 = jnp.where(strict, kk * decay * bcol, 0.0)
                n = -stack(a128, 0.0)
                ea, eb = jnp.exp(gca), jnp.exp(gcb)
                rhs = jnp.concatenate([
                    jnp.concatenate([v128 * bca, v128 * bcb], axis=0),
                    jnp.concatenate([k128 * (bca * ea), k128 * (bcb * eb)], axis=0)], axis=1)
                glast = jnp.where(lo1, gc[last:last + 1, la:la + 1], gc[last:last + 1, la + 1:la + 2])
                u_ref, w_ref, qd_ref, kd_ref, qk_ref, gl_ref = dirs[rev]
                qd_ref[rows, cols] = (q128 * jnp.exp(gcol)).astype(BF16)
                kd_ref[rows, cols] = (k128 * jnp.exp(glast - gcol)).astype(BF16)
                qk_ref[rows, cols] = (qk * decay).astype(BF16)
                gl_ref[c, :, cols] = jnp.broadcast_to(jnp.exp(glast), (8, LANES))
                chains.append(dict(n=n, rhs=rhs, rows=rows, cols=cols, rev=rev))

    def group(i, carry):
        chains = []
        for gi in range(per_iter):
            setup(i * per_iter + gi, chains)
        for ch in chains:
            d0 = jnp.where(same8, ch["n"], 0.0)
            ch["t"] = eye + d0
            ch["pw"] = _dotn(d0, d0, GDN_INVERSE_PASSES)
        for ch in chains:
            both = _dotn(jnp.concatenate([ch["t"], ch["pw"]], axis=0), ch["pw"], GDN_INVERSE_PASSES)
            ch["t"] = ch["t"] + both[0:LANES]
            ch["pw"] = both[LANES:2 * LANES]
        for ch in chains:
            ch["t"] = ch["t"] + _dotn(ch["t"], ch["pw"], GDN_INVERSE_PASSES)
        for off_mask in level_masks:
            for ch in chains:
                ch["x"] = _dotn(jnp.where(off_mask, ch["n"], 0.0), ch["t"], GDN_INVERSE_PASSES)
            for ch in chains:
                ch["t"] = ch["t"] + _dotn(ch["t"], ch["x"], GDN_INVERSE_PASSES)
        for ch in chains:
            sol = _dotn(ch["t"], ch["rhs"], GDN_SOLVE_PASSES)
            u_ref, w_ref = dirs[ch["rev"]][0:2]
            u_ref[ch["rows"], ch["cols"]] = jnp.where(lo, sol[0:cs, 0:LANES], sol[cs:2 * cs, 0:LANES])
            w_ref[ch["rows"], ch["cols"]] = jnp.where(lo, sol[0:cs, LANES:], sol[cs:2 * cs, LANES:]).astype(BF16)
        return carry

    per_iter = math.gcd(GDN_CHUNKS_PER_ITER, tm // cs)
    lax.fori_loop(0, tm // (cs * per_iter), group, 0)


def _gdn_local(nqkv, na, nb, conv_w, alog, dtb):
    b, t, w3 = nqkv.shape
    tm = min(256, t)
    nblk = t // tm
    cpb = tm // GDN_CHUNK
    halo = 8
    hb = tm // halo
    tok = lambda width: pl.BlockSpec((None, tm, width), lambda i, j: (i, j, 0))
    const2 = lambda shape: pl.BlockSpec(shape, lambda i, j: (0, 0))
    in_specs = [
        tok(w3),
        pl.BlockSpec((None, halo, w3), lambda i, j: (i, jnp.maximum(j * hb - 1, 0), 0)),
        pl.BlockSpec((None, halo, w3), lambda i, j: (i, jnp.minimum((j + 1) * hb, t // halo - 1), 0)),
        const2((GDN_CONV, w3)), tok(LANES), tok(LANES), const2((1, LANES)), const2((1, LANES)),
    ]
    per_dir_specs = [tok(GDN_WIDTH)] * 5 + [pl.BlockSpec((None, cpb, 8, GDN_WIDTH), lambda i, j: (i, j, 0, 0))]
    per_dir_shapes = ([jax.ShapeDtypeStruct((b, t, GDN_WIDTH), F32)]
                      + [jax.ShapeDtypeStruct((b, t, GDN_WIDTH), BF16)] * 4
                      + [jax.ShapeDtypeStruct((b, t // GDN_CHUNK, 8, GDN_WIDTH), F32)])
    outs = pl.pallas_call(
        _gdn_local_kernel,
        grid=(b, nblk),
        in_specs=in_specs,
        out_specs=per_dir_specs * 2,
        out_shape=per_dir_shapes * 2,
        scratch_shapes=[
            pltpu.VMEM((tm + 2 * halo, w3), F32),
            pltpu.VMEM((tm, GDN_WIDTH), F32), pltpu.VMEM((tm, GDN_WIDTH), F32), pltpu.VMEM((tm, GDN_WIDTH), F32),
            pltpu.VMEM((tm, LANES), F32), pltpu.VMEM((tm, LANES), F32),
        ],
        compiler_params=_cparams(("parallel", "parallel")),
        name="gdn_local",
    )(nqkv, nqkv, nqkv, conv_w, na, nb, alog, dtb)
    return outs[:6], outs[6:]


def _gdn_scan_kernel(uf_ref, wf_ref, qdf_ref, kdf_ref, qkf_ref, glf_ref,
                     ub_ref, wb_ref, qdb_ref, kdb_ref, qkb_ref, glb_ref,
                     s0f_ref, s0b_ref,
                     of_ref, ob_ref, sff_ref, sfb_ref,
                     stf, stb):
    n = pl.program_id(1)
    cs = GDN_CHUNK
    cpb = uf_ref.shape[0] // cs

    @pl.when(n == 0)
    def _():
        stf[...] = s0f_ref[...]
        stb[...] = s0b_ref[...]

    lo = lax.broadcasted_iota(jnp.int32, (cs, LANES), 1) < 64
    r128 = lax.broadcasted_iota(jnp.int32, (LANES, LANES), 0) // 64
    c128 = lax.broadcasted_iota(jnp.int32, (LANES, LANES), 1) // 64
    bd = r128 == c128

    fwd = (uf_ref, wf_ref, qdf_ref, kdf_ref, qkf_ref, glf_ref, of_ref, stf)
    bwd = (ub_ref, wb_ref, qdb_ref, kdb_ref, qkb_ref, glb_ref, ob_ref, stb)

    for cc in range(cpb):
        chains = []
        for c, (u_ref, w_ref, qd_ref, kd_ref, qk_ref, gl_ref, o_ref, st) in ((cc, fwd), (cpb - 1 - cc, bwd)):
            rows = slice(c * cs, (c + 1) * cs)
            for p in range(GDN_PAIRS):
                cols = slice(p * LANES, (p + 1) * LANES)
                s = st[p]
                sb = s.astype(BF16)
                ws = _dot_nt(w_ref[rows, cols], sb)
                qs = _dot_nt(qd_ref[rows, cols], sb)
                chains.append((c, rows, cols, p, s, ws, qs, u_ref, kd_ref, qk_ref, gl_ref, o_ref, st))
        for c, rows, cols, p, s, ws, qs, u_ref, kd_ref, qk_ref, gl_ref, o_ref, st in chains:
            vnew = u_ref[rows, cols] - ws
            vb = vnew.astype(BF16)
            zero = jnp.zeros_like(vb)
            vstack = jnp.concatenate([jnp.where(lo, vb, zero), jnp.where(lo, zero, vb)], axis=0)
            o_ref[rows, cols] = qs + _dot(qk_ref[rows, cols], vstack)
            upd = _dot(vnew.T.astype(BF16), kd_ref[rows, cols])
            st[p] = s * gl_ref[c, 0:1, cols] + jnp.where(bd, upd, 0.0)

    @pl.when(n == pl.num_programs(1) - 1)
    def _():
        sff_ref[...] = stf[...]
        sfb_ref[...] = stb[...]


def _gdn_scan(fwd, bwd, s0f, s0b):
    b, t, _ = fwd[0].shape
    tm = min(256, t)
    nblk = t // tm
    cpb = tm // GDN_CHUNK
    f_tok = pl.BlockSpec((None, tm, GDN_WIDTH), lambda i, j: (i, j, 0))
    b_tok = pl.BlockSpec((None, tm, GDN_WIDTH), lambda i, j: (i, nblk - 1 - j, 0))
    f_gl = pl.BlockSpec((None, cpb, 8, GDN_WIDTH), lambda i, j: (i, j, 0, 0))
    b_gl = pl.BlockSpec((None, cpb, 8, GDN_WIDTH), lambda i, j: (i, nblk - 1 - j, 0, 0))
    st_spec = pl.BlockSpec((None, GDN_PAIRS, LANES, LANES), lambda i, j: (i, 0, 0, 0))
    st_shape = jax.ShapeDtypeStruct((b, GDN_PAIRS, LANES, LANES), F32)
    o_shape = jax.ShapeDtypeStruct((b, t, GDN_WIDTH), F32)
    return pl.pallas_call(
        _gdn_scan_kernel,
        grid=(b, nblk),
        in_specs=[f_tok] * 5 + [f_gl] + [b_tok] * 5 + [b_gl] + [st_spec, st_spec],
        out_specs=[f_tok, b_tok, st_spec, st_spec],
        out_shape=[o_shape, o_shape, st_shape, st_shape],
        scratch_shapes=[pltpu.VMEM((GDN_PAIRS, LANES, LANES), F32)] * 2,
        compiler_params=_cparams(("parallel", "arbitrary")),
        name="gdn_scan",
    )(*fwd, *bwd, s0f, s0b)


def _out_proj_kernel(h_ref, d_ref, gq_ref, of_ref, ob_ref, z_ref, ng_ref, gate_ref, w_ref, o_ref):
    gm = _group_matrix(1.0 / GDN_HEAD_DIM)
    y = _dot(d_ref[...], w_ref[0:DIFF_WIDTH, :])
    y += _dot(gq_ref[...], w_ref[DIFF_WIDTH:DIFF_WIDTH + GQA_WIDTH, :])
    base = DIFF_WIDTH + GQA_WIDTH
    for blk in range(GDN_PAIRS):
        cols = slice(blk * LANES, (blk + 1) * LANES)
        o = of_ref[:, cols] + ob_ref[:, cols]
        ms = _group_sum(o * o, gm)
        r = o * lax.rsqrt(ms + NORM_EPS) * ng_ref[...] * jax.nn.silu(z_ref[:, cols])
        y += _dot(r.astype(BF16), w_ref[base + blk * LANES:base + (blk + 1) * LANES, :])
    o_ref[...] = h_ref[...] + gate_ref[...] * y


def _out_proj(h, d, gq, of, ob, z, ng, gate, w):
    b, t, dm = h.shape
    tm = min(512, t)
    nb = gate.shape[0]
    mod_map = (lambda i, j: (i, 0, 0)) if nb == b else (lambda i, j: (0, 0, 0))
    tok = lambda width: pl.BlockSpec((None, tm, width), lambda i, j: (i, j, 0))
    return pl.pallas_call(
        _out_proj_kernel,
        grid=(b, t // tm),
        in_specs=[tok(dm), tok(DIFF_WIDTH), tok(GQA_WIDTH), tok(GDN_WIDTH), tok(GDN_WIDTH), tok(GDN_WIDTH),
                  pl.BlockSpec((1, LANES), lambda i, j: (0, 0)),
                  pl.BlockSpec((None, 1, dm), mod_map),
                  pl.BlockSpec(w.shape, lambda i, j: (0, 0))],
        out_specs=tok(dm),
        out_shape=jax.ShapeDtypeStruct((b, t, dm), F32),
        compiler_params=_cparams(("parallel", "parallel")),
        name="out_proj",
    )(h, d, gq, of, ob, z, ng, gate, w)


def _ffn_kernel(h_ref, g_ref, sh_ref, sc_ref, gate_ref, wgu_ref, wd_ref, fg_ref, o_ref, *, final_norm):
    x = h_ref[...]
    y = x * lax.rsqrt(jnp.mean(x * x, axis=-1, keepdims=True) + NORM_EPS) * g_ref[...]
    a = (y * (1.0 + sc_ref[...]) + sh_ref[...]).astype(BF16)
    hidden = wd_ref.shape[0]
    starts = list(range(0, hidden, FFN_CHUNK))

    def gate_up(lo):
        width = min(FFN_CHUNK, hidden - lo)
        return _dot(a, wgu_ref[:, lo:lo + width]), _dot(a, wgu_ref[:, hidden + lo:hidden + lo + width]), width

    pending = [gate_up(lo) for lo in starts[:FFN_SKEW]]
    acc = None
    for n, lo in enumerate(starts):
        gate, up, width = pending.pop(0)
        if n + FFN_SKEW < len(starts):
            pending.append(gate_up(starts[n + FFN_SKEW]))
        act = (jax.nn.silu(gate) * up).astype(BF16)
        part = _dot(act, wd_ref[lo:lo + width, :])
        acc = part if acc is None else acc + part
    out = x + gate_ref[...] * acc
    if final_norm:
        out = out * lax.rsqrt(jnp.mean(out * out, axis=-1, keepdims=True) + NORM_EPS) * fg_ref[...]
    o_ref[...] = out


def _ffn(h, g, sh, sc, gate, w_gu, w_down, fg, final_norm):
    b, t, dm = h.shape
    tm = min(512, t)
    nb = sh.shape[0]
    mod_map = (lambda i, k: (i, 0, 0)) if nb == b else (lambda i, k: (0, 0, 0))
    tok = pl.BlockSpec((None, tm, dm), lambda i, k: (i, k, 0))
    vec = pl.BlockSpec((1, dm), lambda i, k: (0, 0))
    mod = pl.BlockSpec((None, 1, dm), mod_map)
    resident = lambda w: pl.BlockSpec(w.shape, lambda i, k: (0, 0), pipeline_mode=pl.Buffered(1))
    return pl.pallas_call(
        functools.partial(_ffn_kernel, final_norm=final_norm),
        grid=(b, t // tm),
        in_specs=[tok, vec, mod, mod, mod, resident(w_gu), resident(w_down), vec],
        out_specs=tok,
        out_shape=jax.ShapeDtypeStruct((b, t, dm), F32),
        compiler_params=_cparams(("parallel", "parallel")),
        name="ffn",
    )(h, g, sh, sc, gate, w_gu, w_down, fg)


def _rope_tables(t, rot_dim):
    nf = rot_dim // 4
    pos = jnp.arange(t)
    row = (pos // GRID_W).astype(F32)
    col = (pos % GRID_W).astype(F32)
    inv_freq = ROPE_THETA ** (-jnp.arange(nf, dtype=F32) / nf)
    ar, ac = row[:, None] * inv_freq, col[:, None] * inv_freq
    z = jnp.zeros_like(ar)
    cos = jnp.concatenate([jnp.cos(ar), jnp.cos(ar), jnp.cos(ac), jnp.cos(ac)], axis=1)
    sa = jnp.concatenate([-jnp.sin(ar), z, -jnp.sin(ac), z], axis=1)
    sb = jnp.concatenate([z, jnp.sin(ar), z, jnp.sin(ac)], axis=1)
    rep = LANES // rot_dim
    return tuple(jnp.tile(a, (1, rep)) for a in (cos, sa, sb))


def _pad_lanes(v):
    return jnp.pad(v.reshape(1, -1), ((0, 0), (0, LANES - v.size)))


def _relayout_w_in(w):
    o = 0
    pieces = []
    for width in (DIFF_WIDTH, DIFF_WIDTH, DIFF_WIDTH):
        pieces.append(w[:, o:o + width]); o += width
    gq = w[:, o:o + GQA_WIDTH].reshape(-1, GQA_Q_HEADS, GQA_HEAD_DIM); o += GQA_WIDTH
    pieces.append(jnp.take(gq, jnp.array(GQA_Q_ORDER), axis=1).reshape(-1, GQA_WIDTH))
    for width in (GQA_KV_WIDTH, GQA_KV_WIDTH, 3 * GDN_WIDTH, GDN_WIDTH):
        pieces.append(w[:, o:o + width]); o += width
    for width in (2 * GDN_HEADS, 2 * GDN_HEADS):
        pieces.append(jnp.pad(w[:, o:o + width], ((0, 0), (0, LANES - width)))); o += width
    return jnp.concatenate(pieces, axis=1).astype(BF16)


def _relayout_w_out(w):
    gq = w[DIFF_WIDTH:DIFF_WIDTH + GQA_WIDTH].reshape(GQA_Q_HEADS, GQA_HEAD_DIM, -1)
    gq = jnp.take(gq, jnp.array(GQA_Q_ORDER), axis=0).reshape(GQA_WIDTH, -1)
    return jnp.concatenate([w[:DIFF_WIDTH], gq, w[DIFF_WIDTH + GQA_WIDTH:]], axis=0).astype(BF16)


def kernel(x, c, ctx, c_ctx, norm1_g, ada_w, ada_b, w_in, diff_lambda, diff_norm_g, q_norm_g, k_norm_g,
           gdn_conv_w, gdn_a_log, gdn_dt_bias, gdn_norm_g, w_out, norm2_g, ffn_w_gu, ffn_w_down, final_norm_g):
    b, t, d = x.shape
    depth = w_in.shape[0]
    rope = _rope_tables(t, DIFF_QK_DIM) + _rope_tables(t, GQA_HEAD_DIM)

    cond = jnp.concatenate([c, c_ctx[None, :], jnp.zeros((16 - b - 1, d), F32)], axis=0)
    mod = _ada(cond, ada_w, ada_b).reshape(depth, 16, 6, d)

    tile2 = lambda v: jnp.tile(v.reshape(1, -1), (1, LANES // v.size))
    zeros_state = jnp.zeros((b, GDN_PAIRS, LANES, LANES), F32)

    h, hc = x, ctx
    for layer in range(depth):
        need_ctx = layer < depth - 1
        lambda_init = 0.8 - 0.6 * math.exp(-0.3 * layer)
        mod_l = [mod[layer, :b, k][:, None, :] for k in range(6)]
        mod_c = [mod[layer, b:b + 1, k][:, None, :] for k in range(6)]
        w_in_l = _relayout_w_in(w_in[layer])
        w_out_l = _relayout_w_out(w_out[layer])
        w_gu_l = ffn_w_gu[layer].astype(BF16)
        w_down_l = ffn_w_down[layer].astype(BF16)
        g1 = norm1_g[layer].reshape(1, d)
        g2 = norm2_g[layer].reshape(1, d)
        qg, kg = tile2(q_norm_g[layer]), tile2(k_norm_g[layer])
        dng, nng = tile2(diff_norm_g[layer]), tile2(gdn_norm_g[layer])
        alog, dtb = _pad_lanes(gdn_a_log[layer]), _pad_lanes(gdn_dt_bias[layer])
        lam = diff_lambda[layer]
        conv_w = gdn_conv_w[layer]

        pl_ = _in_proj(h, g1, mod_l[0], mod_l[1], w_in_l, qg, kg, rope)
        pc_ = _in_proj(hc, g1, mod_c[0], mod_c[1], w_in_l, qg, kg, None)
        dq_l, dk_l, dv_l, gq_l, gk_l, gv_l, nqkv_l, nz_l, na_l, nb_l = pl_
        dq_c, dk_c, dv_c, gq_c, gk_c, gv_c, nqkv_c, nz_c, na_c, nb_c = pc_

        d_l = _attention("diff", dq_l, dk_c, dv_c, dk_l, dv_l, (lam, dng), lambda_init)
        a_l = _attention("gqa", gq_l, gk_c, gv_c, gk_l, gv_l, ())

        fwd_c, bwd_c = _gdn_local(nqkv_c, na_c, nb_c, conv_w, alog, dtb)
        fwd_l, bwd_l = _gdn_local(nqkv_l, na_l, nb_l, conv_w, alog, dtb)
        ocf, ocb, scf, scb = _gdn_scan(fwd_c, bwd_c, zeros_state, zeros_state)
        olf, olb, _, _ = _gdn_scan(fwd_l, bwd_l, scf, scb)

        h = _out_proj(h, d_l, a_l, olf, olb, nz_l, nng, mod_l[2], w_out_l)
        last = layer == depth - 1
        h = _ffn(h, g2, mod_l[3], mod_l[4], mod_l[5], w_gu_l, w_down_l, final_norm_g.reshape(1, d), last)
        if need_ctx:
            d_c = _attention("diff", dq_c, dk_c, dv_c, None, None, (lam, dng), lambda_init)
            a_c = _attention("gqa", gq_c, gk_c, gv_c, None, None, ())
            hc = _out_proj(hc, d_c, a_c, ocf, ocb, nz_c, nng, mod_c[2], w_out_l)
            hc = _ffn(hc, g2, mod_c[3], mod_c[4], mod_c[5], w_gu_l, w_down_l, final_norm_g.reshape(1, d), False)
    return h
```

```python
import functools
import math

import jax
import jax.numpy as jnp
from jax import lax
from jax.experimental import pallas as pl
from jax.experimental.pallas import tpu as pltpu

F32 = jnp.float32
BF16 = jnp.bfloat16

LANES = 128
D_MODEL = 1024
GRID_W = 64
ROPE_THETA = 10000.0
NORM_EPS = 1e-6
L2_EPS = 1e-6
LOG2E = 1.4426950408889634

DIFF_HEADS = 4
DIFF_QK_DIM = 32
DIFF_V_DIM = 64
DIFF_WIDTH = DIFF_HEADS * DIFF_V_DIM
GQA_Q_HEADS = 6
GQA_KV_HEADS = 2
GQA_HEAD_DIM = 64
GQA_WIDTH = GQA_Q_HEADS * GQA_HEAD_DIM
GQA_KV_WIDTH = GQA_KV_HEADS * GQA_HEAD_DIM
GDN_HEADS = 6
GDN_HEAD_DIM = 64
GDN_WIDTH = GDN_HEADS * GDN_HEAD_DIM
GDN_CONV = 5
GDN_CHUNK = 64
GDN_PAIRS = GDN_HEADS // 2
FFN_HIDDEN = 2816

OFF_DQ = 0
OFF_DK = OFF_DQ + DIFF_WIDTH
OFF_DV = OFF_DK + DIFF_WIDTH
OFF_GQ = OFF_DV + DIFF_WIDTH
OFF_GK = OFF_GQ + GQA_WIDTH
OFF_GV = OFF_GK + GQA_KV_WIDTH
OFF_NQKV = OFF_GV + GQA_KV_WIDTH
OFF_NZ = OFF_NQKV + 3 * GDN_WIDTH
OFF_NA = OFF_NZ + GDN_WIDTH
OFF_NB = OFF_NA + LANES
IN_PAD = OFF_NB + LANES

GQA_Q_ORDER = (0, 3, 1, 4, 2, 5)
HEAD_V = 64
ONES_ROWS = 16
V_ROWS = HEAD_V + ONES_ROWS

GDN_INVERSE_PASSES = 1
GDN_SOLVE_PASSES = 1
FFN_CHUNK = 512
FFN_SKEW = 1
GDN_SCAN_BATCH = 4
GDN_CHUNKS_PER_ITER = 4

ATTN_SKEW = 3
ATTN_CHUNKS_PER_ITER = 16
ATTN_KEY_CHUNK = 512
ATTN_QUERY_TILE = 256
IN_PROJ_ROWS = 512

VMEM_LIMIT = 56 * 1024 * 1024


def _cparams(sem, flags=None):
    return pltpu.CompilerParams(dimension_semantics=sem, vmem_limit_bytes=VMEM_LIMIT, flags=flags)


def _dot(a, b):
    return jnp.dot(a, b, preferred_element_type=F32)


def _dot_nt(a, b):
    return lax.dot_general(a, b, (((1,), (1,)), ((), ())), preferred_element_type=F32)


def _split_bf16(x):
    hi = x.astype(BF16)
    lo = (x - hi.astype(F32)).astype(BF16)
    return hi, lo


def _dot3(a, b):
    ah, al = _split_bf16(a)
    bh, bl = _split_bf16(b)
    return _dot(ah, bh) + _dot(ah, bl) + _dot(al, bh)


def _dotn(a, b, passes):
    if passes == 1:
        return _dot(a.astype(BF16), b.astype(BF16))
    assert passes == 3
    return _dot3(a, b)


def _group_matrix(scale):
    r = lax.broadcasted_iota(jnp.int32, (LANES, LANES), 0) // 64
    c = lax.broadcasted_iota(jnp.int32, (LANES, LANES), 1) // 64
    return jnp.where(r == c, scale, 0.0).astype(BF16)


def _group_sum(x, gm):
    hi, lo = _split_bf16(x)
    return _dot(hi, gm) + _dot(lo, gm)


def _rope(x, c, sa, sb, half):
    return x * c + pltpu.roll(x, LANES - half, 1) * sa + pltpu.roll(x, half, 1) * sb


def _ada_kernel(c_ref, w_ref, b_ref, o_ref):
    s = jax.nn.silu(c_ref[...]).astype(BF16)
    o_ref[...] = _dot(s, w_ref[...].astype(BF16)) + b_ref[...]


def _ada(cond, ada_w, ada_b):
    depth, d, n = ada_w.shape
    rows = cond.shape[0]
    tn = 1536
    return pl.pallas_call(
        _ada_kernel,
        grid=(depth, n // tn),
        in_specs=[
            pl.BlockSpec((rows, d), lambda l, j: (0, 0)),
            pl.BlockSpec((None, d, tn), lambda l, j: (l, 0, j)),
            pl.BlockSpec((None, 1, tn), lambda l, j: (l, 0, j)),
        ],
        out_specs=pl.BlockSpec((None, rows, tn), lambda l, j: (l, 0, j)),
        out_shape=jax.ShapeDtypeStruct((depth, rows, n), F32),
        compiler_params=_cparams(("parallel", "parallel")),
        name="ada_mod",
    )(cond, ada_w, ada_b.reshape(depth, 1, n))


def _store_values(ref, first_head, v):
    vt = v.T.astype(BF16)
    n_chunks, _, tk = ref.shape
    ones = jnp.ones((ONES_ROWS, tk), BF16)
    for c in range(n_chunks):
        for s in range(2):
            r0 = (first_head + s) * V_ROWS
            ref[c, r0:r0 + HEAD_V, :] = vt[s * HEAD_V:(s + 1) * HEAD_V, c * tk:(c + 1) * tk]
            ref[c, r0 + HEAD_V:r0 + V_ROWS, :] = ones


def _in_proj_kernel(*refs, use_rope):
    if use_rope:
        (x_ref, g_ref, sh_ref, sc_ref, w_ref, qg_ref, kg_ref,
         cd_ref, sad_ref, sbd_ref, cg_ref, sag_ref, sbg_ref,
         dq_ref, dk_ref, dv_ref, gq_ref, gk_ref, gv_ref, nqkv_ref, nz_ref, na_ref, nb_ref) = refs
    else:
        (x_ref, g_ref, sh_ref, sc_ref, w_ref, qg_ref, kg_ref,
         dq_ref, dk_ref, dv_ref, gq_ref, gk_ref, gv_ref, nqkv_ref, nz_ref, na_ref, nb_ref) = refs
    x = x_ref[...]
    y = x * lax.rsqrt(jnp.mean(x * x, axis=-1, keepdims=True) + NORM_EPS) * g_ref[...]
    a = (y * (1.0 + sc_ref[...]) + sh_ref[...]).astype(BF16)

    bounds = (OFF_DQ, OFF_DK, OFF_DV, OFF_GQ, OFF_GK, OFF_NQKV, OFF_NZ, OFF_NA, IN_PAD)
    groups = {}

    def proj(lo, width):
        g0 = max(b for b in bounds if b <= lo)
        g1 = min(b for b in bounds if b > lo)
        assert lo + width <= g1
        if g0 not in groups:
            groups[g0] = _dot(a, w_ref[:, g0:g1])
        return groups[g0][:, lo - g0:lo - g0 + width]

    gm = _group_matrix(1.0 / GQA_HEAD_DIM)

    for off, out, scale in ((OFF_DQ, dq_ref, DIFF_QK_DIM ** -0.5 * LOG2E), (OFF_DK, dk_ref, 1.0)):
        for blk in range(DIFF_WIDTH // LANES):
            p = proj(off + blk * LANES, LANES)
            if use_rope:
                p = _rope(p, cd_ref[...], sad_ref[...], sbd_ref[...], DIFF_QK_DIM // 4)
            out[:, blk * LANES:(blk + 1) * LANES] = (p * scale).astype(BF16)
    for blk in range(DIFF_WIDTH // LANES):
        _store_values(dv_ref, 2 * blk, proj(OFF_DV + blk * LANES, LANES))

    def qk_prep(p, gain, scale):
        ms = _group_sum(p * p, gm)
        p = p * lax.rsqrt(ms + NORM_EPS) * gain
        if use_rope:
            p = _rope(p, cg_ref[...], sag_ref[...], sbg_ref[...], GQA_HEAD_DIM // 4)
        return (p * scale).astype(BF16)

    for blk in range(GQA_WIDTH // LANES):
        p = proj(OFF_GQ + blk * LANES, LANES)
        gq_ref[:, blk * LANES:(blk + 1) * LANES] = qk_prep(p, qg_ref[...], GQA_HEAD_DIM ** -0.5 * LOG2E)
    gk_ref[...] = qk_prep(proj(OFF_GK, LANES), kg_ref[...], 1.0)
    _store_values(gv_ref, 0, proj(OFF_GV, LANES))

    for blk in range(3 * GDN_WIDTH // 384):
        nqkv_ref[:, blk * 384:(blk + 1) * 384] = proj(OFF_NQKV + blk * 384, 384)
    nz_ref[...] = proj(OFF_NZ, GDN_WIDTH)
    na_ref[...] = proj(OFF_NA, LANES)
    nb_ref[...] = proj(OFF_NB, LANES)


def _in_proj(x, g, sh, sc, w, qg, kg, rope):
    b, t, d = x.shape
    tm = min(IN_PROJ_ROWS, t)
    nb = sh.shape[0]
    mod_map = (lambda i, j: (i, 0, 0)) if nb == b else (lambda i, j: (0, 0, 0))
    tok = lambda width: pl.BlockSpec((None, tm, width), lambda i, j: (i, j, 0))
    const2 = lambda shape: pl.BlockSpec(shape, lambda i, j: (0, 0))
    in_specs = [
        tok(d), const2((1, d)),
        pl.BlockSpec((None, 1, d), mod_map), pl.BlockSpec((None, 1, d), mod_map),
        const2((d, IN_PAD)), const2((1, LANES)), const2((1, LANES)),
    ]
    args = [x, g, sh, sc, w, qg, kg]
    if rope is not None:
        in_specs += [pl.BlockSpec((tm, LANES), lambda i, j: (j, 0))] * 6
        args += list(rope)
    widths = (DIFF_WIDTH, DIFF_WIDTH, DIFF_HEADS * V_ROWS, GQA_WIDTH, GQA_KV_WIDTH, GQA_KV_HEADS * V_ROWS,
              3 * GDN_WIDTH, GDN_WIDTH, LANES, LANES)
    dtypes = (BF16,) * 6 + (F32,) * 4
    transposed = (2, 5)
    tk = min(ATTN_KEY_CHUNK, tm)
    out_specs = [pl.BlockSpec((None, tm // tk, wd, tk), lambda i, j: (i, j, 0, 0)) if k in transposed else tok(wd)
                 for k, wd in enumerate(widths)]
    out_shape = [jax.ShapeDtypeStruct((b, t // tk, wd, tk) if k in transposed else (b, t, wd), dt)
                 for k, (wd, dt) in enumerate(zip(widths, dtypes))]
    return pl.pallas_call(
        functools.partial(_in_proj_kernel, use_rope=rope is not None),
        grid=(b, t // tm),
        in_specs=in_specs,
        out_specs=out_specs,
        out_shape=out_shape,
        compiler_params=_cparams(("parallel", "parallel")),
        name="in_proj",
    )(*args)


def _flash_all(cols, vrows, qm_s, m_s, acc_s, kc_ref, vc_ref, kl_ref, vl_ref, n_lat):
    nh = len(cols)
    m_s[...] = jnp.full(m_s.shape, -jnp.inf, F32)
    acc_s[...] = jnp.zeros(acc_s.shape, F32)

    def pipeline(chunk_list):
        items = [(ck, h) for ck in chunk_list for h in range(nh)]

        def scores(item):
            (k_ref, _, tk, j), h = item
            if tk is None:
                rows = slice(None)
            elif isinstance(j, int):
                rows = slice(j * tk, (j + 1) * tk)
            else:
                rows = pl.ds(pl.multiple_of(j * tk, tk), tk)
            return _dot_nt(k_ref[rows, cols[h]:cols[h] + LANES], qm_s[h])

        pending = [scores(it) for it in items[:ATTN_SKEW]]
        for n, ((_, v_ref, _, j), h) in enumerate(items):
            s = pending.pop(0)
            if n + ATTN_SKEW < len(items):
                pending.append(scores(items[n + ATTN_SKEW]))
            m_old = m_s[h]
            m_new = jnp.maximum(m_old, jnp.max(s, axis=0, keepdims=True))
            alpha = jnp.exp2(m_old - m_new)
            p = jnp.exp2(s - m_new)
            acc_s[h] = alpha * acc_s[h] + _dot(v_ref[j, vrows[h]:vrows[h] + V_ROWS, :], p.astype(BF16))
            m_s[h] = m_new

    ctx_chunk = (kc_ref, vc_ref, None, 0)
    if not n_lat:
        pipeline([ctx_chunk])
        return
    tk = vl_ref.shape[-1]
    per_iter = math.gcd(ATTN_CHUNKS_PER_ITER, n_lat)
    if per_iter == n_lat:
        pipeline([ctx_chunk] + [(kl_ref, vl_ref, tk, j) for j in range(n_lat)])
        return
    pipeline([ctx_chunk])

    def body(i, carry):
        pipeline([(kl_ref, vl_ref, tk, i * per_iter + jj) for jj in range(per_iter)])
        return carry

    lax.fori_loop(0, n_lat // per_iter, body, 0)


def _normalised(acc_s, h):
    acc = acc_s[h]
    return acc[0:HEAD_V] / acc[HEAD_V:HEAD_V + 1]


def _pair_out(oa, ob):
    return jnp.concatenate([oa, ob], axis=0).T


def _gqa_kernel(*refs, n_lat):
    if n_lat:
        q_ref, kc_ref, vc_ref, kl_ref, vl_ref, o_ref, qm_s, m_s, acc_s = refs
    else:
        q_ref, kc_ref, vc_ref, o_ref, qm_s, m_s, acc_s = refs
        kl_ref = vl_ref = None
    tq = q_ref.shape[0]
    nblk = GQA_WIDTH // LANES
    lo = lax.broadcasted_iota(jnp.int32, (tq, LANES), 1) < GQA_HEAD_DIM
    for blk in range(nblk):
        qb = q_ref[:, blk * LANES:(blk + 1) * LANES]
        zero = jnp.zeros_like(qb)
        qm_s[2 * blk] = jnp.where(lo, qb, zero)
        qm_s[2 * blk + 1] = jnp.where(lo, zero, qb)
    _flash_all((0,) * (2 * nblk), (0, V_ROWS) * nblk, qm_s, m_s, acc_s, kc_ref, vc_ref, kl_ref, vl_ref, n_lat)
    for blk in range(nblk):
        o = _pair_out(_normalised(acc_s, 2 * blk), _normalised(acc_s, 2 * blk + 1))
        o_ref[:, blk * LANES:(blk + 1) * LANES] = o.astype(BF16)


def _diff_kernel(*refs, n_lat, lambda_init):
    if n_lat:
        q_ref, kc_ref, vc_ref, kl_ref, vl_ref, lam_ref, ng_ref, o_ref, qm_s, m_s, acc_s = refs
    else:
        q_ref, kc_ref, vc_ref, lam_ref, ng_ref, o_ref, qm_s, m_s, acc_s = refs
        kl_ref = vl_ref = None
    tq = q_ref.shape[0]
    nblk = DIFF_WIDTH // LANES
    lf = lam_ref[...]
    lam = (jnp.exp(jnp.sum(lf[0:1] * lf[1:2], axis=-1, keepdims=True))
           - jnp.exp(jnp.sum(lf[2:3] * lf[3:4], axis=-1, keepdims=True)) + lambda_init)
    lane = lax.broadcasted_iota(jnp.int32, (tq, LANES), 1)
    lo = lane < DIFF_V_DIM
    gm = _group_matrix(1.0 / DIFF_V_DIM)
    for blk in range(nblk):
        qb = q_ref[:, blk * LANES:(blk + 1) * LANES]
        zero = jnp.zeros_like(qb)
        for sc in range(4):
            qm_s[4 * blk + sc] = jnp.where((lane // DIFF_QK_DIM) == sc, qb, zero)
    cols = tuple(blk * LANES for blk in range(nblk) for _ in range(4))
    vrows = tuple((2 * blk + s) * V_ROWS for blk in range(nblk) for s in range(2) for _ in range(2))
    _flash_all(cols, vrows, qm_s, m_s, acc_s, kc_ref, vc_ref, kl_ref, vl_ref, n_lat)
    for blk in range(nblk):
        halves = []
        for s in range(2):
            i0, i1 = 4 * blk + 2 * s, 4 * blk + 2 * s + 1
            halves.append(_normalised(acc_s, i0) - lam * _normalised(acc_s, i1))
        o = _pair_out(halves[0], halves[1])
        ms = _group_sum(o * o, gm)
        o = o * lax.rsqrt(ms + NORM_EPS) * ng_ref[...] * (1.0 - lambda_init)
        o_ref[:, blk * LANES:(blk + 1) * LANES] = o.astype(BF16)


def _attention(kind, q, kc, vc, kl, vl, extra, lambda_init=None):
    b, t, w = q.shape
    tq = min(ATTN_QUERY_TILE, t)
    full = lambda a: pl.BlockSpec((None,) + a.shape[1:], lambda i, j: (i,) + (0,) * (a.ndim - 1))
    in_specs = [pl.BlockSpec((None, tq, w), lambda i, j: (i, j, 0)), full(kc), full(vc)]
    args = [q, kc, vc]
    n_lat = 0
    if kl is not None:
        n_lat = vl.shape[1]
        in_specs += [full(kl), full(vl)]
        args += [kl, vl]
    for e in extra:
        in_specs.append(pl.BlockSpec(e.shape, lambda i, j: (0, 0)))
        args.append(e)
    if kind == "gqa":
        body = functools.partial(_gqa_kernel, n_lat=n_lat)
        nh = GQA_Q_HEADS
    else:
        body = functools.partial(_diff_kernel, n_lat=n_lat, lambda_init=lambda_init)
        nh = 2 * DIFF_HEADS
    return pl.pallas_call(
        body,
        grid=(b, t // tq),
        in_specs=in_specs,
        out_specs=pl.BlockSpec((None, tq, w), lambda i, j: (i, j, 0)),
        out_shape=jax.ShapeDtypeStruct((b, t, w), BF16),
        scratch_shapes=[pltpu.VMEM((nh, tq, LANES), BF16), pltpu.VMEM((nh, 1, tq), F32),
                        pltpu.VMEM((nh, V_ROWS, tq), F32)],
        compiler_params=_cparams(("parallel", "parallel")),
        name=kind + "_attn",
    )(*args)


def _gdn_local_kernel(x_ref, xp_ref, xn_ref, cw_ref, na_ref, nb_ref, alog_ref, dtb_ref,
                      uf_ref, wf_ref, qdf_ref, kdf_ref, qkf_ref, glf_ref,
                      ub_ref, wb_ref, qdb_ref, kdb_ref, qkb_ref, glb_ref,
                      xbuf, q_s, k_s, v_s, g_s, b_s):
    tm = x_ref.shape[0]
    cs = GDN_CHUNK
    i = pl.program_id(1)
    nblk = pl.num_programs(1)
    halo = xp_ref.shape[0]
    xbuf[halo:halo + tm, :] = x_ref[...]
    xbuf[0:halo, :] = jnp.where(i > 0, xp_ref[...], 0.0)
    xbuf[halo + tm:2 * halo + tm, :] = jnp.where(i < nblk - 1, xn_ref[...], 0.0)
    gm = _group_matrix(1.0)
    for part, dst in enumerate((q_s, k_s, v_s)):
        cols = slice(part * GDN_WIDTH, (part + 1) * GDN_WIDTH)
        acc = None
        xall = xbuf[:, cols]
        for j in range(GDN_CONV):
            d = j - GDN_CONV // 2
            xs = xall if d == 0 else pltpu.roll(xall, (-d) % (tm + 2 * halo), 0)
            term = xs[halo:halo + tm] * cw_ref[j:j + 1, cols]
            acc = term if acc is None else acc + term
        y = jax.nn.silu(acc)
        if part < 2:
            scale = GDN_HEAD_DIM ** -0.5 if part == 0 else 1.0
            for blk in range(GDN_PAIRS):
                yb = y[:, blk * LANES:(blk + 1) * LANES]
                ss = _group_sum(yb * yb, gm)
                dst[:, blk * LANES:(blk + 1) * LANES] = yb * lax.rsqrt(ss + L2_EPS) * scale
        else:
            dst[...] = y
    xa = na_ref[...] + dtb_ref[...]
    softplus = jnp.maximum(xa, 0.0) + jnp.log(1.0 + jnp.exp(-jnp.abs(xa)))
    g_s[...] = -jnp.exp(alog_ref[...]) * softplus
    b_s[...] = jax.nn.sigmoid(nb_ref[...])

    ri = lax.broadcasted_iota(jnp.int32, (cs, LANES), 0)
    li = lax.broadcasted_iota(jnp.int32, (cs, LANES), 1)
    lo = li < 64
    tj = li % 64
    r64 = lax.broadcasted_iota(jnp.int32, (cs, cs), 0)
    c64 = lax.broadcasted_iota(jnp.int32, (cs, cs), 1)
    tri_lo = jnp.where(r64 >= c64, 1.0, 0.0).astype(BF16)
    tri_up = jnp.where(r64 <= c64, 1.0, 0.0).astype(BF16)
    r128 = lax.broadcasted_iota(jnp.int32, (LANES, LANES), 0)
    c128 = lax.broadcasted_iota(jnp.int32, (LANES, LANES), 1)
    eye = jnp.where(r128 == c128, 1.0, 0.0)
    same8 = (r128 // 8) == (c128 // 8)
    level_masks = [((r128 // (2 * m)) == (c128 // (2 * m))) & ((r128 // m) != (c128 // m)) for m in (8, 16, 32)]
    lo1 =lax.broadcasted_iota(jnp.int32, (1, LANES), 1) < 64

    def stack(x, zero):
        return jnp.concatenate([jnp.where(lo, x, zero), jnp.where(lo, zero, x)], axis=0)

    dirs = ((uf_ref, wf_ref, qdf_ref, kdf_ref, qkf_ref, glf_ref),
            (ub_ref, wb_ref, qdb_ref, kdb_ref, qkb_ref, glb_ref))

    def setup(c, chains):
        r0 = pl.multiple_of(c * cs, cs)
        rows = pl.ds(r0, cs)
        g = g_s[rows, :]
        be = b_s[rows, :]
        gh = g.astype(BF16)
        r1 = g - gh.astype(F32)
        gmid = r1.astype(BF16)
        glo = (r1 - gmid.astype(F32)).astype(BF16)
        cum_f = _dot(tri_lo, gh) + _dot(tri_lo, gmid) + _dot(tri_lo, glo)
        cum_b = _dot(tri_up, gh) + _dot(tri_up, gmid) + _dot(tri_up, glo)
        gc = jnp.where(li < GDN_HEADS, cum_f, cum_b)
        gt = jnp.concatenate([gc, gc], axis=0).T
        for p in range(GDN_PAIRS):
            cols = slice(p * LANES, (p + 1) * LANES)
            q128 = q_s[rows, cols]
            k128 = k_s[rows, cols]
            v128 = v_s[rows, cols]
            kb = k128.astype(BF16)
            kstack = stack(kb, jnp.zeros_like(kb))
            kk = _dot_nt(kb, kstack)
            qk = _dot_nt(q128.astype(BF16), kstack)
            for rev in range(2):
                la = rev * GDN_HEADS + 2 * p
                gca, gcb = gc[:, la:la + 1], gc[:, la + 1:la + 2]
                bca, bcb = be[:, la:la + 1], be[:, la + 1:la + 2]
                gcol = jnp.where(lo, gca, gcb)
                grow = jnp.where(lo1, gt[la:la + 1, :], gt[la + 1:la + 2, :])
                bcol = jnp.where(lo, bca, bcb)
                if rev:
                    incl, strict = ri <= tj, ri < tj
                    last = 0
                else:
                    incl, strict = ri >= tj, ri > tj
                    last = cs - 1
                decay = jnp.exp(jnp.where(incl, gcol - grow, -jnp.inf))
                a128 = jnp.where(strict, kk * decay * bcol, 0.0)
                n = -stack(a128, 0.0)
                ea, eb = jnp.exp(gca), jnp.exp(gcb)
                rhs = jnp.concatenate([
                    jnp.concatenate([v128 * bca, v128 * bcb], axis=0),
                    jnp.concatenate([k128 * (bca * ea), k128 * (bcb * eb)], axis=0)], axis=1)
                glast = jnp.where(lo1, gc[last:last + 1, la:la + 1], gc[last:last + 1, la + 1:la + 2])
                u_ref, w_ref, qd_ref, kd_ref, qk_ref, gl_ref = dirs[rev]
                qd_ref[rows, cols] = (q128 * jnp.exp(gcol)).astype(BF16)
                kd_ref[rows, cols] = (k128 * jnp.exp(glast - gcol)).astype(BF16)
                qk_ref[rows, cols] = (qk * decay).astype(BF16)
                gl_ref[c, :, cols] = jnp.broadcast_to(jnp.exp(glast), (8, LANES))
                chains.append(dict(n=n, rhs=rhs, rows=rows, cols=cols, rev=rev))

    def group(i, carry):
        chains = []
        for gi in range(per_iter):
            setup(i * per_iter + gi, chains)
        for ch in chains:
            d0 = jnp.where(same8, ch["n"], 0.0)
            ch["t"] = eye + d0
            ch["pw"] = _dotn(d0, d0, GDN_INVERSE_PASSES)
        for ch in chains:
            both = _dotn(jnp.concatenate([ch["t"], ch["pw"]], axis=0), ch["pw"], GDN_INVERSE_PASSES)
            ch["t"] = ch["t"] + both[0:LANES]
            ch["pw"] = both[LANES:2 * LANES]
        for ch in chains:
            ch["t"] = ch["t"] + _dotn(ch["t"], ch["pw"], GDN_INVERSE_PASSES)
        for off_mask in level_masks:
            for ch in chains:
                ch["x"] = _dotn(jnp.where(off_mask, ch["n"], 0.0), ch["t"], GDN_INVERSE_PASSES)
            for ch in chains:
                ch["t"] = ch["t"] + _dotn(ch["t"], ch["x"], GDN_INVERSE_PASSES)
        for ch in chains:
            sol = _dotn(ch["t"], ch["rhs"], GDN_SOLVE_PASSES)
            u_ref, w_ref = dirs[ch["rev"]][0:2]
            u_ref[ch["rows"], ch["cols"]] = jnp.where(lo, sol[0:cs, 0:LANES], sol[cs:2 * cs, 0:LANES])
            w_ref[ch["rows"], ch["cols"]] = jnp.where(lo, sol[0:cs, LANES:], sol[cs:2 * cs, LANES:]).astype(BF16)
        return carry

    per_iter = math.gcd(GDN_CHUNKS_PER_ITER, tm // cs)
    lax.fori_loop(0, tm // (cs * per_iter), group, 0)


def _gdn_local(nqkv, na, nb, conv_w, alog, dtb):
    b, t, w3 = nqkv.shape
    tm = min(256, t)
    nblk = t // tm
    cpb = tm // GDN_CHUNK
    halo = 8
    hb = tm // halo
    tok = lambda width: pl.BlockSpec((None, tm, width), lambda i, j: (i, j, 0))
    const2 = lambda shape: pl.BlockSpec(shape, lambda i, j: (0, 0))
    in_specs = [
        tok(w3),
        pl.BlockSpec((None, halo, w3), lambda i, j: (i, jnp.maximum(j * hb - 1, 0), 0)),
        pl.BlockSpec((None, halo, w3), lambda i, j: (i, jnp.minimum((j + 1) * hb, t // halo - 1), 0)),
        const2((GDN_CONV, w3)), tok(LANES), tok(LANES), const2((1, LANES)), const2((1, LANES)),
    ]
    per_dir_specs = [tok(GDN_WIDTH)] * 5 + [pl.BlockSpec((None, cpb, 8, GDN_WIDTH), lambda i, j: (i, j, 0, 0))]
    per_dir_shapes = ([jax.ShapeDtypeStruct((b, t, GDN_WIDTH), F32)]
                      + [jax.ShapeDtypeStruct((b, t, GDN_WIDTH), BF16)] * 4
                      + [jax.ShapeDtypeStruct((b, t // GDN_CHUNK, 8, GDN_WIDTH), F32)])
    outs = pl.pallas_call(
        _gdn_local_kernel,
        grid=(b, nblk),
        in_specs=in_specs,
        out_specs=per_dir_specs * 2,
        out_shape=per_dir_shapes * 2,
        scratch_shapes=[
            pltpu.VMEM((tm + 2 * halo, w3), F32),
            pltpu.VMEM((tm, GDN_WIDTH), F32), pltpu.VMEM((tm, GDN_WIDTH), F32), pltpu.VMEM((tm, GDN_WIDTH), F32),
            pltpu.VMEM((tm, LANES), F32), pltpu.VMEM((tm, LANES), F32),
        ],
        compiler_params=_cparams(("parallel", "parallel")),
        name="gdn_local",
    )(nqkv, nqkv, nqkv, conv_w, na, nb, alog, dtb)
    return outs[:6], outs[6:]


def _gdn_scan_kernel(uf_ref, wf_ref, qdf_ref, kdf_ref, qkf_ref, glf_ref,
                     ub_ref, wb_ref, qdb_ref, kdb_ref, qkb_ref, glb_ref,
                     s0f_ref, s0b_ref,
                     of_ref, ob_ref, sff_ref, sfb_ref,
                     stf, stb):
    n = pl.program_id(1)
    cs = GDN_CHUNK
    nbat = uf_ref.shape[0]
    cpb = uf_ref.shape[1] // cs

    @pl.when(n == 0)
    def _():
        stf[...] = s0f_ref[...]
        stb[...] = s0b_ref[...]

    lo = lax.broadcasted_iota(jnp.int32, (cs, LANES), 1) < 64
    r128 = lax.broadcasted_iota(jnp.int32, (LANES, LANES), 0) // 64
    c128 = lax.broadcasted_iota(jnp.int32, (LANES, LANES), 1) // 64
    bd = r128 == c128

    fwd = (uf_ref, wf_ref, qdf_ref, kdf_ref, qkf_ref, glf_ref, of_ref, stf)
    bwd = (ub_ref, wb_ref, qdb_ref, kdb_ref, qkb_ref, glb_ref, ob_ref, stb)

    for cc in range(cpb):
        chains = []
        for c, (u_ref, w_ref, qd_ref, kd_ref, qk_ref, gl_ref, o_ref, st) in ((cc, fwd), (cpb - 1 - cc, bwd)):
            rows = slice(c * cs, (c + 1) * cs)
            for bi in range(nbat):
                for p in range(GDN_PAIRS):
                    cols = slice(p * LANES, (p + 1) * LANES)
                    s = st[bi, p]
                    sb = s.astype(BF16)
                    ws = _dot_nt(w_ref[bi, rows, cols], sb)
                    qs = _dot_nt(qd_ref[bi, rows, cols], sb)
                    chains.append((c, bi, rows, cols, p, s, ws, qs, u_ref, kd_ref, qk_ref, gl_ref, o_ref, st))
        for c, bi, rows, cols, p, s, ws, qs, u_ref, kd_ref, qk_ref, gl_ref, o_ref, st in chains:
            vnew = u_ref[bi, rows, cols] - ws
            vb = vnew.astype(BF16)
            zero = jnp.zeros_like(vb)
            vstack = jnp.concatenate([jnp.where(lo, vb, zero), jnp.where(lo, zero, vb)], axis=0)
            o_ref[bi, rows, cols] = qs + _dot(qk_ref[bi, rows, cols], vstack)
            upd = _dot(vnew.T.astype(BF16), kd_ref[bi, rows, cols])
            st[bi, p] = s * gl_ref[bi, c, 0:1, cols] + jnp.where(bd, upd, 0.0)

    @pl.when(n == pl.num_programs(1) - 1)
    def _():
        sff_ref[...] = stf[...]
        sfb_ref[...] = stb[...]


def _gdn_scan(fwd, bwd, s0f, s0b):
    b, t, _ = fwd[0].shape
    tm = min(256, t)
    nblk = t // tm
    cpb = tm // GDN_CHUNK
    bb = math.gcd(GDN_SCAN_BATCH, b)
    f_tok = pl.BlockSpec((bb, tm, GDN_WIDTH), lambda i, j: (i, j, 0))
    b_tok = pl.BlockSpec((bb, tm, GDN_WIDTH), lambda i, j: (i, nblk - 1 - j, 0))
    f_gl = pl.BlockSpec((bb, cpb, 8, GDN_WIDTH), lambda i, j: (i, j, 0, 0))
    b_gl = pl.BlockSpec((bb, cpb, 8, GDN_WIDTH), lambda i, j: (i, nblk - 1 - j, 0, 0))
    st_spec = pl.BlockSpec((bb, GDN_PAIRS, LANES, LANES), lambda i, j: (i, 0, 0, 0))
    st_shape = jax.ShapeDtypeStruct((b, GDN_PAIRS, LANES, LANES), F32)
    o_shape = jax.ShapeDtypeStruct((b, t, GDN_WIDTH), F32)
    return pl.pallas_call(
        _gdn_scan_kernel,
        grid=(b // bb, nblk),
        in_specs=[f_tok] * 5 + [f_gl] + [b_tok] * 5 + [b_gl] + [st_spec, st_spec],
        out_specs=[f_tok, b_tok, st_spec, st_spec],
        out_shape=[o_shape, o_shape, st_shape, st_shape],
        scratch_shapes=[pltpu.VMEM((bb, GDN_PAIRS, LANES, LANES), F32)] * 2,
        compiler_params=_cparams(("parallel", "arbitrary")),
        name="gdn_scan",
    )(*fwd, *bwd, s0f, s0b)


def _out_proj_kernel(h_ref, d_ref, gq_ref, of_ref, ob_ref, z_ref, ng_ref, gate_ref, w_ref, o_ref):
    gm = _group_matrix(1.0 / GDN_HEAD_DIM)
    pieces = [d_ref[...], gq_ref[...]]
    for blk in range(GDN_PAIRS):
        cols = slice(blk * LANES, (blk + 1) * LANES)
        o = of_ref[:, cols] + ob_ref[:, cols]
        ms = _group_sum(o * o, gm)
        r = o * lax.rsqrt(ms + NORM_EPS) * ng_ref[...] * jax.nn.silu(z_ref[:, cols])
        pieces.append(r.astype(BF16))
    y = _dot(jnp.concatenate(pieces, axis=1), w_ref[...])
    o_ref[...] = h_ref[...] + gate_ref[...] * y


def _out_proj(h, d, gq, of, ob, z, ng, gate, w):
    b, t, dm = h.shape
    tm = min(512, t)
    nb = gate.shape[0]
    mod_map = (lambda i, j: (i, 0, 0)) if nb == b else (lambda i, j: (0, 0, 0))
    tok = lambda width: pl.BlockSpec((None, tm, width), lambda i, j: (i, j, 0))
    return pl.pallas_call(
        _out_proj_kernel,
        grid=(b, t // tm),
        in_specs=[tok(dm), tok(DIFF_WIDTH), tok(GQA_WIDTH), tok(GDN_WIDTH), tok(GDN_WIDTH), tok(GDN_WIDTH),
                  pl.BlockSpec((1, LANES), lambda i, j: (0, 0)),
                  pl.BlockSpec((None, 1, dm), mod_map),
                  pl.BlockSpec(w.shape, lambda i, j: (0, 0))],
        out_specs=tok(dm),
        out_shape=jax.ShapeDtypeStruct((b, t, dm), F32),
        compiler_params=_cparams(("parallel", "parallel")),
        name="out_proj",
    )(h, d, gq, of, ob, z, ng, gate, w)


def _ffn_kernel(h_ref, g_ref, sh_ref, sc_ref, gate_ref, wgu_ref, wd_ref, fg_ref, o_ref, *, final_norm):
    x = h_ref[...]
    y = x * lax.rsqrt(jnp.mean(x * x, axis=-1, keepdims=True) + NORM_EPS) * g_ref[...]
    a = (y * (1.0 + sc_ref[...]) + sh_ref[...]).astype(BF16)
    hidden = wd_ref.shape[0]
    starts = list(range(0, hidden, FFN_CHUNK))

    def gate_up(lo):
        width = min(FFN_CHUNK, hidden - lo)
        return _dot(a, wgu_ref[:, lo:lo + width]), _dot(a, wgu_ref[:, hidden + lo:hidden + lo + width]), width

    pending = [gate_up(lo) for lo in starts[:FFN_SKEW]]
    acc = None
    for n, lo in enumerate(starts):
        gate, up, width = pending.pop(0)
        if n + FFN_SKEW < len(starts):
            pending.append(gate_up(starts[n + FFN_SKEW]))
        act = (jax.nn.silu(gate) * up).astype(BF16)
        part = _dot(act, wd_ref[lo:lo + width, :])
        acc = part if acc is None else acc + part
    out = x + gate_ref[...] * acc
    if final_norm:
        out = out * lax.rsqrt(jnp.mean(out * out, axis=-1, keepdims=True) + NORM_EPS) * fg_ref[...]
    o_ref[...] = out


def _ffn(h, g, sh, sc, gate, w_gu, w_down, fg, final_norm):
    b, t, dm = h.shape
    tm = min(512, t)
    nb = sh.shape[0]
    mod_map = (lambda i, k: (i, 0, 0)) if nb == b else (lambda i, k: (0, 0, 0))
    tok = pl.BlockSpec((None, tm, dm), lambda i, k: (i, k, 0))
    vec = pl.BlockSpec((1, dm), lambda i, k: (0, 0))
    mod = pl.BlockSpec((None, 1, dm), mod_map)
    resident = lambda w: pl.BlockSpec(w.shape, lambda i, k: (0, 0), pipeline_mode=pl.Buffered(1))
    return pl.pallas_call(
        functools.partial(_ffn_kernel, final_norm=final_norm),
        grid=(b, t // tm),
        in_specs=[tok, vec, mod, mod, mod, resident(w_gu), resident(w_down), vec],
        out_specs=tok,
        out_shape=jax.ShapeDtypeStruct((b, t, dm), F32),
        compiler_params=_cparams(("parallel", "parallel")),
        name="ffn",
    )(h, g, sh, sc, gate, w_gu, w_down, fg)


def _rope_tables(t, rot_dim):
    nf = rot_dim // 4
    pos = jnp.arange(t)
    row = (pos // GRID_W).astype(F32)
    col = (pos % GRID_W).astype(F32)
    inv_freq = ROPE_THETA ** (-jnp.arange(nf, dtype=F32) / nf)
    ar, ac = row[:, None] * inv_freq, col[:, None] * inv_freq
    z = jnp.zeros_like(ar)
    cos = jnp.concatenate([jnp.cos(ar), jnp.cos(ar), jnp.cos(ac), jnp.cos(ac)], axis=1)
    sa = jnp.concatenate([-jnp.sin(ar), z, -jnp.sin(ac), z], axis=1)
    sb = jnp.concatenate([z, jnp.sin(ar), z, jnp.sin(ac)], axis=1)
    rep = LANES // rot_dim
    return tuple(jnp.tile(a, (1, rep)) for a in (cos, sa, sb))


def _pad_lanes(v):
    return jnp.pad(v.reshape(1, -1), ((0, 0), (0, LANES - v.size)))


def _relayout_w_in(w):
    o = 0
    pieces = []
    for width in (DIFF_WIDTH, DIFF_WIDTH, DIFF_WIDTH):
        pieces.append(w[:, o:o + width]); o += width
    gq = w[:, o:o + GQA_WIDTH].reshape(-1, GQA_Q_HEADS, GQA_HEAD_DIM); o += GQA_WIDTH
    pieces.append(jnp.take(gq, jnp.array(GQA_Q_ORDER), axis=1).reshape(-1, GQA_WIDTH))
    for width in (GQA_KV_WIDTH, GQA_KV_WIDTH, 3 * GDN_WIDTH, GDN_WIDTH):
        pieces.append(w[:, o:o + width]); o += width
    for width in (2 * GDN_HEADS, 2 * GDN_HEADS):
        pieces.append(jnp.pad(w[:, o:o + width], ((0, 0), (0, LANES - width)))); o += width
    return jnp.concatenate(pieces, axis=1).astype(BF16)


def _relayout_w_out(w):
    gq = w[DIFF_WIDTH:DIFF_WIDTH + GQA_WIDTH].reshape(GQA_Q_HEADS, GQA_HEAD_DIM, -1)
    gq = jnp.take(gq, jnp.array(GQA_Q_ORDER), axis=0).reshape(GQA_WIDTH, -1)
    return jnp.concatenate([w[:DIFF_WIDTH], gq, w[DIFF_WIDTH + GQA_WIDTH:]], axis=0).astype(BF16)


def kernel(x, c, ctx, c_ctx, norm1_g, ada_w, ada_b, w_in, diff_lambda, diff_norm_g, q_norm_g, k_norm_g,
           gdn_conv_w, gdn_a_log, gdn_dt_bias, gdn_norm_g, w_out, norm2_g, ffn_w_gu, ffn_w_down, final_norm_g):
    b, t, d = x.shape
    depth = w_in.shape[0]
    rope = _rope_tables(t, DIFF_QK_DIM) + _rope_tables(t, GQA_HEAD_DIM)

    cond = jnp.concatenate([c, c_ctx[None, :], jnp.zeros((16 - b - 1, d), F32)], axis=0)
    mod = _ada(cond, ada_w, ada_b).reshape(depth, 16, 6, d)

    tile2 = lambda v: jnp.tile(v.reshape(1, -1), (1, LANES // v.size))
    zeros_state = jnp.zeros((b, GDN_PAIRS, LANES, LANES), F32)

    h, hc = x, ctx
    for layer in range(depth):
        need_ctx = layer < depth - 1
        lambda_init = 0.8 - 0.6 * math.exp(-0.3 * layer)
        mod_l = [mod[layer, :b, k][:, None, :] for k in range(6)]
        mod_c = [mod[layer, b:b + 1, k][:, None, :] for k in range(6)]
        w_in_l = _relayout_w_in(w_in[layer])
        w_out_l = _relayout_w_out(w_out[layer])
        w_gu_l = ffn_w_gu[layer].astype(BF16)
        w_down_l = ffn_w_down[layer].astype(BF16)
        g1 = norm1_g[layer].reshape(1, d)
        g2 = norm2_g[layer].reshape(1, d)
        qg, kg = tile2(q_norm_g[layer]), tile2(k_norm_g[layer])
        dng, nng = tile2(diff_norm_g[layer]), tile2(gdn_norm_g[layer])
        alog, dtb = _pad_lanes(gdn_a_log[layer]), _pad_lanes(gdn_dt_bias[layer])
        lam = diff_lambda[layer]
        conv_w = gdn_conv_w[layer]

        pl_ = _in_proj(h, g1, mod_l[0], mod_l[1], w_in_l, qg, kg, rope)
        pc_ = _in_proj(hc, g1, mod_c[0], mod_c[1], w_in_l, qg, kg, None)
        dq_l, dk_l, dv_l, gq_l, gk_l, gv_l, nqkv_l, nz_l, na_l, nb_l = pl_
        dq_c, dk_c, dv_c, gq_c, gk_c, gv_c, nqkv_c, nz_c, na_c, nb_c = pc_

        d_l = _attention("diff", dq_l, dk_c, dv_c, dk_l, dv_l, (lam, dng), lambda_init)
        a_l = _attention("gqa", gq_l, gk_c, gv_c, gk_l, gv_l, ())

        fwd_c, bwd_c = _gdn_local(nqkv_c, na_c, nb_c, conv_w, alog, dtb)
        fwd_l, bwd_l = _gdn_local(nqkv_l, na_l, nb_l, conv_w, alog, dtb)
        ocf, ocb, scf, scb = _gdn_scan(fwd_c, bwd_c, zeros_state, zeros_state)
        olf, olb, _, _ = _gdn_scan(fwd_l, bwd_l, scf, scb)

        h = _out_proj(h, d_l, a_l, olf, olb, nz_l, nng, mod_l[2], w_out_l)
        last = layer == depth - 1
        h = _ffn(h, g2, mod_l[3], mod_l[4], mod_l[5], w_gu_l, w_down_l, final_norm_g.reshape(1, d), last)
        if need_ctx:
            d_c = _attention("diff", dq_c, dk_c, dv_c, None, None, (lam, dng), lambda_init)
            a_c = _attention("gqa", gq_c, gk_c, gv_c, None, None, ())
            hc = _out_proj(hc, d_c, a_c, ocf, ocb, nz_c, nng, mod_c[2], w_out_l)
            hc = _ffn(hc, g2, mod_c[3], mod_c[4], mod_c[5], w_gu_l, w_down_l, final_norm_g.reshape(1, d), False)
    return h
```

```python
import functools
import math

import jax
import jax.numpy as jnp
from jax import lax
from jax.experimental import pallas as pl
from jax.experimental.pallas import tpu as pltpu

F32 = jnp.float32
BF16 = jnp.bfloat16

LANES = 128
SUBLANES = 8
D_MODEL = 1024
GRID_W = 64
ROPE_THETA = 10000.0
NORM_EPS = 1e-6
L2_EPS = 1e-6
LOG2E = 1.4426950408889634

DIFF_HEADS = 4
DIFF_QK_DIM = 32
DIFF_V_DIM = 64
DIFF_WIDTH = DIFF_HEADS * DIFF_V_DIM
GQA_Q_HEADS = 6
GQA_KV_HEADS = 2
GQA_HEAD_DIM = 64
GQA_WIDTH = GQA_Q_HEADS * GQA_HEAD_DIM
GQA_KV_WIDTH = GQA_KV_HEADS * GQA_HEAD_DIM
GDN_HEADS = 6
GDN_HEAD_DIM = 64
GDN_WIDTH = GDN_HEADS * GDN_HEAD_DIM
GDN_CONV = 5
GDN_CHUNK = 64
GDN_PAIRS = GDN_HEADS // 2
FFN_HIDDEN = 2816

OFF_DQ = 0
OFF_DK = OFF_DQ + DIFF_WIDTH
OFF_DV = OFF_DK + DIFF_WIDTH
OFF_GQ = OFF_DV + DIFF_WIDTH
OFF_GK = OFF_GQ + GQA_WIDTH
OFF_GV = OFF_GK + GQA_KV_WIDTH
OFF_NQKV = OFF_GV + GQA_KV_WIDTH
OFF_NZ = OFF_NQKV + 3 * GDN_WIDTH
OFF_NA = OFF_NZ + GDN_WIDTH
OFF_NB = OFF_NA + LANES
IN_PAD = OFF_NB + LANES

GQA_Q_ORDER = (0, 3, 1, 4, 2, 5)
HEAD_LANES = 64
HEAD_V = HEAD_LANES
ONES_ROWS = 16
V_ROWS = HEAD_V + ONES_ROWS

GDN_INVERSE_PASSES = 1
GDN_SOLVE_PASSES = 1
FFN_CHUNK = 512
FFN_SKEW = 1
GDN_SCAN_BATCH = 4
GDN_CHUNKS_PER_ITER = 4

ATTN_SKEW = 3
ATTN_CHUNKS_PER_ITER = 16
ATTN_KEY_CHUNK = 512
ATTN_QUERY_TILE = 256
IN_PROJ_ROWS = 512
MIX_FFN_ROWS = 512
GDN_BLOCK_ROWS = 256

VMEM_LIMIT = 56 * 1024 * 1024


def _cparams(sem):
    return pltpu.CompilerParams(dimension_semantics=sem, vmem_limit_bytes=VMEM_LIMIT)


def _dot(a, b):
    return jnp.dot(a, b, preferred_element_type=F32)


def _dot_nt(a, b):
    return lax.dot_general(a, b, (((1,), (1,)), ((), ())), preferred_element_type=F32)


def _split_bf16(x):
    hi = x.astype(BF16)
    lo = (x - hi.astype(F32)).astype(BF16)
    return hi, lo


def _dot3(a, b):
    ah, al = _split_bf16(a)
    bh, bl = _split_bf16(b)
    return _dot(ah, bh) + _dot(ah, bl) + _dot(al, bh)


def _dotn(a, b, passes):
    if passes == 1:
        return _dot(a.astype(BF16), b.astype(BF16))
    assert passes == 3
    return _dot3(a, b)


def _group_matrix(scale):
    r = lax.broadcasted_iota(jnp.int32, (LANES, LANES), 0) // HEAD_LANES
    c = lax.broadcasted_iota(jnp.int32, (LANES, LANES), 1) // HEAD_LANES
    return jnp.where(r == c, scale, 0.0).astype(BF16)


def _group_sum(x, gm):
    hi, lo = _split_bf16(x)
    return _dot(hi, gm) + _dot(lo, gm)


def _rope(x, c, sa, sb, half):
    return x * c + pltpu.roll(x, LANES - half, 1) * sa + pltpu.roll(x, half, 1) * sb


def _ada_kernel(c_ref, w_ref, b_ref, o_ref):
    s = jax.nn.silu(c_ref[...]).astype(BF16)
    o_ref[...] = _dot(s, w_ref[...].astype(BF16)) + b_ref[...]


def _ada(cond, ada_w, ada_b):
    depth, d, n = ada_w.shape
    rows = cond.shape[0]
    tn = 1536
    return pl.pallas_call(
        _ada_kernel,
        grid=(depth, n // tn),
        in_specs=[
            pl.BlockSpec((rows, d), lambda l, j: (0, 0)),
            pl.BlockSpec((None, d, tn), lambda l, j: (l, 0, j)),
            pl.BlockSpec((None, 1, tn), lambda l, j: (l, 0, j)),
        ],
        out_specs=pl.BlockSpec((None, rows, tn), lambda l, j: (l, 0, j)),
        out_shape=jax.ShapeDtypeStruct((depth, rows, n), F32),
        compiler_params=_cparams(("parallel", "parallel")),
        name="ada_mod",
    )(cond, ada_w, ada_b.reshape(depth, 1, n))


def _store_values(ref, first_head, v):
    vt = v.T.astype(BF16)
    n_chunks, _, tk = ref.shape
    ones = jnp.ones((ONES_ROWS, tk), BF16)
    for c in range(n_chunks):
        for s in range(2):
            r0 = (first_head + s) * V_ROWS
            ref[c, r0:r0 + HEAD_V, :] = vt[s * HEAD_V:(s + 1) * HEAD_V, c * tk:(c + 1) * tk]
            ref[c, r0 + HEAD_V:r0 + V_ROWS, :] = ones


def _in_proj_kernel(*refs, use_rope):
    if use_rope:
        (x_ref, g_ref, sh_ref, sc_ref, w_ref, qg_ref, kg_ref,
         cd_ref, sad_ref, sbd_ref, cg_ref, sag_ref, sbg_ref,
         dq_ref, dk_ref, dv_ref, gq_ref, gk_ref, gv_ref, nqkv_ref, nz_ref, na_ref, nb_ref) = refs
    else:
        (x_ref, g_ref, sh_ref, sc_ref, w_ref, qg_ref, kg_ref,
         dq_ref, dk_ref, dv_ref, gq_ref, gk_ref, gv_ref, nqkv_ref, nz_ref, na_ref, nb_ref) = refs
    x = x_ref[...]
    y = x * lax.rsqrt(jnp.mean(x * x, axis=-1, keepdims=True) + NORM_EPS) * g_ref[...]
    a = (y * (1.0 + sc_ref[...]) + sh_ref[...]).astype(BF16)

    bounds = (OFF_DQ, OFF_DK, OFF_DV, OFF_GQ, OFF_GK, OFF_NQKV, OFF_NZ, OFF_NA, IN_PAD)
    groups = {}

    def proj(lo, width):
        g0 = max(b for b in bounds if b <= lo)
        g1 = min(b for b in bounds if b > lo)
        assert lo + width <= g1
        if g0 not in groups:
            groups[g0] = _dot(a, w_ref[:, g0:g1])
        return groups[g0][:, lo - g0:lo - g0 + width]

    gm = _group_matrix(1.0 / GQA_HEAD_DIM)

    for off, out, scale in ((OFF_DQ, dq_ref, DIFF_QK_DIM ** -0.5 * LOG2E), (OFF_DK, dk_ref, 1.0)):
        for blk in range(DIFF_WIDTH // LANES):
            p = proj(off + blk * LANES, LANES)
            if use_rope:
                p = _rope(p, cd_ref[...], sad_ref[...], sbd_ref[...], DIFF_QK_DIM // 4)
            out[:, blk * LANES:(blk + 1) * LANES] = (p * scale).astype(BF16)
    for blk in range(DIFF_WIDTH // LANES):
        _store_values(dv_ref, 2 * blk, proj(OFF_DV + blk * LANES, LANES))

    def qk_prep(p, gain, scale):
        ms = _group_sum(p * p, gm)
        p = p * lax.rsqrt(ms + NORM_EPS) * gain
        if use_rope:
            p = _rope(p, cg_ref[...], sag_ref[...], sbg_ref[...], GQA_HEAD_DIM // 4)
        return (p * scale).astype(BF16)

    for blk in range(GQA_WIDTH // LANES):
        p = proj(OFF_GQ + blk * LANES, LANES)
        gq_ref[:, blk * LANES:(blk + 1) * LANES] = qk_prep(p, qg_ref[...], GQA_HEAD_DIM ** -0.5 * LOG2E)
    gk_ref[...] = qk_prep(proj(OFF_GK, LANES), kg_ref[...], 1.0)
    _store_values(gv_ref, 0, proj(OFF_GV, LANES))

    for blk in range(3 * GDN_WIDTH // 384):
        nqkv_ref[:, blk * 384:(blk + 1) * 384] = proj(OFF_NQKV + blk * 384, 384)
    nz_ref[...] = proj(OFF_NZ, GDN_WIDTH)
    na_ref[...] = proj(OFF_NA, LANES)
    nb_ref[...] = proj(OFF_NB, LANES)


def _in_proj(x, g, sh, sc, w, qg, kg, rope):
    b, t, d = x.shape
    tm = min(IN_PROJ_ROWS, t)
    nb = sh.shape[0]
    mod_map = (lambda i, j: (i, 0, 0)) if nb == b else (lambda i, j: (0, 0, 0))
    tok = lambda width: pl.BlockSpec((None, tm, width), lambda i, j: (i, j, 0))
    const2 = lambda shape: pl.BlockSpec(shape, lambda i, j: (0, 0))
    in_specs = [
        tok(d), const2((1, d)),
        pl.BlockSpec((None, 1, d), mod_map), pl.BlockSpec((None, 1, d), mod_map),
        const2((d, IN_PAD)), const2((1, LANES)), const2((1, LANES)),
    ]
    args = [x, g, sh, sc, w, qg, kg]
    if rope is not None:
        in_specs += [pl.BlockSpec((tm, LANES), lambda i, j: (j, 0))] * 6
        args += list(rope)
    widths = (DIFF_WIDTH, DIFF_WIDTH, DIFF_HEADS * V_ROWS, GQA_WIDTH, GQA_KV_WIDTH, GQA_KV_HEADS * V_ROWS,
              3 * GDN_WIDTH, GDN_WIDTH, LANES, LANES)
    dtypes = (BF16,) * 6 + (F32,) * 4
    transposed = (2, 5)
    tk = min(ATTN_KEY_CHUNK, tm)
    out_specs = [pl.BlockSpec((None, tm // tk, wd, tk), lambda i, j: (i, j, 0, 0)) if k in transposed else tok(wd)
                 for k, wd in enumerate(widths)]
    out_shape = [jax.ShapeDtypeStruct((b, t // tk, wd, tk) if k in transposed else (b, t, wd), dt)
                 for k, (wd, dt) in enumerate(zip(widths, dtypes))]
    return pl.pallas_call(
        functools.partial(_in_proj_kernel, use_rope=rope is not None),
        grid=(b, t // tm),
        in_specs=in_specs,
        out_specs=out_specs,
        out_shape=out_shape,
        compiler_params=_cparams(("parallel", "parallel")),
        name="in_proj",
    )(*args)


def _flash_all(cols, vrows, qm_s, m_s, acc_s, kc_ref, vc_ref, kl_ref, vl_ref, n_lat):
    nh = len(cols)
    m_s[...] = jnp.full(m_s.shape, -jnp.inf, F32)
    acc_s[...] = jnp.zeros(acc_s.shape, F32)

    def pipeline(chunk_list):
        items = [(ck, h) for ck in chunk_list for h in range(nh)]

        def scores(item):
            (k_ref, _, tk, j), h = item
            if tk is None:
                rows = slice(None)
            elif isinstance(j, int):
                rows = slice(j * tk, (j + 1) * tk)
            else:
                rows = pl.ds(pl.multiple_of(j * tk, tk), tk)
            return _dot_nt(k_ref[rows, cols[h]:cols[h] + LANES], qm_s[h])

        pending = [scores(it) for it in items[:ATTN_SKEW]]
        for n, ((_, v_ref, _, j), h) in enumerate(items):
            s = pending.pop(0)
            if n + ATTN_SKEW < len(items):
                pending.append(scores(items[n + ATTN_SKEW]))
            m_old = m_s[h]
            m_new = jnp.maximum(m_old, jnp.max(s, axis=0, keepdims=True))
            alpha = jnp.exp2(m_old - m_new)
            p = jnp.exp2(s - m_new)
            acc_s[h] = alpha * acc_s[h] + _dot(v_ref[j, vrows[h]:vrows[h] + V_ROWS, :], p.astype(BF16))
            m_s[h] = m_new

    ctx_chunk = (kc_ref, vc_ref, None, 0)
    if not n_lat:
        pipeline([ctx_chunk])
        return
    tk = vl_ref.shape[-1]
    per_iter = math.gcd(ATTN_CHUNKS_PER_ITER, n_lat)
    if per_iter == n_lat:
        pipeline([ctx_chunk] + [(kl_ref, vl_ref, tk, j) for j in range(n_lat)])
        return
    pipeline([ctx_chunk])

    def body(i, carry):
        pipeline([(kl_ref, vl_ref, tk, i * per_iter + jj) for jj in range(per_iter)])
        return carry

    lax.fori_loop(0, n_lat // per_iter, body, 0)


def _normalised(acc_s, h):
    acc = acc_s[h]
    return acc[0:HEAD_V] / acc[HEAD_V:HEAD_V + 1]


def _pair_out(oa, ob):
    return jnp.concatenate([oa, ob], axis=0).T


def _gqa_kernel(*refs, n_lat):
    if n_lat:
        q_ref, kc_ref, vc_ref, kl_ref, vl_ref, o_ref, qm_s, m_s, acc_s = refs
    else:
        q_ref, kc_ref, vc_ref, o_ref, qm_s, m_s, acc_s = refs
        kl_ref = vl_ref = None
    tq = q_ref.shape[0]
    nblk = GQA_WIDTH // LANES
    lo = lax.broadcasted_iota(jnp.int32, (tq, LANES), 1) < GQA_HEAD_DIM
    for blk in range(nblk):
        qb = q_ref[:, blk * LANES:(blk + 1) * LANES]
        zero = jnp.zeros_like(qb)
        qm_s[2 * blk] = jnp.where(lo, qb, zero)
        qm_s[2 * blk + 1] = jnp.where(lo, zero, qb)
    _flash_all((0,) * (2 * nblk), (0, V_ROWS) * nblk, qm_s, m_s, acc_s, kc_ref, vc_ref, kl_ref, vl_ref, n_lat)
    for blk in range(nblk):
        o = _pair_out(_normalised(acc_s, 2 * blk), _normalised(acc_s, 2 * blk + 1))
        o_ref[:, blk * LANES:(blk + 1) * LANES] = o.astype(BF16)


def _diff_kernel(*refs, n_lat, lambda_init):
    if n_lat:
        q_ref, kc_ref, vc_ref, kl_ref, vl_ref, lam_ref, ng_ref, o_ref, qm_s, m_s, acc_s = refs
    else:
        q_ref, kc_ref, vc_ref, lam_ref, ng_ref, o_ref, qm_s, m_s, acc_s = refs
        kl_ref = vl_ref = None
    tq = q_ref.shape[0]
    nblk = DIFF_WIDTH // LANES
    lf = lam_ref[...]
    lam = (jnp.exp(jnp.sum(lf[0:1] * lf[1:2], axis=-1, keepdims=True))
           - jnp.exp(jnp.sum(lf[2:3] * lf[3:4], axis=-1, keepdims=True)) + lambda_init)
    lane = lax.broadcasted_iota(jnp.int32, (tq, LANES), 1)
    gm = _group_matrix(1.0 / DIFF_V_DIM)
    for blk in range(nblk):
        qb = q_ref[:, blk * LANES:(blk + 1) * LANES]
        zero = jnp.zeros_like(qb)
        for sc in range(4):
            qm_s[4 * blk + sc] = jnp.where((lane // DIFF_QK_DIM) == sc, qb, zero)
    cols = tuple(blk * LANES for blk in range(nblk) for _ in range(4))
    vrows = tuple((2 * blk + s) * V_ROWS for blk in range(nblk) for s in range(2) for _ in range(2))
    _flash_all(cols, vrows, qm_s, m_s, acc_s, kc_ref, vc_ref, kl_ref, vl_ref, n_lat)
    for blk in range(nblk):
        halves = []
        for s in range(2):
            i0, i1 = 4 * blk + 2 * s, 4 * blk + 2 * s + 1
            halves.append(_normalised(acc_s, i0) - lam * _normalised(acc_s, i1))
        o = _pair_out(halves[0], halves[1])
        ms = _group_sum(o * o, gm)
        o = o * lax.rsqrt(ms + NORM_EPS) * ng_ref[...] * (1.0 - lambda_init)
        o_ref[:, blk * LANES:(blk + 1) * LANES] = o.astype(BF16)


def _attention(kind, q, kc, vc, kl, vl, extra, lambda_init=None):
    b, t, w = q.shape
    tq = min(ATTN_QUERY_TILE, t)
    full = lambda a: pl.BlockSpec((None,) + a.shape[1:], lambda i, j: (i,) + (0,) * (a.ndim - 1))
    in_specs = [pl.BlockSpec((None, tq, w), lambda i, j: (i, j, 0)), full(kc), full(vc)]
    args = [q, kc, vc]
    n_lat = 0
    if kl is not None:
        n_lat = vl.shape[1]
        in_specs += [full(kl), full(vl)]
        args += [kl, vl]
    for e in extra:
        in_specs.append(pl.BlockSpec(e.shape, lambda i, j: (0, 0)))
        args.append(e)
    if kind == "gqa":
        body = functools.partial(_gqa_kernel, n_lat=n_lat)
        nh = GQA_Q_HEADS
    else:
        body = functools.partial(_diff_kernel, n_lat=n_lat, lambda_init=lambda_init)
        nh = 2 * DIFF_HEADS
    return pl.pallas_call(
        body,
        grid=(b, t // tq),
        in_specs=in_specs,
        out_specs=pl.BlockSpec((None, tq, w), lambda i, j: (i, j, 0)),
        out_shape=jax.ShapeDtypeStruct((b, t, w), BF16),
        scratch_shapes=[pltpu.VMEM((nh, tq, LANES), BF16), pltpu.VMEM((nh, 1, tq), F32),
                        pltpu.VMEM((nh, V_ROWS, tq), F32)],
        compiler_params=_cparams(("parallel", "parallel")),
        name=kind + "_attn",
    )(*args)


def _gdn_local_kernel(x_ref, xp_ref, xn_ref, cw_ref, na_ref, nb_ref, alog_ref, dtb_ref,
                      uf_ref, wf_ref, qdf_ref, kdf_ref, qkf_ref, glf_ref,
                      ub_ref, wb_ref, qdb_ref, kdb_ref, qkb_ref, glb_ref,
                      xbuf, q_s, k_s, v_s, g_s, b_s):
    tm = x_ref.shape[0]
    cs = GDN_CHUNK
    i = pl.program_id(1)
    nblk = pl.num_programs(1)
    halo = xp_ref.shape[0]
    xbuf[halo:halo + tm, :] = x_ref[...]
    xbuf[0:halo, :] = jnp.where(i > 0, xp_ref[...], 0.0)
    xbuf[halo + tm:2 * halo + tm, :] = jnp.where(i < nblk - 1, xn_ref[...], 0.0)
    gm = _group_matrix(1.0)
    for part, dst in enumerate((q_s, k_s, v_s)):
        cols = slice(part * GDN_WIDTH, (part + 1) * GDN_WIDTH)
        acc = None
        xall = xbuf[:, cols]
        for j in range(GDN_CONV):
            d = j - GDN_CONV // 2
            xs = xall if d == 0 else pltpu.roll(xall, (-d) % (tm + 2 * halo), 0)
            term = xs[halo:halo + tm] * cw_ref[j:j + 1, cols]
            acc = term if acc is None else acc + term
        y = jax.nn.silu(acc)
        if part < 2:
            scale = GDN_HEAD_DIM ** -0.5 if part == 0 else 1.0
            for blk in range(GDN_PAIRS):
                yb = y[:, blk * LANES:(blk + 1) * LANES]
                ss = _group_sum(yb * yb, gm)
                dst[:, blk * LANES:(blk + 1) * LANES] = yb * lax.rsqrt(ss + L2_EPS) * scale
        else:
            dst[...] = y
    xa = na_ref[...] + dtb_ref[...]
    softplus = jnp.maximum(xa, 0.0) + jnp.log(1.0 + jnp.exp(-jnp.abs(xa)))
    g_s[...] = -jnp.exp(alog_ref[...]) * softplus
    b_s[...] = jax.nn.sigmoid(nb_ref[...])

    ri = lax.broadcasted_iota(jnp.int32, (cs, LANES), 0)
    li = lax.broadcasted_iota(jnp.int32, (cs, LANES), 1)
    lo = li < HEAD_LANES
    tj = li % HEAD_LANES
    r64 = lax.broadcasted_iota(jnp.int32, (cs, cs), 0)
    c64 = lax.broadcasted_iota(jnp.int32, (cs, cs), 1)
    tri_lo = jnp.where(r64 >= c64, 1.0, 0.0).astype(BF16)
    tri_up = jnp.where(r64 <= c64, 1.0, 0.0).astype(BF16)
    r128 = lax.broadcasted_iota(jnp.int32, (LANES, LANES), 0)
    c128 = lax.broadcasted_iota(jnp.int32, (LANES, LANES), 1)
    eye = jnp.where(r128 == c128, 1.0, 0.0)
    same8 = (r128 // 8) == (c128 // 8)
    level_masks = [((r128 // (2 * m)) == (c128 // (2 * m))) & ((r128 // m) != (c128 // m)) for m in (8, 16, 32)]
    lo1 = lax.broadcasted_iota(jnp.int32, (1, LANES), 1) < HEAD_LANES

    def stack(x, zero):
        return jnp.concatenate([jnp.where(lo, x, zero), jnp.where(lo, zero, x)], axis=0)

    dirs = ((uf_ref, wf_ref, qdf_ref, kdf_ref, qkf_ref, glf_ref),
            (ub_ref, wb_ref, qdb_ref, kdb_ref, qkb_ref, glb_ref))

    def setup(c, chains):
        r0 = pl.multiple_of(c * cs, cs)
        rows = pl.ds(r0, cs)
        g = g_s[rows, :]
        be = b_s[rows, :]
        gh = g.astype(BF16)
        r1 = g - gh.astype(F32)
        gmid = r1.astype(BF16)
        glo = (r1 - gmid.astype(F32)).astype(BF16)
        cum_f = _dot(tri_lo, gh) + _dot(tri_lo, gmid) + _dot(tri_lo, glo)
        cum_b = _dot(tri_up, gh) + _dot(tri_up, gmid) + _dot(tri_up, glo)
        gc = jnp.where(li < GDN_HEADS, cum_f, cum_b)
        gt = jnp.concatenate([gc, gc], axis=0).T
        for p in range(GDN_PAIRS):
            cols = slice(p * LANES, (p + 1) * LANES)
            q128 = q_s[rows, cols]
            k128 = k_s[rows, cols]
            v128 = v_s[rows, cols]
            kb = k128.astype(BF16)
            kstack = stack(kb, jnp.zeros_like(kb))
            kk = _dot_nt(kb, kstack)
            qk = _dot_nt(q128.astype(BF16), kstack)
            for rev in range(2):
                la = rev * GDN_HEADS + 2 * p
                gca, gcb = gc[:, la:la + 1], gc[:, la + 1:la + 2]
                bca, bcb = be[:, la:la + 1], be[:, la + 1:la + 2]
                gcol = jnp.where(lo, gca, gcb)
                grow = jnp.where(lo1, gt[la:la + 1, :], gt[la + 1:la + 2, :])
                bcol = jnp.where(lo, bca, bcb)
                if rev:
                    incl, strict = ri <= tj, ri < tj
                    last = 0
                else:
                    incl, strict = ri >= tj, ri > tj
                    last = cs - 1
                decay = jnp.exp(jnp.where(incl, gcol - grow, -jnp.inf))
                a128 = jnp.where(strict, kk * decay * bcol, 0.0)
                n = -stack(a128, 0.0)
                ea, eb = jnp.exp(gca), jnp.exp(gcb)
                rhs = jnp.concatenate([
                    jnp.concatenate([v128 * bca, v128 * bcb], axis=0),
                    jnp.concatenate([k128 * (bca * ea), k128 * (bcb * eb)], axis=0)], axis=1)
                glast = jnp.where(lo1, gc[last:last + 1, la:la + 1], gc[last:last + 1, la + 1:la + 2])
                u_ref, w_ref, qd_ref, kd_ref, qk_ref, gl_ref = dirs[rev]
                qd_ref[rows, cols] = (q128 * jnp.exp(gcol)).astype(BF16)
                kd_ref[rows, cols] = (k128 * jnp.exp(glast - gcol)).astype(BF16)
                qk_ref[rows, cols] = (qk * decay).astype(BF16)
                gl_ref[c, :, cols] = jnp.broadcast_to(jnp.exp(glast), (SUBLANES, LANES))
                chains.append(dict(n=n, rhs=rhs, rows=rows, cols=cols, rev=rev))

    def group(i, carry):
        chains = []
        for gi in range(per_iter):
            setup(i * per_iter + gi, chains)
        for ch in chains:
            d0 = jnp.where(same8, ch["n"], 0.0)
            ch["t"] = eye + d0
            ch["pw"] = _dotn(d0, d0, GDN_INVERSE_PASSES)
        for ch in chains:
            both = _dotn(jnp.concatenate([ch["t"], ch["pw"]], axis=0), ch["pw"], GDN_INVERSE_PASSES)
            ch["t"] = ch["t"] + both[0:LANES]
            ch["pw"] = both[LANES:2 * LANES]
        for ch in chains:
            ch["t"] = ch["t"] + _dotn(ch["t"], ch["pw"], GDN_INVERSE_PASSES)
        for off_mask in level_masks:
            for ch in chains:
                ch["x"] = _dotn(jnp.where(off_mask, ch["n"], 0.0), ch["t"], GDN_INVERSE_PASSES)
            for ch in chains:
                ch["t"] = ch["t"] + _dotn(ch["t"], ch["x"], GDN_INVERSE_PASSES)
        for ch in chains:
            sol = _dotn(ch["t"], ch["rhs"], GDN_SOLVE_PASSES)
            u_ref, w_ref = dirs[ch["rev"]][0:2]
            u_ref[ch["rows"], ch["cols"]] = jnp.where(lo, sol[0:cs, 0:LANES], sol[cs:2 * cs, 0:LANES])
            w_ref[ch["rows"], ch["cols"]] = jnp.where(lo, sol[0:cs, LANES:], sol[cs:2 * cs, LANES:]).astype(BF16)
        return carry

    per_iter = math.gcd(GDN_CHUNKS_PER_ITER, tm // cs)
    lax.fori_loop(0, tm // (cs * per_iter), group, 0)


def _gdn_local(nqkv, na, nb, conv_w, alog, dtb):
    b, t, w3 = nqkv.shape
    tm = min(GDN_BLOCK_ROWS, t)
    nblk = t // tm
    cpb = tm // GDN_CHUNK
    halo = SUBLANES
    hb = tm // halo
    tok = lambda width: pl.BlockSpec((None, tm, width), lambda i, j: (i, j, 0))
    const2 = lambda shape: pl.BlockSpec(shape, lambda i, j: (0, 0))
    in_specs = [
        tok(w3),
        pl.BlockSpec((None, halo, w3), lambda i, j: (i, jnp.maximum(j * hb - 1, 0), 0)),
        pl.BlockSpec((None, halo, w3), lambda i, j: (i, jnp.minimum((j + 1) * hb, t // halo - 1), 0)),
        const2((GDN_CONV, w3)), tok(LANES), tok(LANES), const2((1, LANES)), const2((1, LANES)),
    ]
    per_dir_specs = [tok(GDN_WIDTH)] * 5 + [pl.BlockSpec((None, cpb, SUBLANES, GDN_WIDTH), lambda i, j: (i, j, 0, 0))]
    per_dir_shapes = ([jax.ShapeDtypeStruct((b, t, GDN_WIDTH), F32)]
                      + [jax.ShapeDtypeStruct((b, t, GDN_WIDTH), BF16)] * 4
                      + [jax.ShapeDtypeStruct((b, t // GDN_CHUNK, SUBLANES, GDN_WIDTH), F32)])
    outs = pl.pallas_call(
        _gdn_local_kernel,
        grid=(b, nblk),
        in_specs=in_specs,
        out_specs=per_dir_specs * 2,
        out_shape=per_dir_shapes * 2,
        scratch_shapes=[
            pltpu.VMEM((tm + 2 * halo, w3), F32),
            pltpu.VMEM((tm, GDN_WIDTH), F32), pltpu.VMEM((tm, GDN_WIDTH), F32), pltpu.VMEM((tm, GDN_WIDTH), F32),
            pltpu.VMEM((tm, LANES), F32), pltpu.VMEM((tm, LANES), F32),
        ],
        compiler_params=_cparams(("parallel", "parallel")),
        name="gdn_local",
    )(nqkv, nqkv, nqkv, conv_w, na, nb, alog, dtb)
    return outs[:6], outs[6:]


def _gdn_scan_kernel(uf_ref, wf_ref, qdf_ref, kdf_ref, qkf_ref, glf_ref,
                     ub_ref, wb_ref, qdb_ref, kdb_ref, qkb_ref, glb_ref,
                     s0f_ref, s0b_ref,
                     of_ref, ob_ref, sff_ref, sfb_ref,
                     stf, stb):
    n = pl.program_id(1)
    cs = GDN_CHUNK
    nbat = uf_ref.shape[0]
    cpb = uf_ref.shape[1] // cs

    @pl.when(n == 0)
    def _():
        stf[...] = s0f_ref[...]
        stb[...] = s0b_ref[...]

    lo = lax.broadcasted_iota(jnp.int32, (cs, LANES), 1) < HEAD_LANES
    r128 = lax.broadcasted_iota(jnp.int32, (LANES, LANES), 0) // HEAD_LANES
    c128 = lax.broadcasted_iota(jnp.int32, (LANES, LANES), 1) // HEAD_LANES
    bd = r128 == c128

    fwd = (uf_ref, wf_ref, qdf_ref, kdf_ref, qkf_ref, glf_ref, of_ref, stf)
    bwd = (ub_ref, wb_ref, qdb_ref, kdb_ref, qkb_ref, glb_ref, ob_ref, stb)

    for cc in range(cpb):
        chains = []
        for c, (u_ref, w_ref, qd_ref, kd_ref, qk_ref, gl_ref, o_ref, st) in ((cc, fwd), (cpb - 1 - cc, bwd)):
            rows = slice(c * cs, (c + 1) * cs)
            for bi in range(nbat):
                for p in range(GDN_PAIRS):
                    cols = slice(p * LANES, (p + 1) * LANES)
                    s = st[bi, p]
                    sb = s.astype(BF16)
                    ws = _dot_nt(w_ref[bi, rows, cols], sb)
                    qs = _dot_nt(qd_ref[bi, rows, cols], sb)
                    chains.append((c, bi, rows, cols, p, s, ws, qs, u_ref, kd_ref, qk_ref, gl_ref, o_ref, st))
        for c, bi, rows, cols, p, s, ws, qs, u_ref, kd_ref, qk_ref, gl_ref, o_ref, st in chains:
            vnew = u_ref[bi, rows, cols] - ws
            vb = vnew.astype(BF16)
            zero = jnp.zeros_like(vb)
            vstack = jnp.concatenate([jnp.where(lo, vb, zero), jnp.where(lo, zero, vb)], axis=0)
            o_ref[bi, rows, cols] = qs + _dot(qk_ref[bi, rows, cols], vstack)
            upd = _dot(vnew.T.astype(BF16), kd_ref[bi, rows, cols])
            st[bi, p] = s * gl_ref[bi, c, 0:1, cols] + jnp.where(bd, upd, 0.0)

    @pl.when(n == pl.num_programs(1) - 1)
    def _():
        sff_ref[...] = stf[...]
        sfb_ref[...] = stb[...]


def _gdn_scan(fwd, bwd, s0f, s0b):
    b, t, _ = fwd[0].shape
    tm = min(GDN_BLOCK_ROWS, t)
    nblk = t // tm
    cpb = tm // GDN_CHUNK
    bb = math.gcd(GDN_SCAN_BATCH, b)
    f_tok = pl.BlockSpec((bb, tm, GDN_WIDTH), lambda i, j: (i, j, 0))
    b_tok = pl.BlockSpec((bb, tm, GDN_WIDTH), lambda i, j: (i, nblk - 1 - j, 0))
    f_gl = pl.BlockSpec((bb, cpb, SUBLANES, GDN_WIDTH), lambda i, j: (i, j, 0, 0))
    b_gl = pl.BlockSpec((bb, cpb, SUBLANES, GDN_WIDTH), lambda i, j: (i, nblk - 1 - j, 0, 0))
    st_spec = pl.BlockSpec((bb, GDN_PAIRS, LANES, LANES), lambda i, j: (i, 0, 0, 0))
    st_shape = jax.ShapeDtypeStruct((b, GDN_PAIRS, LANES, LANES), F32)
    o_shape = jax.ShapeDtypeStruct((b, t, GDN_WIDTH), F32)
    return pl.pallas_call(
        _gdn_scan_kernel,
        grid=(b // bb, nblk),
        in_specs=[f_tok] * 5 + [f_gl] + [b_tok] * 5 + [b_gl] + [st_spec, st_spec],
        out_specs=[f_tok, b_tok, st_spec, st_spec],
        out_shape=[o_shape, o_shape, st_shape, st_shape],
        scratch_shapes=[pltpu.VMEM((bb, GDN_PAIRS, LANES, LANES), F32)] * 2,
        compiler_params=_cparams(("parallel", "arbitrary")),
        name="gdn_scan",
    )(*fwd, *bwd, s0f, s0b)


def _mix_ffn_kernel(h_ref, d_ref, gq_ref, of_ref, ob_ref, z_ref, ng_ref, gate1_ref, wo_ref,
                    g_ref, sh_ref, sc_ref, gate_ref, wgu_ref, wd_ref, fg_ref, o_ref, *, final_norm):
    gm = _group_matrix(1.0 / GDN_HEAD_DIM)
    pieces = [d_ref[...], gq_ref[...]]
    for blk in range(GDN_PAIRS):
        cols = slice(blk * LANES, (blk + 1) * LANES)
        o = of_ref[:, cols] + ob_ref[:, cols]
        ms = _group_sum(o * o, gm)
        r = o * lax.rsqrt(ms + NORM_EPS) * ng_ref[...] * jax.nn.silu(z_ref[:, cols])
        pieces.append(r.astype(BF16))
    x = h_ref[...] + gate1_ref[...] * _dot(jnp.concatenate(pieces, axis=1), wo_ref[...])

    y = x * lax.rsqrt(jnp.mean(x * x, axis=-1, keepdims=True) + NORM_EPS) * g_ref[...]
    a = (y * (1.0 + sc_ref[...]) + sh_ref[...]).astype(BF16)
    hidden = wd_ref.shape[0]
    starts = list(range(0, hidden, FFN_CHUNK))

    def gate_up(lo):
        width = min(FFN_CHUNK, hidden - lo)
        return _dot(a, wgu_ref[:, lo:lo + width]), _dot(a, wgu_ref[:, hidden + lo:hidden + lo + width]), width

    pending = [gate_up(lo) for lo in starts[:FFN_SKEW]]
    acc = None
    for n, lo in enumerate(starts):
        gate, up, width = pending.pop(0)
        if n + FFN_SKEW < len(starts):
            pending.append(gate_up(starts[n + FFN_SKEW]))
        act = (jax.nn.silu(gate) * up).astype(BF16)
        part = _dot(act, wd_ref[lo:lo + width, :])
        acc = part if acc is None else acc + part
    out = x + gate_ref[...] * acc
    if final_norm:
        out = out * lax.rsqrt(jnp.mean(out * out, axis=-1, keepdims=True) + NORM_EPS) * fg_ref[...]
    o_ref[...] = out


def _mix_ffn(h, d, gq, of, ob, z, ng, gate1, w_out, g, sh, sc, gate, w_gu, w_down, fg, final_norm):
    b, t, dm = h.shape
    tm = min(MIX_FFN_ROWS, t)
    nb = sh.shape[0]
    mod_map = (lambda i, k: (i, 0, 0)) if nb == b else (lambda i, k: (0, 0, 0))
    tok = lambda width: pl.BlockSpec((None, tm, width), lambda i, k: (i, k, 0))
    vec = lambda width: pl.BlockSpec((1, width), lambda i, k: (0, 0))
    mod = pl.BlockSpec((None, 1, dm), mod_map)
    resident = lambda w: pl.BlockSpec(w.shape, lambda i, k: (0, 0), pipeline_mode=pl.Buffered(1))
    return pl.pallas_call(
        functools.partial(_mix_ffn_kernel, final_norm=final_norm),
        grid=(b, t // tm),
        in_specs=[tok(dm), tok(DIFF_WIDTH), tok(GQA_WIDTH), tok(GDN_WIDTH), tok(GDN_WIDTH), tok(GDN_WIDTH),
                  vec(LANES), mod, resident(w_out),
                  vec(dm), mod, mod, mod, resident(w_gu), resident(w_down), vec(dm)],
        out_specs=tok(dm),
        out_shape=jax.ShapeDtypeStruct((b, t, dm), F32),
        compiler_params=_cparams(("parallel", "parallel")),
        name="mix_ffn",
    )(h, d, gq, of, ob, z, ng, gate1, w_out, g, sh, sc, gate, w_gu, w_down, fg)


def _rope_tables(t, rot_dim):
    nf = rot_dim // 4
    pos = jnp.arange(t)
    row = (pos // GRID_W).astype(F32)
    col = (pos % GRID_W).astype(F32)
    inv_freq = ROPE_THETA ** (-jnp.arange(nf, dtype=F32) / nf)
    ar, ac = row[:, None] * inv_freq, col[:, None] * inv_freq
    z = jnp.zeros_like(ar)
    cos = jnp.concatenate([jnp.cos(ar), jnp.cos(ar), jnp.cos(ac), jnp.cos(ac)], axis=1)
    sa = jnp.concatenate([-jnp.sin(ar), z, -jnp.sin(ac), z], axis=1)
    sb = jnp.concatenate([z, jnp.sin(ar), z, jnp.sin(ac)], axis=1)
    rep = LANES // rot_dim
    return tuple(jnp.tile(a, (1, rep)) for a in (cos, sa, sb))


def _pad_lanes(v):
    return jnp.pad(v.reshape(1, -1), ((0, 0), (0, LANES - v.size)))


def _relayout_w_in(w):
    o = 0
    pieces = []
    for width in (DIFF_WIDTH, DIFF_WIDTH, DIFF_WIDTH):
        pieces.append(w[:, o:o + width]); o += width
    gq = w[:, o:o + GQA_WIDTH].reshape(-1, GQA_Q_HEADS, GQA_HEAD_DIM); o += GQA_WIDTH
    pieces.append(jnp.take(gq, jnp.array(GQA_Q_ORDER), axis=1).reshape(-1, GQA_WIDTH))
    for width in (GQA_KV_WIDTH, GQA_KV_WIDTH, 3 * GDN_WIDTH, GDN_WIDTH):
        pieces.append(w[:, o:o + width]); o += width
    for width in (2 * GDN_HEADS, 2 * GDN_HEADS):
        pieces.append(jnp.pad(w[:, o:o + width], ((0, 0), (0, LANES - width)))); o += width
    return jnp.concatenate(pieces, axis=1).astype(BF16)


def _relayout_w_out(w):
    gq = w[DIFF_WIDTH:DIFF_WIDTH + GQA_WIDTH].reshape(GQA_Q_HEADS, GQA_HEAD_DIM, -1)
    gq = jnp.take(gq, jnp.array(GQA_Q_ORDER), axis=0).reshape(GQA_WIDTH, -1)
    return jnp.concatenate([w[:DIFF_WIDTH], gq, w[DIFF_WIDTH + GQA_WIDTH:]], axis=0).astype(BF16)


def kernel(x, c, ctx, c_ctx, norm1_g, ada_w, ada_b, w_in, diff_lambda, diff_norm_g, q_norm_g, k_norm_g,
           gdn_conv_w, gdn_a_log, gdn_dt_bias, gdn_norm_g, w_out, norm2_g, ffn_w_gu, ffn_w_down, final_norm_g):
    b, t, d = x.shape
    depth = w_in.shape[0]
    rope = _rope_tables(t, DIFF_QK_DIM) + _rope_tables(t, GQA_HEAD_DIM)

    cond = jnp.concatenate([c, c_ctx[None, :], jnp.zeros((16 - b - 1, d), F32)], axis=0)
    mod = _ada(cond, ada_w, ada_b).reshape(depth, 16, 6, d)

    tile2 = lambda v: jnp.tile(v.reshape(1, -1), (1, LANES // v.size))
    zeros_state = jnp.zeros((b, GDN_PAIRS, LANES, LANES), F32)

    h, hc = x, ctx
    for layer in range(depth):
        need_ctx = layer < depth - 1
        lambda_init = 0.8 - 0.6 * math.exp(-0.3 * layer)
        mod_l = [mod[layer, :b, k][:, None, :] for k in range(6)]
        mod_c = [mod[layer, b:b + 1, k][:, None, :] for k in range(6)]
        w_in_l = _relayout_w_in(w_in[layer])
        w_out_l = _relayout_w_out(w_out[layer])
        w_gu_l = ffn_w_gu[layer].astype(BF16)
        w_down_l = ffn_w_down[layer].astype(BF16)
        g1 = norm1_g[layer].reshape(1, d)
        g2 = norm2_g[layer].reshape(1, d)
        qg, kg = tile2(q_norm_g[layer]), tile2(k_norm_g[layer])
        dng, nng = tile2(diff_norm_g[layer]), tile2(gdn_norm_g[layer])
        alog, dtb = _pad_lanes(gdn_a_log[layer]), _pad_lanes(gdn_dt_bias[layer])
        lam = diff_lambda[layer]
        conv_w = gdn_conv_w[layer]

        pl_ = _in_proj(h, g1, mod_l[0], mod_l[1], w_in_l, qg, kg, rope)
        pc_ = _in_proj(hc, g1, mod_c[0], mod_c[1], w_in_l, qg, kg, None)
        dq_l, dk_l, dv_l, gq_l, gk_l, gv_l, nqkv_l, nz_l, na_l, nb_l = pl_
        dq_c, dk_c, dv_c, gq_c, gk_c, gv_c, nqkv_c, nz_c, na_c, nb_c = pc_

        d_l = _attention("diff", dq_l, dk_c, dv_c, dk_l, dv_l, (lam, dng), lambda_init)
        a_l = _attention("gqa", gq_l, gk_c, gv_c, gk_l, gv_l, ())

        fwd_c, bwd_c = _gdn_local(nqkv_c, na_c, nb_c, conv_w, alog, dtb)
        fwd_l, bwd_l = _gdn_local(nqkv_l, na_l, nb_l, conv_w, alog, dtb)
        ocf, ocb, scf, scb = _gdn_scan(fwd_c, bwd_c, zeros_state, zeros_state)
        olf, olb, _, _ = _gdn_scan(fwd_l, bwd_l, scf, scb)

        last = layer == depth - 1
        fg = final_norm_g.reshape(1, d)
        h = _mix_ffn(h, d_l, a_l, olf, olb, nz_l, nng, mod_l[2], w_out_l,
                     g2, mod_l[3], mod_l[4], mod_l[5], w_gu_l, w_down_l, fg, last)
        if need_ctx:
            d_c = _attention("diff", dq_c, dk_c, dv_c, None, None, (lam, dng), lambda_init)
            a_c = _attention("gqa", gq_c, gk_c, gv_c, None, None, ())
            hc = _mix_ffn(hc, d_c, a_c, ocf, ocb, nz_c, nng, mod_c[2], w_out_l,
                          g2, mod_c[3], mod_c[4], mod_c[5], w_gu_l, w_down_l, fg, False)
    return h
```

```python
import functools
import math

import jax
import jax.numpy as jnp
from jax import lax
from jax.experimental import pallas as pl
from jax.experimental.pallas import tpu as pltpu

F32 = jnp.float32
BF16 = jnp.bfloat16

LANES = 128
SUBLANES = 8
D_MODEL = 1024
GRID_W = 64
ROPE_THETA = 10000.0
NORM_EPS = 1e-6
L2_EPS = 1e-6
LOG2E = 1.4426950408889634

DIFF_HEADS = 4
DIFF_QK_DIM = 32
DIFF_V_DIM = 64
DIFF_WIDTH = DIFF_HEADS * DIFF_V_DIM
GQA_Q_HEADS = 6
GQA_KV_HEADS = 2
GQA_HEAD_DIM = 64
GQA_WIDTH = GQA_Q_HEADS * GQA_HEAD_DIM
GQA_KV_WIDTH = GQA_KV_HEADS * GQA_HEAD_DIM
GDN_HEADS = 6
GDN_HEAD_DIM = 64
GDN_WIDTH = GDN_HEADS * GDN_HEAD_DIM
GDN_CONV = 5
GDN_CHUNK = 64
GDN_PAIRS = GDN_HEADS // 2
FFN_HIDDEN = 2816

OFF_DQ = 0
OFF_DK = OFF_DQ + DIFF_WIDTH
OFF_DV = OFF_DK + DIFF_WIDTH
OFF_GQ = OFF_DV + DIFF_WIDTH
OFF_GK = OFF_GQ + GQA_WIDTH
OFF_GV = OFF_GK + GQA_KV_WIDTH
OFF_NQKV = OFF_GV + GQA_KV_WIDTH
OFF_NZ = OFF_NQKV + 3 * GDN_WIDTH
OFF_NA = OFF_NZ + GDN_WIDTH
OFF_NB = OFF_NA + LANES
IN_PAD = OFF_NB + LANES

GQA_Q_ORDER = (0, 3, 1, 4, 2, 5)
HEAD_LANES = 64
HEAD_V = HEAD_LANES
ONES_ROWS = 16
V_ROWS = HEAD_V + ONES_ROWS

GDN_INVERSE_PASSES = 1
GDN_SOLVE_PASSES = 1
FFN_CHUNK = 512
FFN_SKEW = 1
GDN_SCAN_BATCH = 4
GDN_CHUNKS_PER_ITER = 4

ATTN_SKEW = {"diff": 3, "gqa": 4}
ATTN_CHUNKS_PER_ITER = 16
ATTN_KEY_CHUNK = 512
ATTN_QUERY_TILE = 256
IN_PROJ_ROWS = 512
MIX_FFN_ROWS = 512
GDN_BLOCK_ROWS = 256

VMEM_LIMIT = 56 * 1024 * 1024


def _cparams(sem):
    return pltpu.CompilerParams(dimension_semantics=sem, vmem_limit_bytes=VMEM_LIMIT)


def _dot(a, b):
    return jnp.dot(a, b, preferred_element_type=F32)


def _dot_nt(a, b):
    return lax.dot_general(a, b, (((1,), (1,)), ((), ())), preferred_element_type=F32)


def _split_bf16(x):
    hi = x.astype(BF16)
    lo = (x - hi.astype(F32)).astype(BF16)
    return hi, lo


def _dot3(a, b):
    ah, al = _split_bf16(a)
    bh, bl = _split_bf16(b)
    return _dot(ah, bh) + _dot(ah, bl) + _dot(al, bh)


def _dotn(a, b, passes):
    if passes == 1:
        return _dot(a.astype(BF16), b.astype(BF16))
    assert passes == 3
    return _dot3(a, b)


def _group_matrix(scale):
    r = lax.broadcasted_iota(jnp.int32, (LANES, LANES), 0) // HEAD_LANES
    c = lax.broadcasted_iota(jnp.int32, (LANES, LANES), 1) // HEAD_LANES
    return jnp.where(r == c, scale, 0.0).astype(BF16)


def _group_sum(x, gm):
    hi, lo = _split_bf16(x)
    return _dot(hi, gm) + _dot(lo, gm)


def _rope(x, c, sa, sb, half):
    return x * c + pltpu.roll(x, LANES - half, 1) * sa + pltpu.roll(x, half, 1) * sb


def _ada_kernel(c_ref, w_ref, b_ref, o_ref):
    s = jax.nn.silu(c_ref[...]).astype(BF16)
    o_ref[...] = _dot(s, w_ref[...].astype(BF16)) + b_ref[...]


def _ada(cond, ada_w, ada_b):
    depth, d, n = ada_w.shape
    rows = cond.shape[0]
    tn = 1536
    return pl.pallas_call(
        _ada_kernel,
        grid=(depth, n // tn),
        in_specs=[
            pl.BlockSpec((rows, d), lambda l, j: (0, 0)),
            pl.BlockSpec((None, d, tn), lambda l, j: (l, 0, j)),
            pl.BlockSpec((None, 1, tn), lambda l, j: (l, 0, j)),
        ],
        out_specs=pl.BlockSpec((None, rows, tn), lambda l, j: (l, 0, j)),
        out_shape=jax.ShapeDtypeStruct((depth, rows, n), F32),
        compiler_params=_cparams(("parallel", "parallel")),
        name="ada_mod",
    )(cond, ada_w, ada_b.reshape(depth, 1, n))


def _store_values(ref, first_head, v):
    vt = v.T.astype(BF16)
    n_chunks, _, tk = ref.shape
    ones = jnp.ones((ONES_ROWS, tk), BF16)
    for c in range(n_chunks):
        for s in range(2):
            r0 = (first_head + s) * V_ROWS
            ref[c, r0:r0 + HEAD_V, :] = vt[s * HEAD_V:(s + 1) * HEAD_V, c * tk:(c + 1) * tk]
            ref[c, r0 + HEAD_V:r0 + V_ROWS, :] = ones


def _in_proj_kernel(*refs, use_rope):
    if use_rope:
        (x_ref, g_ref, sh_ref, sc_ref, w_ref, qg_ref, kg_ref,
         cd_ref, sad_ref, sbd_ref, cg_ref, sag_ref, sbg_ref,
         dq_ref, dk_ref, dv_ref, gq_ref, gk_ref, gv_ref, nqkv_ref, nz_ref, na_ref, nb_ref) = refs
    else:
        (x_ref, g_ref, sh_ref, sc_ref, w_ref, qg_ref, kg_ref,
         dq_ref, dk_ref, dv_ref, gq_ref, gk_ref, gv_ref, nqkv_ref, nz_ref, na_ref, nb_ref) = refs
    x = x_ref[...]
    y = x * lax.rsqrt(jnp.mean(x * x, axis=-1, keepdims=True) + NORM_EPS) * g_ref[...]
    a = (y * (1.0 + sc_ref[...]) + sh_ref[...]).astype(BF16)

    bounds = (OFF_DQ, OFF_DK, OFF_DV, OFF_GQ, OFF_GK, OFF_NQKV, OFF_NZ, OFF_NA, IN_PAD)
    groups = {}

    def proj(lo, width):
        g0 = max(b for b in bounds if b <= lo)
        g1 = min(b for b in bounds if b > lo)
        assert lo + width <= g1
        if g0 not in groups:
            groups[g0] = _dot(a, w_ref[:, g0:g1])
        return groups[g0][:, lo - g0:lo - g0 + width]

    gm = _group_matrix(1.0 / GQA_HEAD_DIM)

    for off, out, scale in ((OFF_DQ, dq_ref, DIFF_QK_DIM ** -0.5 * LOG2E), (OFF_DK, dk_ref, 1.0)):
        for blk in range(DIFF_WIDTH // LANES):
            p = proj(off + blk * LANES, LANES)
            if use_rope:
                p = _rope(p, cd_ref[...], sad_ref[...], sbd_ref[...], DIFF_QK_DIM // 4)
            out[:, blk * LANES:(blk + 1) * LANES] = (p * scale).astype(BF16)
    for blk in range(DIFF_WIDTH // LANES):
        _store_values(dv_ref, 2 * blk, proj(OFF_DV + blk * LANES, LANES))

    def qk_prep(p, gain, scale):
        ms = _group_sum(p * p, gm)
        p = p * lax.rsqrt(ms + NORM_EPS) * gain
        if use_rope:
            p = _rope(p, cg_ref[...], sag_ref[...], sbg_ref[...], GQA_HEAD_DIM // 4)
        return (p * scale).astype(BF16)

    for blk in range(GQA_WIDTH // LANES):
        p = proj(OFF_GQ + blk * LANES, LANES)
        gq_ref[:, blk * LANES:(blk + 1) * LANES] = qk_prep(p, qg_ref[...], GQA_HEAD_DIM ** -0.5 * LOG2E)
    gk_ref[...] = qk_prep(proj(OFF_GK, LANES), kg_ref[...], 1.0)
    _store_values(gv_ref, 0, proj(OFF_GV, LANES))

    for blk in range(3 * GDN_WIDTH // 384):
        nqkv_ref[:, blk * 384:(blk + 1) * 384] = proj(OFF_NQKV + blk * 384, 384)
    nz_ref[...] = proj(OFF_NZ, GDN_WIDTH)
    na_ref[...] = proj(OFF_NA, LANES)
    nb_ref[...] = proj(OFF_NB, LANES)


def _in_proj(x, g, sh, sc, w, qg, kg, rope):
    b, t, d = x.shape
    tm = min(IN_PROJ_ROWS, t)
    nb = sh.shape[0]
    mod_map = (lambda i, j: (i, 0, 0)) if nb == b else (lambda i, j: (0, 0, 0))
    tok = lambda width: pl.BlockSpec((None, tm, width), lambda i, j: (i, j, 0))
    const2 = lambda shape: pl.BlockSpec(shape, lambda i, j: (0, 0))
    in_specs = [
        tok(d), const2((1, d)),
        pl.BlockSpec((None, 1, d), mod_map), pl.BlockSpec((None, 1, d), mod_map),
        const2((d, IN_PAD)), const2((1, LANES)), const2((1, LANES)),
    ]
    args = [x, g, sh, sc, w, qg, kg]
    if rope is not None:
        in_specs += [pl.BlockSpec((tm, LANES), lambda i, j: (j, 0))] * 6
        args += list(rope)
    widths = (DIFF_WIDTH, DIFF_WIDTH, DIFF_HEADS * V_ROWS, GQA_WIDTH, GQA_KV_WIDTH, GQA_KV_HEADS * V_ROWS,
              3 * GDN_WIDTH, GDN_WIDTH, LANES, LANES)
    dtypes = (BF16,) * 6 + (F32,) * 4
    transposed = (2, 5)
    tk = min(ATTN_KEY_CHUNK, tm)
    out_specs = [pl.BlockSpec((None, tm // tk, wd, tk), lambda i, j: (i, j, 0, 0)) if k in transposed else tok(wd)
                 for k, wd in enumerate(widths)]
    out_shape = [jax.ShapeDtypeStruct((b, t // tk, wd, tk) if k in transposed else (b, t, wd), dt)
                 for k, (wd, dt) in enumerate(zip(widths, dtypes))]
    return pl.pallas_call(
        functools.partial(_in_proj_kernel, use_rope=rope is not None),
        grid=(b, t // tm),
        in_specs=in_specs,
        out_specs=out_specs,
        out_shape=out_shape,
        compiler_params=_cparams(("parallel", "parallel")),
        name="in_proj",
    )(*args)


def _flash_all(cols, vrows, skew, qm_s, m_s, acc_s, kc_ref, vc_ref, kl_ref, vl_ref, n_lat):
    nh = len(cols)
    m_s[...] = jnp.full(m_s.shape, -jnp.inf, F32)
    acc_s[...] = jnp.zeros(acc_s.shape, F32)

    def pipeline(chunk_list):
        items = [(ck, h) for ck in chunk_list for h in range(nh)]

        def scores(item):
            (k_ref, _, tk, j), h = item
            if tk is None:
                rows = slice(None)
            elif isinstance(j, int):
                rows = slice(j * tk, (j + 1) * tk)
            else:
                rows = pl.ds(pl.multiple_of(j * tk, tk), tk)
            return _dot(k_ref[rows, cols[h]:cols[h] + LANES], qm_s[h])

        pending = [scores(it) for it in items[:skew]]
        for n, ((_, v_ref, _, j), h) in enumerate(items):
            s = pending.pop(0)
            if n + skew < len(items):
                pending.append(scores(items[n + skew]))
            m_old = m_s[h]
            m_new = jnp.maximum(m_old, jnp.max(s, axis=0, keepdims=True))
            alpha = jnp.exp2(m_old - m_new)
            p = jnp.exp2(s - m_new)
            acc_s[h] = alpha * acc_s[h] + _dot(v_ref[j, vrows[h]:vrows[h] + V_ROWS, :], p.astype(BF16))
            m_s[h] = m_new

    ctx_chunk = (kc_ref, vc_ref, None, 0)
    if not n_lat:
        pipeline([ctx_chunk])
        return
    tk = vl_ref.shape[-1]
    per_iter = math.gcd(ATTN_CHUNKS_PER_ITER, n_lat)
    if per_iter == n_lat:
        pipeline([ctx_chunk] + [(kl_ref, vl_ref, tk, j) for j in range(n_lat)])
        return
    pipeline([ctx_chunk])

    def body(i, carry):
        pipeline([(kl_ref, vl_ref, tk, i * per_iter + jj) for jj in range(per_iter)])
        return carry

    lax.fori_loop(0, n_lat // per_iter, body, 0)


def _transposed_q(q_ref, blk):
    return q_ref[:, blk * LANES:(blk + 1) * LANES].astype(F32).T


def _normalised(acc_s, h):
    acc = acc_s[h]
    return acc[0:HEAD_V] / acc[HEAD_V:HEAD_V + 1]


def _pair_out(oa, ob):
    return jnp.concatenate([oa, ob], axis=0).T


def _gqa_kernel(*refs, n_lat):
    if n_lat:
        q_ref, kc_ref, vc_ref, kl_ref, vl_ref, o_ref, qm_s, m_s, acc_s = refs
    else:
        q_ref, kc_ref, vc_ref, o_ref, qm_s, m_s, acc_s = refs
        kl_ref = vl_ref = None
    tq = q_ref.shape[0]
    nblk = GQA_WIDTH // LANES
    lo = lax.broadcasted_iota(jnp.int32, (LANES, tq), 0) < GQA_HEAD_DIM
    for blk in range(nblk):
        qt = _transposed_q(q_ref, blk)
        qm_s[2 * blk] = jnp.where(lo, qt, 0.0).astype(BF16)
        qm_s[2 * blk + 1] = jnp.where(lo, 0.0, qt).astype(BF16)
    _flash_all((0,) * (2 * nblk), (0, V_ROWS) * nblk, ATTN_SKEW["gqa"], qm_s, m_s, acc_s, kc_ref, vc_ref, kl_ref, vl_ref, n_lat)
    for blk in range(nblk):
        o = _pair_out(_normalised(acc_s, 2 * blk), _normalised(acc_s, 2 * blk + 1))
        o_ref[:, blk * LANES:(blk + 1) * LANES] = o.astype(BF16)


def _diff_kernel(*refs, n_lat, lambda_init):
    if n_lat:
        q_ref, kc_ref, vc_ref, kl_ref, vl_ref, lam_ref, ng_ref, o_ref, qm_s, m_s, acc_s = refs
    else:
        q_ref, kc_ref, vc_ref, lam_ref, ng_ref, o_ref, qm_s, m_s, acc_s = refs
        kl_ref = vl_ref = None
    tq = q_ref.shape[0]
    nblk = DIFF_WIDTH // LANES
    lf = lam_ref[...]
    lam = (jnp.exp(jnp.sum(lf[0:1] * lf[1:2], axis=-1, keepdims=True))
           - jnp.exp(jnp.sum(lf[2:3] * lf[3:4], axis=-1, keepdims=True)) + lambda_init)
    row = lax.broadcasted_iota(jnp.int32, (LANES, tq), 0)
    gm = _group_matrix(1.0 / DIFF_V_DIM)
    for blk in range(nblk):
        qt = _transposed_q(q_ref, blk)
        for sc in range(4):
            qm_s[4 * blk + sc] = jnp.where((row // DIFF_QK_DIM) == sc, qt, 0.0).astype(BF16)
    cols = tuple(blk * LANES for blk in range(nblk) for _ in range(4))
    vrows = tuple((2 * blk + s) * V_ROWS for blk in range(nblk) for s in range(2) for _ in range(2))
    _flash_all(cols, vrows, ATTN_SKEW["diff"], qm_s, m_s, acc_s, kc_ref, vc_ref, kl_ref, vl_ref, n_lat)
    for blk in range(nblk):
        halves = []
        for s in range(2):
            i0, i1 = 4 * blk + 2 * s, 4 * blk + 2 * s + 1
            halves.append(_normalised(acc_s, i0) - lam * _normalised(acc_s, i1))
        o = _pair_out(halves[0], halves[1])
        ms = _group_sum(o * o, gm)
        o = o * lax.rsqrt(ms + NORM_EPS) * ng_ref[...] * (1.0 - lambda_init)
        o_ref[:, blk * LANES:(blk + 1) * LANES] = o.astype(BF16)


def _attention(kind, q, kc, vc, kl, vl, extra, lambda_init=None):
    b, t, w = q.shape
    tq = min(ATTN_QUERY_TILE, t)
    full = lambda a: pl.BlockSpec((None,) + a.shape[1:], lambda i, j: (i,) + (0,) * (a.ndim - 1))
    in_specs = [pl.BlockSpec((None, tq, w), lambda i, j: (i, j, 0)), full(kc), full(vc)]
    args = [q, kc, vc]
    n_lat = 0
    if kl is not None:
        n_lat = vl.shape[1]
        in_specs += [full(kl), full(vl)]
        args += [kl, vl]
    for e in extra:
        in_specs.append(pl.BlockSpec(e.shape, lambda i, j: (0, 0)))
        args.append(e)
    if kind == "gqa":
        body = functools.partial(_gqa_kernel, n_lat=n_lat)
        nh = GQA_Q_HEADS
    else:
        body = functools.partial(_diff_kernel, n_lat=n_lat, lambda_init=lambda_init)
        nh = 2 * DIFF_HEADS
    return pl.pallas_call(
        body,
        grid=(b, t // tq),
        in_specs=in_specs,
        out_specs=pl.BlockSpec((None, tq, w), lambda i, j: (i, j, 0)),
        out_shape=jax.ShapeDtypeStruct((b, t, w), BF16),
        scratch_shapes=[pltpu.VMEM((nh, LANES, tq), BF16), pltpu.VMEM((nh, 1, tq), F32),
                        pltpu.VMEM((nh, V_ROWS, tq), F32)],
        compiler_params=_cparams(("parallel", "parallel")),
        name=kind + "_attn",
    )(*args)


def _gdn_local_kernel(x_ref, xp_ref, xn_ref, cw_ref, na_ref, nb_ref, alog_ref, dtb_ref,
                      uf_ref, wf_ref, qdf_ref, kdf_ref, qkf_ref, glf_ref,
                      ub_ref, wb_ref, qdb_ref, kdb_ref, qkb_ref, glb_ref,
                      xbuf, q_s, k_s, v_s, g_s, b_s):
    tm = x_ref.shape[0]
    cs = GDN_CHUNK
    i = pl.program_id(1)
    nblk = pl.num_programs(1)
    halo = xp_ref.shape[0]
    xbuf[halo:halo + tm, :] = x_ref[...]
    xbuf[0:halo, :] = jnp.where(i > 0, xp_ref[...], 0.0)
    xbuf[halo + tm:2 * halo + tm, :] = jnp.where(i < nblk - 1, xn_ref[...], 0.0)
    gm = _group_matrix(1.0)
    for part, dst in enumerate((q_s, k_s, v_s)):
        cols = slice(part * GDN_WIDTH, (part + 1) * GDN_WIDTH)
        acc = None
        xall = xbuf[:, cols]
        for j in range(GDN_CONV):
            d = j - GDN_CONV // 2
            xs = xall if d == 0 else pltpu.roll(xall, (-d) % (tm + 2 * halo), 0)
            term = xs[halo:halo + tm] * cw_ref[j:j + 1, cols]
            acc = term if acc is None else acc + term
        y = jax.nn.silu(acc)
        if part < 2:
            scale = GDN_HEAD_DIM ** -0.5 if part == 0 else 1.0
            for blk in range(GDN_PAIRS):
                yb = y[:, blk * LANES:(blk + 1) * LANES]
                ss = _group_sum(yb * yb, gm)
                dst[:, blk * LANES:(blk + 1) * LANES] = yb * lax.rsqrt(ss + L2_EPS) * scale
        else:
            dst[...] = y
    xa = na_ref[...] + dtb_ref[...]
    softplus = jnp.maximum(xa, 0.0) + jnp.log(1.0 + jnp.exp(-jnp.abs(xa)))
    g_s[...] = -jnp.exp(alog_ref[...]) * softplus
    b_s[...] = jax.nn.sigmoid(nb_ref[...])

    ri = lax.broadcasted_iota(jnp.int32, (cs, LANES), 0)
    li = lax.broadcasted_iota(jnp.int32, (cs, LANES), 1)
    lo = li < HEAD_LANES
    tj = li % HEAD_LANES
    r64 = lax.broadcasted_iota(jnp.int32, (cs, cs), 0)
    c64 = lax.broadcasted_iota(jnp.int32, (cs, cs), 1)
    tri_lo = jnp.where(r64 >= c64, 1.0, 0.0).astype(BF16)
    tri_up = jnp.where(r64 <= c64, 1.0, 0.0).astype(BF16)
    r128 = lax.broadcasted_iota(jnp.int32, (LANES, LANES), 0)
    c128 = lax.broadcasted_iota(jnp.int32, (LANES, LANES), 1)
    eye = jnp.where(r128 == c128, 1.0, 0.0)
    same8 = (r128 // 8) == (c128 // 8)
    level_masks = [((r128 // (2 * m)) == (c128 // (2 * m))) & ((r128 // m) != (c128 // m)) for m in (8, 16, 32)]
    lo1 = lax.broadcasted_iota(jnp.int32, (1, LANES), 1) < HEAD_LANES

    def stack(x, zero):
        return jnp.concatenate([jnp.where(lo, x, zero), jnp.where(lo, zero, x)], axis=0)

    dirs = ((uf_ref, wf_ref, qdf_ref, kdf_ref, qkf_ref, glf_ref),
            (ub_ref, wb_ref, qdb_ref, kdb_ref, qkb_ref, glb_ref))

    def setup(c, chains):
        r0 = pl.multiple_of(c * cs, cs)
        rows = pl.ds(r0, cs)
        g = g_s[rows, :]
        be = b_s[rows, :]
        gh = g.astype(BF16)
        r1 = g - gh.astype(F32)
        gmid = r1.astype(BF16)
        glo = (r1 - gmid.astype(F32)).astype(BF16)
        cum_f = _dot(tri_lo, gh) + _dot(tri_lo, gmid) + _dot(tri_lo, glo)
        cum_b = _dot(tri_up, gh) + _dot(tri_up, gmid) + _dot(tri_up, glo)
        gc = jnp.where(li < GDN_HEADS, cum_f, cum_b)
        gt = jnp.concatenate([gc, gc], axis=0).T
        for p in range(GDN_PAIRS):
            cols = slice(p * LANES, (p + 1) * LANES)
            q128 = q_s[rows, cols]
            k128 = k_s[rows, cols]
            v128 = v_s[rows, cols]
            kb = k128.astype(BF16)
            kstack = stack(kb, jnp.zeros_like(kb))
            kk = _dot_nt(kb, kstack)
            qk = _dot_nt(q128.astype(BF16), kstack)
            for rev in range(2):
                la = rev * GDN_HEADS + 2 * p
                gca, gcb = gc[:, la:la + 1], gc[:, la + 1:la + 2]
                bca, bcb = be[:, la:la + 1], be[:, la + 1:la + 2]
                gcol = jnp.where(lo, gca, gcb)
                grow = jnp.where(lo1, gt[la:la + 1, :], gt[la + 1:la + 2, :])
                bcol = jnp.where(lo, bca, bcb)
                if rev:
                    incl, strict = ri <= tj, ri < tj
                    last = 0
                else:
                    incl, strict = ri >= tj, ri > tj
                    last = cs - 1
                decay = jnp.exp(jnp.where(incl, gcol - grow, -jnp.inf))
                a128 = jnp.where(strict, kk * decay * bcol, 0.0)
                n = -stack(a128, 0.0)
                ea, eb = jnp.exp(gca), jnp.exp(gcb)
                rhs = jnp.concatenate([
                    jnp.concatenate([v128 * bca, v128 * bcb], axis=0),
                    jnp.concatenate([k128 * (bca * ea), k128 * (bcb * eb)], axis=0)], axis=1)
                glast = jnp.where(lo1, gc[last:last + 1, la:la + 1], gc[last:last + 1, la + 1:la + 2])
                u_ref, w_ref, qd_ref, kd_ref, qk_ref, gl_ref = dirs[rev]
                qd_ref[rows, cols] = (q128 * jnp.exp(gcol)).astype(BF16)
                kd_ref[rows, cols] = (k128 * jnp.exp(glast - gcol)).astype(BF16)
                qk_ref[rows, cols] = (qk * decay).astype(BF16)
                gl_ref[c, :, cols] = jnp.broadcast_to(jnp.exp(glast), (SUBLANES, LANES))
                chains.append(dict(n=n, rhs=rhs, rows=rows, cols=cols, rev=rev))

    def group(i, carry):
        chains = []
        for gi in range(per_iter):
            setup(i * per_iter + gi, chains)
        for ch in chains:
            d0 = jnp.where(same8, ch["n"], 0.0)
            ch["t"] = eye + d0
            ch["pw"] = _dotn(d0, d0, GDN_INVERSE_PASSES)
        for ch in chains:
            both = _dotn(jnp.concatenate([ch["t"], ch["pw"]], axis=0), ch["pw"], GDN_INVERSE_PASSES)
            ch["t"] = ch["t"] + both[0:LANES]
            ch["pw"] = both[LANES:2 * LANES]
        for ch in chains:
            ch["t"] = ch["t"] + _dotn(ch["t"], ch["pw"], GDN_INVERSE_PASSES)
        for off_mask in level_masks:
            for ch in chains:
                ch["x"] = _dotn(jnp.where(off_mask, ch["n"], 0.0), ch["t"], GDN_INVERSE_PASSES)
            for ch in chains:
                ch["t"] = ch["t"] + _dotn(ch["t"], ch["x"], GDN_INVERSE_PASSES)
        for ch in chains:
            sol = _dotn(ch["t"], ch["rhs"], GDN_SOLVE_PASSES)
            u_ref, w_ref = dirs[ch["rev"]][0:2]
            u_ref[ch["rows"], ch["cols"]] = jnp.where(lo, sol[0:cs, 0:LANES], sol[cs:2 * cs, 0:LANES])
            w_ref[ch["rows"], ch["cols"]] = jnp.where(lo, sol[0:cs, LANES:], sol[cs:2 * cs, LANES:]).astype(BF16)
        return carry

    per_iter = math.gcd(GDN_CHUNKS_PER_ITER, tm // cs)
    lax.fori_loop(0, tm // (cs * per_iter), group, 0)


def _gdn_local(nqkv, na, nb, conv_w, alog, dtb):
    b, t, w3 = nqkv.shape
    tm = min(GDN_BLOCK_ROWS, t)
    nblk = t // tm
    cpb = tm // GDN_CHUNK
    halo = SUBLANES
    hb = tm // halo
    tok = lambda width: pl.BlockSpec((None, tm, width), lambda i, j: (i, j, 0))
    const2 = lambda shape: pl.BlockSpec(shape, lambda i, j: (0, 0))
    in_specs = [
        tok(w3),
        pl.BlockSpec((None, halo, w3), lambda i, j: (i, jnp.maximum(j * hb - 1, 0), 0)),
        pl.BlockSpec((None, halo, w3), lambda i, j: (i, jnp.minimum((j + 1) * hb, t // halo - 1), 0)),
        const2((GDN_CONV, w3)), tok(LANES), tok(LANES), const2((1, LANES)), const2((1, LANES)),
    ]
    per_dir_specs = [tok(GDN_WIDTH)] * 5 + [pl.BlockSpec((None, cpb, SUBLANES, GDN_WIDTH), lambda i, j: (i, j, 0, 0))]
    per_dir_shapes = ([jax.ShapeDtypeStruct((b, t, GDN_WIDTH), F32)]
                      + [jax.ShapeDtypeStruct((b, t, GDN_WIDTH), BF16)] * 4
                      + [jax.ShapeDtypeStruct((b, t // GDN_CHUNK, SUBLANES, GDN_WIDTH), F32)])
    outs = pl.pallas_call(
        _gdn_local_kernel,
        grid=(b, nblk),
        in_specs=in_specs,
        out_specs=per_dir_specs * 2,
        out_shape=per_dir_shapes * 2,
        scratch_shapes=[
            pltpu.VMEM((tm + 2 * halo, w3), F32),
            pltpu.VMEM((tm, GDN_WIDTH), F32), pltpu.VMEM((tm, GDN_WIDTH), F32), pltpu.VMEM((tm, GDN_WIDTH), F32),
            pltpu.VMEM((tm, LANES), F32), pltpu.VMEM((tm, LANES), F32),
        ],
        compiler_params=_cparams(("parallel", "parallel")),
        name="gdn_local",
    )(nqkv, nqkv, nqkv, conv_w, na, nb, alog, dtb)
    return outs[:6], outs[6:]


def _gdn_scan_kernel(uf_ref, wf_ref, qdf_ref, kdf_ref, qkf_ref, glf_ref,
                     ub_ref, wb_ref, qdb_ref, kdb_ref, qkb_ref, glb_ref,
                     s0f_ref, s0b_ref,
                     of_ref, ob_ref, sff_ref, sfb_ref,
                     stf, stb):
    n = pl.program_id(1)
    cs = GDN_CHUNK
    nbat = uf_ref.shape[0]
    cpb = uf_ref.shape[1] // cs

    @pl.when(n == 0)
    def _():
        stf[...] = s0f_ref[...]
        stb[...] = s0b_ref[...]

    lo = lax.broadcasted_iota(jnp.int32, (cs, LANES), 1) < HEAD_LANES
    r128 = lax.broadcasted_iota(jnp.int32, (LANES, LANES), 0) // HEAD_LANES
    c128 = lax.broadcasted_iota(jnp.int32, (LANES, LANES), 1) // HEAD_LANES
    bd = r128 == c128

    fwd = (uf_ref, wf_ref, qdf_ref, kdf_ref, qkf_ref, glf_ref, of_ref, stf)
    bwd = (ub_ref, wb_ref, qdb_ref, kdb_ref, qkb_ref, glb_ref, ob_ref, stb)

    for cc in range(cpb):
        chains = []
        for c, (u_ref, w_ref, qd_ref, kd_ref, qk_ref, gl_ref, o_ref, st) in ((cc, fwd), (cpb - 1 - cc, bwd)):
            rows = slice(c * cs, (c + 1) * cs)
            for bi in range(nbat):
                for p in range(GDN_PAIRS):
                    cols = slice(p * LANES, (p + 1) * LANES)
                    s = st[bi, p]
                    sb = s.astype(BF16)
                    ws = _dot(w_ref[bi, rows, cols], sb)
                    qs = _dot(qd_ref[bi, rows, cols], sb)
                    chains.append((c, bi, rows, cols, p, s, ws, qs, u_ref, kd_ref, qk_ref, gl_ref, o_ref, st))
        for c, bi, rows, cols, p, s, ws, qs, u_ref, kd_ref, qk_ref, gl_ref, o_ref, st in chains:
            vnew = u_ref[bi, rows, cols] - ws
            vb = vnew.astype(BF16)
            zero = jnp.zeros_like(vb)
            vstack = jnp.concatenate([jnp.where(lo, vb, zero), jnp.where(lo, zero, vb)], axis=0)
            o_ref[bi, rows, cols] = qs + _dot(qk_ref[bi, rows, cols], vstack)
            upd = _dot(kd_ref[bi, rows, cols].astype(F32).T.astype(BF16), vb)
            st[bi, p] = s * gl_ref[bi, c, 0:1, cols] + jnp.where(bd, upd, 0.0)

    @pl.when(n == pl.num_programs(1) - 1)
    def _():
        sff_ref[...] = stf[...]
        sfb_ref[...] = stb[...]


def _gdn_scan(fwd, bwd, s0f, s0b):
    b, t, _ = fwd[0].shape
    tm = min(GDN_BLOCK_ROWS, t)
    nblk = t // tm
    cpb = tm // GDN_CHUNK
    bb = math.gcd(GDN_SCAN_BATCH, b)
    f_tok = pl.BlockSpec((bb, tm, GDN_WIDTH), lambda i, j: (i, j, 0))
    b_tok = pl.BlockSpec((bb, tm, GDN_WIDTH), lambda i, j: (i, nblk - 1 - j, 0))
    f_gl = pl.BlockSpec((bb, cpb, SUBLANES, GDN_WIDTH), lambda i, j: (i, j, 0, 0))
    b_gl = pl.BlockSpec((bb, cpb, SUBLANES, GDN_WIDTH), lambda i, j: (i, nblk - 1 - j, 0, 0))
    st_spec = pl.BlockSpec((bb, GDN_PAIRS, LANES, LANES), lambda i, j: (i, 0, 0, 0))
    st_shape = jax.ShapeDtypeStruct((b, GDN_PAIRS, LANES, LANES), F32)
    o_shape = jax.ShapeDtypeStruct((b, t, GDN_WIDTH), F32)
    return pl.pallas_call(
        _gdn_scan_kernel,
        grid=(b // bb, nblk),
        in_specs=[f_tok] * 5 + [f_gl] + [b_tok] * 5 + [b_gl] + [st_spec, st_spec],
        out_specs=[f_tok, b_tok, st_spec, st_spec],
        out_shape=[o_shape, o_shape, st_shape, st_shape],
        scratch_shapes=[pltpu.VMEM((bb, GDN_PAIRS, LANES, LANES), F32)] * 2,
        compiler_params=_cparams(("parallel", "arbitrary")),
        name="gdn_scan",
    )(*fwd, *bwd, s0f, s0b)


def _mix_ffn_kernel(h_ref, d_ref, gq_ref, of_ref, ob_ref, z_ref, ng_ref, gate1_ref, wo_ref,
                    g_ref, sh_ref, sc_ref, gate_ref, wgu_ref, wd_ref, fg_ref, o_ref, *, final_norm):
    gm = _group_matrix(1.0 / GDN_HEAD_DIM)
    pieces = [d_ref[...], gq_ref[...]]
    for blk in range(GDN_PAIRS):
        cols = slice(blk * LANES, (blk + 1) * LANES)
        o = of_ref[:, cols] + ob_ref[:, cols]
        ms = _group_sum(o * o, gm)
        r = o * lax.rsqrt(ms + NORM_EPS) * ng_ref[...] * jax.nn.silu(z_ref[:, cols])
        pieces.append(r.astype(BF16))
    x = h_ref[...] + gate1_ref[...] * _dot(jnp.concatenate(pieces, axis=1), wo_ref[...])

    y = x * lax.rsqrt(jnp.mean(x * x, axis=-1, keepdims=True) + NORM_EPS) * g_ref[...]
    a = (y * (1.0 + sc_ref[...]) + sh_ref[...]).astype(BF16)
    hidden = wd_ref.shape[0]
    starts = list(range(0, hidden, FFN_CHUNK))

    def gate_up(lo):
        width = min(FFN_CHUNK, hidden - lo)
        return _dot(a, wgu_ref[:, lo:lo + width]), _dot(a, wgu_ref[:, hidden + lo:hidden + lo + width]), width

    pending = [gate_up(lo) for lo in starts[:FFN_SKEW]]
    acc = None
    for n, lo in enumerate(starts):
        gate, up, width = pending.pop(0)
        if n + FFN_SKEW < len(starts):
            pending.append(gate_up(starts[n + FFN_SKEW]))
        act = (jax.nn.silu(gate) * up).astype(BF16)
        part = _dot(act, wd_ref[lo:lo + width, :])
        acc = part if acc is None else acc + part
    out = x + gate_ref[...] * acc
    if final_norm:
        out = out * lax.rsqrt(jnp.mean(out * out, axis=-1, keepdims=True) + NORM_EPS) * fg_ref[...]
    o_ref[...] = out


def _mix_ffn(h, d, gq, of, ob, z, ng, gate1, w_out, g, sh, sc, gate, w_gu, w_down, fg, final_norm):
    b, t, dm = h.shape
    tm = min(MIX_FFN_ROWS, t)
    nb = sh.shape[0]
    mod_map = (lambda i, k: (i, 0, 0)) if nb == b else (lambda i, k: (0, 0, 0))
    tok = lambda width: pl.BlockSpec((None, tm, width), lambda i, k: (i, k, 0))
    vec = lambda width: pl.BlockSpec((1, width), lambda i, k: (0, 0))
    mod = pl.BlockSpec((None, 1, dm), mod_map)
    resident = lambda w: pl.BlockSpec(w.shape, lambda i, k: (0, 0), pipeline_mode=pl.Buffered(1))
    return pl.pallas_call(
        functools.partial(_mix_ffn_kernel, final_norm=final_norm),
        grid=(b, t // tm),
        in_specs=[tok(dm), tok(DIFF_WIDTH), tok(GQA_WIDTH), tok(GDN_WIDTH), tok(GDN_WIDTH), tok(GDN_WIDTH),
                  vec(LANES), mod, resident(w_out),
                  vec(dm), mod, mod, mod, resident(w_gu), resident(w_down), vec(dm)],
        out_specs=tok(dm),
        out_shape=jax.ShapeDtypeStruct((b, t, dm), F32),
        compiler_params=_cparams(("parallel", "parallel")),
        name="mix_ffn",
    )(h, d, gq, of, ob, z, ng, gate1, w_out, g, sh, sc, gate, w_gu, w_down, fg)


def _rope_tables(t, rot_dim):
    nf = rot_dim // 4
    pos = jnp.arange(t)
    row = (pos // GRID_W).astype(F32)
    col = (pos % GRID_W).astype(F32)
    inv_freq = ROPE_THETA ** (-jnp.arange(nf, dtype=F32) / nf)
    ar, ac = row[:, None] * inv_freq, col[:, None] * inv_freq
    z = jnp.zeros_like(ar)
    cos = jnp.concatenate([jnp.cos(ar), jnp.cos(ar), jnp.cos(ac), jnp.cos(ac)], axis=1)
    sa = jnp.concatenate([-jnp.sin(ar), z, -jnp.sin(ac), z], axis=1)
    sb = jnp.concatenate([z, jnp.sin(ar), z, jnp.sin(ac)], axis=1)
    rep = LANES // rot_dim
    return tuple(jnp.tile(a, (1, rep)) for a in (cos, sa, sb))


def _pad_lanes(v):
    return jnp.pad(v.reshape(1, -1), ((0, 0), (0, LANES - v.size)))


def _relayout_w_in(w):
    o = 0
    pieces = []
    for width in (DIFF_WIDTH, DIFF_WIDTH, DIFF_WIDTH):
        pieces.append(w[:, o:o + width]); o += width
    gq = w[:, o:o + GQA_WIDTH].reshape(-1, GQA_Q_HEADS, GQA_HEAD_DIM); o += GQA_WIDTH
    pieces.append(jnp.take(gq, jnp.array(GQA_Q_ORDER), axis=1).reshape(-1, GQA_WIDTH))
    for width in (GQA_KV_WIDTH, GQA_KV_WIDTH, 3 * GDN_WIDTH, GDN_WIDTH):
        pieces.append(w[:, o:o + width]); o += width
    for width in (2 * GDN_HEADS, 2 * GDN_HEADS):
        pieces.append(jnp.pad(w[:, o:o + width], ((0, 0), (0, LANES - width)))); o += width
    return jnp.concatenate(pieces, axis=1).astype(BF16)


def _relayout_w_out(w):
    gq = w[DIFF_WIDTH:DIFF_WIDTH + GQA_WIDTH].reshape(GQA_Q_HEADS, GQA_HEAD_DIM, -1)
    gq = jnp.take(gq, jnp.array(GQA_Q_ORDER), axis=0).reshape(GQA_WIDTH, -1)
    return jnp.concatenate([w[:DIFF_WIDTH], gq, w[DIFF_WIDTH + GQA_WIDTH:]], axis=0).astype(BF16)


def kernel(x, c, ctx, c_ctx, norm1_g, ada_w, ada_b, w_in, diff_lambda, diff_norm_g, q_norm_g, k_norm_g,
           gdn_conv_w, gdn_a_log, gdn_dt_bias, gdn_norm_g, w_out, norm2_g, ffn_w_gu, ffn_w_down, final_norm_g):
    b, t, d = x.shape
    depth = w_in.shape[0]
    rope = _rope_tables(t, DIFF_QK_DIM) + _rope_tables(t, GQA_HEAD_DIM)

    cond = jnp.concatenate([c, c_ctx[None, :], jnp.zeros((16 - b - 1, d), F32)], axis=0)
    mod = _ada(cond, ada_w, ada_b).reshape(depth, 16, 6, d)

    tile2 = lambda v: jnp.tile(v.reshape(1, -1), (1, LANES // v.size))
    zeros_state = jnp.zeros((b, GDN_PAIRS, LANES, LANES), F32)

    h, hc = x, ctx
    for layer in range(depth):
        need_ctx = layer < depth - 1
        lambda_init = 0.8 - 0.6 * math.exp(-0.3 * layer)
        mod_l = [mod[layer, :b, k][:, None, :] for k in range(6)]
        mod_c = [mod[layer, b:b + 1, k][:, None, :] for k in range(6)]
        w_in_l = _relayout_w_in(w_in[layer])
        w_out_l = _relayout_w_out(w_out[layer])
        w_gu_l = ffn_w_gu[layer].astype(BF16)
        w_down_l = ffn_w_down[layer].astype(BF16)
        g1 = norm1_g[layer].reshape(1, d)
        g2 = norm2_g[layer].reshape(1, d)
        qg, kg = tile2(q_norm_g[layer]), tile2(k_norm_g[layer])
        dng, nng = tile2(diff_norm_g[layer]), tile2(gdn_norm_g[layer])
        alog, dtb = _pad_lanes(gdn_a_log[layer]), _pad_lanes(gdn_dt_bias[layer])
        lam = diff_lambda[layer]
        conv_w = gdn_conv_w[layer]

        pl_ = _in_proj(h, g1, mod_l[0], mod_l[1], w_in_l, qg, kg, rope)
        pc_ = _in_proj(hc, g1, mod_c[0], mod_c[1], w_in_l, qg, kg, None)
        dq_l, dk_l, dv_l, gq_l, gk_l, gv_l, nqkv_l, nz_l, na_l, nb_l = pl_
        dq_c, dk_c, dv_c, gq_c, gk_c, gv_c, nqkv_c, nz_c, na_c, nb_c = pc_

        d_l = _attention("diff", dq_l, dk_c, dv_c, dk_l, dv_l, (lam, dng), lambda_init)
        a_l = _attention("gqa", gq_l, gk_c, gv_c, gk_l, gv_l, ())

        fwd_c, bwd_c = _gdn_local(nqkv_c, na_c, nb_c, conv_w, alog, dtb)
        fwd_l, bwd_l = _gdn_local(nqkv_l, na_l, nb_l, conv_w, alog, dtb)
        ocf, ocb, scf, scb = _gdn_scan(fwd_c, bwd_c, zeros_state, zeros_state)
        olf, olb, _, _ = _gdn_scan(fwd_l, bwd_l, scf, scb)

        last = layer == depth - 1
        fg = final_norm_g.reshape(1, d)
        h = _mix_ffn(h, d_l, a_l, olf, olb, nz_l, nng, mod_l[2], w_out_l,
                     g2, mod_l[3], mod_l[4], mod_l[5], w_gu_l, w_down_l, fg, last)
        if need_ctx:
            d_c = _attention("diff", dq_c, dk_c, dv_c, None, None, (lam, dng), lambda_init)
            a_c = _attention("gqa", gq_c, gk_c, gv_c, None, None, ())
            hc = _mix_ffn(hc, d_c, a_c, ocf, ocb, nz_c, nng, mod_c[2], w_out_l,
                          g2, mod_c[3], mod_c[4], mod_c[5], w_gu_l, w_down_l, fg, False)
    return h
```

```python
import functools
import math

import jax
import jax.numpy as jnp
from jax import lax
from jax.experimental import pallas as pl
from jax.experimental.pallas import tpu as pltpu

F32 = jnp.float32
BF16 = jnp.bfloat16

LANES = 128
SUBLANES = 8
D_MODEL = 1024
GRID_W = 64
ROPE_THETA = 10000.0
NORM_EPS = 1e-6
L2_EPS = 1e-6
LOG2E = 1.4426950408889634

DIFF_HEADS = 4
DIFF_QK_DIM = 32
DIFF_V_DIM = 64
DIFF_WIDTH = DIFF_HEADS * DIFF_V_DIM
GQA_Q_HEADS = 6
GQA_KV_HEADS = 2
GQA_HEAD_DIM = 64
GQA_WIDTH = GQA_Q_HEADS * GQA_HEAD_DIM
GQA_KV_WIDTH = GQA_KV_HEADS * GQA_HEAD_DIM
GDN_HEADS = 6
GDN_HEAD_DIM = 64
GDN_WIDTH = GDN_HEADS * GDN_HEAD_DIM
GDN_CONV = 5
GDN_CHUNK = 64
GDN_PAIRS = GDN_HEADS // 2
FFN_HIDDEN = 2816

OFF_DQ = 0
OFF_DK = OFF_DQ + DIFF_WIDTH
OFF_DV = OFF_DK + DIFF_WIDTH
OFF_GQ = OFF_DV + DIFF_WIDTH
OFF_GK = OFF_GQ + GQA_WIDTH
OFF_GV = OFF_GK + GQA_KV_WIDTH
OFF_NQKV = OFF_GV + GQA_KV_WIDTH
OFF_NZ = OFF_NQKV + 3 * GDN_WIDTH
OFF_NA = OFF_NZ + GDN_WIDTH
OFF_NB = OFF_NA + LANES
IN_PAD = OFF_NB + LANES

GQA_Q_ORDER = (0, 3, 1, 4, 2, 5)
HEAD_LANES = 64
HEAD_V = HEAD_LANES
ONES_ROWS = 16
V_ROWS = HEAD_V + ONES_ROWS

GDN_INVERSE_PASSES = 1
GDN_SOLVE_PASSES = 1
FFN_CHUNK = 512
FFN_SKEW = 1
GDN_SCAN_BATCH = 4
GDN_CHUNKS_PER_ITER = 4

ATTN_SKEW = {"diff": 3, "gqa": 4}
ATTN_CHUNKS_PER_ITER = 16
ATTN_KEY_CHUNK = 512
ATTN_QUERY_TILE = 256
ATTN_SUBTILES = {"diff": 2, "gqa": 1}
IN_PROJ_ROWS = 512
MIX_FFN_ROWS = 512
GDN_BLOCK_ROWS = 256

VMEM_LIMIT = 56 * 1024 * 1024


def _cparams(sem):
    return pltpu.CompilerParams(dimension_semantics=sem, vmem_limit_bytes=VMEM_LIMIT)


def _dot(a, b):
    return jnp.dot(a, b, preferred_element_type=F32)


def _dot_nt(a, b):
    return lax.dot_general(a, b, (((1,), (1,)), ((), ())), preferred_element_type=F32)


def _split_bf16(x):
    hi = x.astype(BF16)
    lo = (x - hi.astype(F32)).astype(BF16)
    return hi, lo


def _dot3(a, b):
    ah, al = _split_bf16(a)
    bh, bl = _split_bf16(b)
    return _dot(ah, bh) + _dot(ah, bl) + _dot(al, bh)


def _dotn(a, b, passes):
    if passes == 1:
        return _dot(a.astype(BF16), b.astype(BF16))
    assert passes == 3
    return _dot3(a, b)


def _group_matrix(scale):
    r = lax.broadcasted_iota(jnp.int32, (LANES, LANES), 0) // HEAD_LANES
    c = lax.broadcasted_iota(jnp.int32, (LANES, LANES), 1) // HEAD_LANES
    return jnp.where(r == c, scale, 0.0).astype(BF16)


def _group_sum(x, gm):
    hi, lo = _split_bf16(x)
    return _dot(hi, gm) + _dot(lo, gm)


def _rope(x, c, sa, sb, half):
    return x * c + pltpu.roll(x, LANES - half, 1) * sa + pltpu.roll(x, half, 1) * sb


def _ada_kernel(c_ref, w_ref, b_ref, o_ref):
    s = jax.nn.silu(c_ref[...]).astype(BF16)
    o_ref[...] = _dot(s, w_ref[...].astype(BF16)) + b_ref[...]


def _ada(cond, ada_w, ada_b):
    depth, d, n = ada_w.shape
    rows = cond.shape[0]
    tn = 1536
    return pl.pallas_call(
        _ada_kernel,
        grid=(depth, n // tn),
        in_specs=[
            pl.BlockSpec((rows, d), lambda l, j: (0, 0)),
            pl.BlockSpec((None, d, tn), lambda l, j: (l, 0, j)),
            pl.BlockSpec((None, 1, tn), lambda l, j: (l, 0, j)),
        ],
        out_specs=pl.BlockSpec((None, rows, tn), lambda l, j: (l, 0, j)),
        out_shape=jax.ShapeDtypeStruct((depth, rows, n), F32),
        compiler_params=_cparams(("parallel", "parallel")),
        name="ada_mod",
    )(cond, ada_w, ada_b.reshape(depth, 1, n))


def _store_values(ref, first_head, v):
    vt = v.T.astype(BF16)
    n_chunks, _, tk = ref.shape
    ones = jnp.ones((ONES_ROWS, tk), BF16)
    for c in range(n_chunks):
        for s in range(2):
            r0 = (first_head + s) * V_ROWS
            ref[c, r0:r0 + HEAD_V, :] = vt[s * HEAD_V:(s + 1) * HEAD_V, c * tk:(c + 1) * tk]
            ref[c, r0 + HEAD_V:r0 + V_ROWS, :] = ones


def _in_proj_kernel(*refs, use_rope):
    if use_rope:
        (x_ref, g_ref, sh_ref, sc_ref, w_ref, qg_ref, kg_ref,
         cd_ref, sad_ref, sbd_ref, cg_ref, sag_ref, sbg_ref,
         dq_ref, dk_ref, dv_ref, gq_ref, gk_ref, gv_ref, nqkv_ref, nz_ref, na_ref, nb_ref) = refs
    else:
        (x_ref, g_ref, sh_ref, sc_ref, w_ref, qg_ref, kg_ref,
         dq_ref, dk_ref, dv_ref, gq_ref, gk_ref, gv_ref, nqkv_ref, nz_ref, na_ref, nb_ref) = refs
    x = x_ref[...]
    y = x * lax.rsqrt(jnp.mean(x * x, axis=-1, keepdims=True) + NORM_EPS) * g_ref[...]
    a = (y * (1.0 + sc_ref[...]) + sh_ref[...]).astype(BF16)

    bounds = (OFF_DQ, OFF_DK, OFF_DV, OFF_GQ, OFF_GK, OFF_NQKV, OFF_NZ, OFF_NA, IN_PAD)
    groups = {}

    def proj(lo, width):
        g0 = max(b for b in bounds if b <= lo)
        g1 = min(b for b in bounds if b > lo)
        assert lo + width <= g1
        if g0 not in groups:
            groups[g0] = _dot(a, w_ref[:, g0:g1])
        return groups[g0][:, lo - g0:lo - g0 + width]

    gm = _group_matrix(1.0 / GQA_HEAD_DIM)

    for off, out, scale in ((OFF_DQ, dq_ref, DIFF_QK_DIM ** -0.5 * LOG2E), (OFF_DK, dk_ref, 1.0)):
        for blk in range(DIFF_WIDTH // LANES):
            p = proj(off + blk * LANES, LANES)
            if use_rope:
                p = _rope(p, cd_ref[...], sad_ref[...], sbd_ref[...], DIFF_QK_DIM // 4)
            out[:, blk * LANES:(blk + 1) * LANES] = (p * scale).astype(BF16)
    for blk in range(DIFF_WIDTH // LANES):
        _store_values(dv_ref, 2 * blk, proj(OFF_DV + blk * LANES, LANES))

    def qk_prep(p, gain, scale):
        ms = _group_sum(p * p, gm)
        p = p * lax.rsqrt(ms + NORM_EPS) * gain
        if use_rope:
            p = _rope(p, cg_ref[...], sag_ref[...], sbg_ref[...], GQA_HEAD_DIM // 4)
        return (p * scale).astype(BF16)

    for blk in range(GQA_WIDTH // LANES):
        p = proj(OFF_GQ + blk * LANES, LANES)
        gq_ref[:, blk * LANES:(blk + 1) * LANES] = qk_prep(p, qg_ref[...], GQA_HEAD_DIM ** -0.5 * LOG2E)
    gk_ref[...] = qk_prep(proj(OFF_GK, LANES), kg_ref[...], 1.0)
    _store_values(gv_ref, 0, proj(OFF_GV, LANES))

    for blk in range(3 * GDN_WIDTH // 384):
        nqkv_ref[:, blk * 384:(blk + 1) * 384] = proj(OFF_NQKV + blk * 384, 384)
    nz_ref[...] = proj(OFF_NZ, GDN_WIDTH)
    na_ref[...] = proj(OFF_NA, LANES)
    nb_ref[...] = proj(OFF_NB, LANES)


def _in_proj(x, g, sh, sc, w, qg, kg, rope):
    b, t, d = x.shape
    tm = min(IN_PROJ_ROWS, t)
    nb = sh.shape[0]
    mod_map = (lambda i, j: (i, 0, 0)) if nb == b else (lambda i, j: (0, 0, 0))
    tok = lambda width: pl.BlockSpec((None, tm, width), lambda i, j: (i, j, 0))
    const2 = lambda shape: pl.BlockSpec(shape, lambda i, j: (0, 0))
    in_specs = [
        tok(d), const2((1, d)),
        pl.BlockSpec((None, 1, d), mod_map), pl.BlockSpec((None, 1, d), mod_map),
        const2((d, IN_PAD)), const2((1, LANES)), const2((1, LANES)),
    ]
    args = [x, g, sh, sc, w, qg, kg]
    if rope is not None:
        in_specs += [pl.BlockSpec((tm, LANES), lambda i, j: (j, 0))] * 6
        args += list(rope)
    widths = (DIFF_WIDTH, DIFF_WIDTH, DIFF_HEADS * V_ROWS, GQA_WIDTH, GQA_KV_WIDTH, GQA_KV_HEADS * V_ROWS,
              3 * GDN_WIDTH, GDN_WIDTH, LANES, LANES)
    dtypes = (BF16,) * 6 + (F32,) * 4
    transposed = (2, 5)
    tk = min(ATTN_KEY_CHUNK, tm)
    out_specs = [pl.BlockSpec((None, tm // tk, wd, tk), lambda i, j: (i, j, 0, 0)) if k in transposed else tok(wd)
                 for k, wd in enumerate(widths)]
    out_shape = [jax.ShapeDtypeStruct((b, t // tk, wd, tk) if k in transposed else (b, t, wd), dt)
                 for k, (wd, dt) in enumerate(zip(widths, dtypes))]
    return pl.pallas_call(
        functools.partial(_in_proj_kernel, use_rope=rope is not None),
        grid=(b, t // tm),
        in_specs=in_specs,
        out_specs=out_specs,
        out_shape=out_shape,
        compiler_params=_cparams(("parallel", "parallel")),
        name="in_proj",
    )(*args)


def _flash_all(cols, vrows, skew, qm_s, m_s, acc_s, kc_ref, vc_ref, kl_ref, vl_ref, n_lat):
    nh = len(cols)
    m_s[...] = jnp.full(m_s.shape, -jnp.inf, F32)
    acc_s[...] = jnp.zeros(acc_s.shape, F32)

    def pipeline(chunk_list):
        items = [(ck, h) for ck in chunk_list for h in range(nh)]

        def scores(item):
            (k_ref, _, tk, j), h = item
            if tk is None:
                rows = slice(None)
            elif isinstance(j, int):
                rows = slice(j * tk, (j + 1) * tk)
            else:
                rows = pl.ds(pl.multiple_of(j * tk, tk), tk)
            return _dot(k_ref[rows, cols[h]:cols[h] + LANES], qm_s[h])

        pending = [scores(it) for it in items[:skew]]
        for n, ((_, v_ref, _, j), h) in enumerate(items):
            s = pending.pop(0)
            if n + skew < len(items):
                pending.append(scores(items[n + skew]))
            m_old = m_s[h]
            m_new = jnp.maximum(m_old, jnp.max(s, axis=0, keepdims=True))
            alpha = jnp.exp2(m_old - m_new)
            p = jnp.exp2(s - m_new)
            acc_s[h] = alpha * acc_s[h] + _dot(v_ref[j, vrows[h]:vrows[h] + V_ROWS, :], p.astype(BF16))
            m_s[h] = m_new

    ctx_chunk = (kc_ref, vc_ref, None, 0)
    if not n_lat:
        pipeline([ctx_chunk])
        return
    tk = vl_ref.shape[-1]
    per_iter = math.gcd(ATTN_CHUNKS_PER_ITER, n_lat)
    if per_iter == n_lat:
        pipeline([ctx_chunk] + [(kl_ref, vl_ref, tk, j) for j in range(n_lat)])
        return
    pipeline([ctx_chunk])

    def body(i, carry):
        pipeline([(kl_ref, vl_ref, tk, i * per_iter + jj) for jj in range(per_iter)])
        return carry

    lax.fori_loop(0, n_lat // per_iter, body, 0)


def _transposed_q(q_ref, rows, blk):
    return q_ref[rows, blk * LANES:(blk + 1) * LANES].astype(F32).T


def _normalised(acc_s, h):
    acc = acc_s[h]
    return acc[0:HEAD_V] / acc[HEAD_V:HEAD_V + 1]


def _pair_out(oa, ob):
    return jnp.concatenate([oa, ob], axis=0).T


def _gqa_kernel(*refs, n_lat):
    if n_lat:
        q_ref, kc_ref, vc_ref, kl_ref, vl_ref, o_ref, qm_s, m_s, acc_s = refs
    else:
        q_ref, kc_ref, vc_ref, o_ref, qm_s, m_s, acc_s = refs
        kl_ref = vl_ref = None
    tq = qm_s.shape[-1]
    nsub = q_ref.shape[0] // tq
    nblk = GQA_WIDTH // LANES
    nh = 2 * nblk
    lo = lax.broadcasted_iota(jnp.int32, (LANES, tq), 0) < GQA_HEAD_DIM
    for sub in range(nsub):
        rows = slice(sub * tq, (sub + 1) * tq)
        for blk in range(nblk):
            qt = _transposed_q(q_ref, rows, blk)
            qm_s[sub * nh + 2 * blk] = jnp.where(lo, qt, 0.0).astype(BF16)
            qm_s[sub * nh + 2 * blk + 1] = jnp.where(lo, 0.0, qt).astype(BF16)
    _flash_all((0,) * (nh * nsub), (0, V_ROWS) * (nblk * nsub), ATTN_SKEW["gqa"], qm_s, m_s, acc_s,
               kc_ref, vc_ref, kl_ref, vl_ref, n_lat)
    for sub in range(nsub):
        for blk in range(nblk):
            h0 = sub * nh + 2 * blk
            o = _pair_out(_normalised(acc_s, h0), _normalised(acc_s, h0 + 1))
            o_ref[sub * tq:(sub + 1) * tq, blk * LANES:(blk + 1) * LANES] = o.astype(BF16)


def _diff_kernel(*refs, n_lat, lambda_init):
    if n_lat:
        q_ref, kc_ref, vc_ref, kl_ref, vl_ref, lam_ref, ng_ref, o_ref, qm_s, m_s, acc_s = refs
    else:
        q_ref, kc_ref, vc_ref, lam_ref, ng_ref, o_ref, qm_s, m_s, acc_s = refs
        kl_ref = vl_ref = None
    tq = qm_s.shape[-1]
    nsub = q_ref.shape[0] // tq
    nblk = DIFF_WIDTH // LANES
    nh = 4 * nblk
    lf = lam_ref[...]
    lam = (jnp.exp(jnp.sum(lf[0:1] * lf[1:2], axis=-1, keepdims=True))
           - jnp.exp(jnp.sum(lf[2:3] * lf[3:4], axis=-1, keepdims=True)) + lambda_init)
    row = lax.broadcasted_iota(jnp.int32, (LANES, tq), 0)
    gm = _group_matrix(1.0 / DIFF_V_DIM)
    for sub in range(nsub):
        for blk in range(nblk):
            qt = _transposed_q(q_ref, slice(sub * tq, (sub + 1) * tq), blk)
            for sc in range(4):
                qm_s[sub * nh + 4 * blk + sc] = jnp.where((row // DIFF_QK_DIM) == sc, qt, 0.0).astype(BF16)
    cols = tuple(blk * LANES for blk in range(nblk) for _ in range(4)) * nsub
    vrows = tuple((2 * blk + s) * V_ROWS for blk in range(nblk) for s in range(2) for _ in range(2)) * nsub
    _flash_all(cols, vrows, ATTN_SKEW["diff"], qm_s, m_s, acc_s, kc_ref, vc_ref, kl_ref, vl_ref, n_lat)
    for sub in range(nsub):
        for blk in range(nblk):
            halves = []
            for s in range(2):
                i0 = sub * nh + 4 * blk + 2 * s
                halves.append(_normalised(acc_s, i0) - lam * _normalised(acc_s, i0 + 1))
            o = _pair_out(halves[0], halves[1])
            ms = _group_sum(o * o, gm)
            o = o * lax.rsqrt(ms + NORM_EPS) * ng_ref[...] * (1.0 - lambda_init)
            o_ref[sub * tq:(sub + 1) * tq, blk * LANES:(blk + 1) * LANES] = o.astype(BF16)


def _attention(kind, q, kc, vc, kl, vl, extra, lambda_init=None):
    b, t, w = q.shape
    tq = min(ATTN_QUERY_TILE, t)
    nsub = math.gcd(ATTN_SUBTILES[kind], t // tq)
    rows = tq * nsub
    full = lambda a: pl.BlockSpec((None,) + a.shape[1:], lambda i, j: (i,) + (0,) * (a.ndim - 1))
    in_specs = [pl.BlockSpec((None, rows, w), lambda i, j: (i, j, 0)), full(kc), full(vc)]
    args = [q, kc, vc]
    n_lat = 0
    if kl is not None:
        n_lat = vl.shape[1]
        in_specs += [full(kl), full(vl)]
        args += [kl, vl]
    for e in extra:
        in_specs.append(pl.BlockSpec(e.shape, lambda i, j: (0, 0)))
        args.append(e)
    if kind == "gqa":
        body = functools.partial(_gqa_kernel, n_lat=n_lat)
        nh = GQA_Q_HEADS * nsub
    else:
        body = functools.partial(_diff_kernel, n_lat=n_lat, lambda_init=lambda_init)
        nh = 2 * DIFF_HEADS * nsub
    return pl.pallas_call(
        body,
        grid=(b, t // rows),
        in_specs=in_specs,
        out_specs=pl.BlockSpec((None, rows, w), lambda i, j: (i, j, 0)),
        out_shape=jax.ShapeDtypeStruct((b, t, w), BF16),
        scratch_shapes=[pltpu.VMEM((nh, LANES, tq), BF16), pltpu.VMEM((nh, 1, tq), F32),
                        pltpu.VMEM((nh, V_ROWS, tq), F32)],
        compiler_params=_cparams(("parallel", "parallel")),
        name=kind + "_attn",
    )(*args)


def _gdn_local_kernel(x_ref, xp_ref, xn_ref, cw_ref, na_ref, nb_ref, alog_ref, dtb_ref,
                      uf_ref, wf_ref, qdf_ref, kdf_ref, qkf_ref, glf_ref,
                      ub_ref, wb_ref, qdb_ref, kdb_ref, qkb_ref, glb_ref,
                      xbuf, q_s, k_s, v_s, g_s, b_s):
    tm = x_ref.shape[0]
    cs = GDN_CHUNK
    i = pl.program_id(1)
    nblk = pl.num_programs(1)
    halo = xp_ref.shape[0]
    xbuf[halo:halo + tm, :] = x_ref[...]
    xbuf[0:halo, :] = jnp.where(i > 0, xp_ref[...], 0.0)
    xbuf[halo + tm:2 * halo + tm, :] = jnp.where(i < nblk - 1, xn_ref[...], 0.0)
    gm = _group_matrix(1.0)
    for part, dst in enumerate((q_s, k_s, v_s)):
        cols = slice(part * GDN_WIDTH, (part + 1) * GDN_WIDTH)
        acc = None
        xall = xbuf[:, cols]
        for j in range(GDN_CONV):
            d = j - GDN_CONV // 2
            xs = xall if d == 0 else pltpu.roll(xall, (-d) % (tm + 2 * halo), 0)
            term = xs[halo:halo + tm] * cw_ref[j:j + 1, cols]
            acc = term if acc is None else acc + term
        y = jax.nn.silu(acc)
        if part < 2:
            scale = GDN_HEAD_DIM ** -0.5 if part == 0 else 1.0
            for blk in range(GDN_PAIRS):
                yb = y[:, blk * LANES:(blk + 1) * LANES]
                ss = _group_sum(yb * yb, gm)
                dst[:, blk * LANES:(blk + 1) * LANES] = yb * lax.rsqrt(ss + L2_EPS) * scale
        else:
            dst[...] = y
    xa = na_ref[...] + dtb_ref[...]
    softplus = jnp.maximum(xa, 0.0) + jnp.log(1.0 + jnp.exp(-jnp.abs(xa)))
    g_s[...] = -jnp.exp(alog_ref[...]) * softplus
    b_s[...] = jax.nn.sigmoid(nb_ref[...])

    ri = lax.broadcasted_iota(jnp.int32, (cs, LANES), 0)
    li = lax.broadcasted_iota(jnp.int32, (cs, LANES), 1)
    lo = li < HEAD_LANES
    tj = li % HEAD_LANES
    r64 = lax.broadcasted_iota(jnp.int32, (cs, cs), 0)
    c64 = lax.broadcasted_iota(jnp.int32, (cs, cs), 1)
    tri_lo = jnp.where(r64 >= c64, 1.0, 0.0).astype(BF16)
    tri_up = jnp.where(r64 <= c64, 1.0, 0.0).astype(BF16)
    r128 = lax.broadcasted_iota(jnp.int32, (LANES, LANES), 0)
    c128 = lax.broadcasted_iota(jnp.int32, (LANES, LANES), 1)
    eye = jnp.where(r128 == c128, 1.0, 0.0)
    same8 = (r128 // 8) == (c128 // 8)
    level_masks = [((r128 // (2 * m)) == (c128 // (2 * m))) & ((r128 // m) != (c128 // m)) for m in (8, 16, 32)]
    lo1 = lax.broadcasted_iota(jnp.int32, (1, LANES), 1) < HEAD_LANES

    def stack(x, zero):
        return jnp.concatenate([jnp.where(lo, x, zero), jnp.where(lo, zero, x)], axis=0)

    dirs = ((uf_ref, wf_ref, qdf_ref, kdf_ref, qkf_ref, glf_ref),
            (ub_ref, wb_ref, qdb_ref, kdb_ref, qkb_ref, glb_ref))

    def setup(c, chains):
        r0 = pl.multiple_of(c * cs, cs)
        rows = pl.ds(r0, cs)
        g = g_s[rows, :]
        be = b_s[rows, :]
        gh = g.astype(BF16)
        r1 = g - gh.astype(F32)
        gmid = r1.astype(BF16)
        glo = (r1 - gmid.astype(F32)).astype(BF16)
        cum_f = _dot(tri_lo, gh) + _dot(tri_lo, gmid) + _dot(tri_lo, glo)
        cum_b = _dot(tri_up, gh) + _dot(tri_up, gmid) + _dot(tri_up, glo)
        gc = jnp.where(li < GDN_HEADS, cum_f, cum_b)
        gt = jnp.concatenate([gc, gc], axis=0).T
        for p in range(GDN_PAIRS):
            cols = slice(p * LANES, (p + 1) * LANES)
            q128 = q_s[rows, cols]
            k128 = k_s[rows, cols]
            v128 = v_s[rows, cols]
            kb = k128.astype(BF16)
            kstack = stack(kb, jnp.zeros_like(kb))
            kk = _dot_nt(kb, kstack)
            qk = _dot_nt(q128.astype(BF16), kstack)
            for rev in range(2):
                la = rev * GDN_HEADS + 2 * p
                gca, gcb = gc[:, la:la + 1], gc[:, la + 1:la + 2]
                bca, bcb = be[:, la:la + 1], be[:, la + 1:la + 2]
                gcol = jnp.where(lo, gca, gcb)
                grow = jnp.where(lo1, gt[la:la + 1, :], gt[la + 1:la + 2, :])
                bcol = jnp.where(lo, bca, bcb)
                if rev:
                    incl, strict = ri <= tj, ri < tj
                    last = 0
                else:
                    incl, strict = ri >= tj, ri > tj
                    last = cs - 1
                decay = jnp.exp(jnp.where(incl, gcol - grow, -jnp.inf))
                a128 = jnp.where(strict, kk * decay * bcol, 0.0)
                n = -stack(a128, 0.0)
                ea, eb = jnp.exp(gca), jnp.exp(gcb)
                rhs = jnp.concatenate([
                    jnp.concatenate([v128 * bca, v128 * bcb], axis=0),
                    jnp.concatenate([k128 * (bca * ea), k128 * (bcb * eb)], axis=0)], axis=1)
                glast = jnp.where(lo1, gc[last:last + 1, la:la + 1], gc[last:last + 1, la + 1:la + 2])
                u_ref, w_ref, qd_ref, kd_ref, qk_ref, gl_ref = dirs[rev]
                qd_ref[rows, cols] = (q128 * jnp.exp(gcol)).astype(BF16)
                kd_ref[rows, cols] = (k128 * jnp.exp(glast - gcol)).astype(BF16)
                qk_ref[rows, cols] = (qk * decay).astype(BF16)
                gl_ref[c, :, cols] = jnp.broadcast_to(jnp.exp(glast), (SUBLANES, LANES))
                chains.append(dict(n=n, rhs=rhs, rows=rows, cols=cols, rev=rev))

    def group(i, carry):
        chains = []
        for gi in range(per_iter):
            setup(i * per_iter + gi, chains)
        for ch in chains:
            d0 = jnp.where(same8, ch["n"], 0.0)
            ch["t"] = eye + d0
            ch["pw"] = _dotn(d0, d0, GDN_INVERSE_PASSES)
        for ch in chains:
            both = _dotn(jnp.concatenate([ch["t"], ch["pw"]], axis=0), ch["pw"], GDN_INVERSE_PASSES)
            ch["t"] = ch["t"] + both[0:LANES]
            ch["pw"] = both[LANES:2 * LANES]
        for ch in chains:
            ch["t"] = ch["t"] + _dotn(ch["t"], ch["pw"], GDN_INVERSE_PASSES)
        for off_mask in level_masks:
            for ch in chains:
                ch["x"] = _dotn(jnp.where(off_mask, ch["n"], 0.0), ch["t"], GDN_INVERSE_PASSES)
            for ch in chains:
                ch["t"] = ch["t"] + _dotn(ch["t"], ch["x"], GDN_INVERSE_PASSES)
        for ch in chains:
            sol = _dotn(ch["t"], ch["rhs"], GDN_SOLVE_PASSES)
            u_ref, w_ref = dirs[ch["rev"]][0:2]
            u_ref[ch["rows"], ch["cols"]] = jnp.where(lo, sol[0:cs, 0:LANES], sol[cs:2 * cs, 0:LANES])
            w_ref[ch["rows"], ch["cols"]] = jnp.where(lo, sol[0:cs, LANES:], sol[cs:2 * cs, LANES:]).astype(BF16)
        return carry

    per_iter = math.gcd(GDN_CHUNKS_PER_ITER, tm // cs)
    lax.fori_loop(0, tm // (cs * per_iter), group, 0)


def _gdn_local(nqkv, na, nb, conv_w, alog, dtb):
    b, t, w3 = nqkv.shape
    tm = min(GDN_BLOCK_ROWS, t)
    nblk = t // tm
    cpb = tm // GDN_CHUNK
    halo = SUBLANES
    hb = tm // halo
    tok = lambda width: pl.BlockSpec((None, tm, width), lambda i, j: (i, j, 0))
    const2 = lambda shape: pl.BlockSpec(shape, lambda i, j: (0, 0))
    in_specs = [
        tok(w3),
        pl.BlockSpec((None, halo, w3), lambda i, j: (i, jnp.maximum(j * hb - 1, 0), 0)),
        pl.BlockSpec((None, halo, w3), lambda i, j: (i, jnp.minimum((j + 1) * hb, t // halo - 1), 0)),
        const2((GDN_CONV, w3)), tok(LANES), tok(LANES), const2((1, LANES)), const2((1, LANES)),
    ]
    per_dir_specs = [tok(GDN_WIDTH)] * 5 + [pl.BlockSpec((None, cpb, SUBLANES, GDN_WIDTH), lambda i, j: (i, j, 0, 0))]
    per_dir_shapes = ([jax.ShapeDtypeStruct((b, t, GDN_WIDTH), F32)]
                      + [jax.ShapeDtypeStruct((b, t, GDN_WIDTH), BF16)] * 4
                      + [jax.ShapeDtypeStruct((b, t // GDN_CHUNK, SUBLANES, GDN_WIDTH), F32)])
    outs = pl.pallas_call(
        _gdn_local_kernel,
        grid=(b, nblk),
        in_specs=in_specs,
        out_specs=per_dir_specs * 2,
        out_shape=per_dir_shapes * 2,
        scratch_shapes=[
            pltpu.VMEM((tm + 2 * halo, w3), F32),
            pltpu.VMEM((tm, GDN_WIDTH), F32), pltpu.VMEM((tm, GDN_WIDTH), F32), pltpu.VMEM((tm, GDN_WIDTH), F32),
            pltpu.VMEM((tm, LANES), F32), pltpu.VMEM((tm, LANES), F32),
        ],
        compiler_params=_cparams(("parallel", "parallel")),
        name="gdn_local",
    )(nqkv, nqkv, nqkv, conv_w, na, nb, alog, dtb)
    return outs[:6], outs[6:]


def _gdn_scan_kernel(uf_ref, wf_ref, qdf_ref, kdf_ref, qkf_ref, glf_ref,
                     ub_ref, wb_ref, qdb_ref, kdb_ref, qkb_ref, glb_ref,
                     s0f_ref, s0b_ref,
                     of_ref, ob_ref, sff_ref, sfb_ref,
                     stf, stb):
    n = pl.program_id(1)
    cs = GDN_CHUNK
    nbat = uf_ref.shape[0]
    cpb = uf_ref.shape[1] // cs

    @pl.when(n == 0)
    def _():
        stf[...] = s0f_ref[...]
        stb[...] = s0b_ref[...]

    lo = lax.broadcasted_iota(jnp.int32, (cs, LANES), 1) < HEAD_LANES
    r128 = lax.broadcasted_iota(jnp.int32, (LANES, LANES), 0) // HEAD_LANES
    c128 = lax.broadcasted_iota(jnp.int32, (LANES, LANES), 1) // HEAD_LANES
    bd = r128 == c128

    fwd = (uf_ref, wf_ref, qdf_ref, kdf_ref, qkf_ref, glf_ref, of_ref, stf)
    bwd = (ub_ref, wb_ref, qdb_ref, kdb_ref, qkb_ref, glb_ref, ob_ref, stb)

    for cc in range(cpb):
        chains = []
        for c, (u_ref, w_ref, qd_ref, kd_ref, qk_ref, gl_ref, o_ref, st) in ((cc, fwd), (cpb - 1 - cc, bwd)):
            rows = slice(c * cs, (c + 1) * cs)
            for bi in range(nbat):
                for p in range(GDN_PAIRS):
                    cols = slice(p * LANES, (p + 1) * LANES)
                    s = st[bi, p]
                    sb = s.astype(BF16)
                    ws = _dot(w_ref[bi, rows, cols], sb)
                    qs = _dot(qd_ref[bi, rows, cols], sb)
                    chains.append((c, bi, rows, cols, p, s, ws, qs, u_ref, kd_ref, qk_ref, gl_ref, o_ref, st))
        for c, bi, rows, cols, p, s, ws, qs, u_ref, kd_ref, qk_ref, gl_ref, o_ref, st in chains:
            vnew = u_ref[bi, rows, cols] - ws
            vb = vnew.astype(BF16)
            zero = jnp.zeros_like(vb)
            vstack = jnp.concatenate([jnp.where(lo, vb, zero), jnp.where(lo, zero, vb)], axis=0)
            o_ref[bi, rows, cols] = qs + _dot(qk_ref[bi, rows, cols], vstack)
            upd = _dot(kd_ref[bi, rows, cols].astype(F32).T.astype(BF16), vb)
            st[bi, p] = s * gl_ref[bi, c, 0:1, cols] + jnp.where(bd, upd, 0.0)

    @pl.when(n == pl.num_programs(1) - 1)
    def _():
        sff_ref[...] = stf[...]
        sfb_ref[...] = stb[...]


def _gdn_scan(fwd, bwd, s0f, s0b):
    b, t, _ = fwd[0].shape
    tm = min(GDN_BLOCK_ROWS, t)
    nblk = t // tm
    cpb = tm // GDN_CHUNK
    bb = math.gcd(GDN_SCAN_BATCH, b)
    f_tok = pl.BlockSpec((bb, tm, GDN_WIDTH), lambda i, j: (i, j, 0))
    b_tok = pl.BlockSpec((bb, tm, GDN_WIDTH), lambda i, j: (i, nblk - 1 - j, 0))
    f_gl = pl.BlockSpec((bb, cpb, SUBLANES, GDN_WIDTH), lambda i, j: (i, j, 0, 0))
    b_gl = pl.BlockSpec((bb, cpb, SUBLANES, GDN_WIDTH), lambda i, j: (i, nblk - 1 - j, 0, 0))
    st_spec = pl.BlockSpec((bb, GDN_PAIRS, LANES, LANES), lambda i, j: (i, 0, 0, 0))
    st_shape = jax.ShapeDtypeStruct((b, GDN_PAIRS, LANES, LANES), F32)
    o_shape = jax.ShapeDtypeStruct((b, t, GDN_WIDTH), F32)
    return pl.pallas_call(
        _gdn_scan_kernel,
        grid=(b // bb, nblk),
        in_specs=[f_tok] * 5 + [f_gl] + [b_tok] * 5 + [b_gl] + [st_spec, st_spec],
        out_specs=[f_tok, b_tok, st_spec, st_spec],
        out_shape=[o_shape, o_shape, st_shape, st_shape],
        scratch_shapes=[pltpu.VMEM((bb, GDN_PAIRS, LANES, LANES), F32)] * 2,
        compiler_params=_cparams(("parallel", "arbitrary")),
        name="gdn_scan",
    )(*fwd, *bwd, s0f, s0b)


def _mix_ffn_kernel(h_ref, d_ref, gq_ref, of_ref, ob_ref, z_ref, ng_ref, gate1_ref, wo_ref,
                    g_ref, sh_ref, sc_ref, gate_ref, wgu_ref, wd_ref, fg_ref, o_ref, *, final_norm):
    gm = _group_matrix(1.0 / GDN_HEAD_DIM)
    pieces = [d_ref[...], gq_ref[...]]
    for blk in range(GDN_PAIRS):
        cols = slice(blk * LANES, (blk + 1) * LANES)
        o = of_ref[:, cols] + ob_ref[:, cols]
        ms = _group_sum(o * o, gm)
        r = o * lax.rsqrt(ms + NORM_EPS) * ng_ref[...] * jax.nn.silu(z_ref[:, cols])
        pieces.append(r.astype(BF16))
    x = h_ref[...] + gate1_ref[...] * _dot(jnp.concatenate(pieces, axis=1), wo_ref[...])

    y = x * lax.rsqrt(jnp.mean(x * x, axis=-1, keepdims=True) + NORM_EPS) * g_ref[...]
    a = (y * (1.0 + sc_ref[...]) + sh_ref[...]).astype(BF16)
    hidden = wd_ref.shape[0]
    starts = list(range(0, hidden, FFN_CHUNK))

    def gate_up(lo):
        width = min(FFN_CHUNK, hidden - lo)
        return _dot(a, wgu_ref[:, lo:lo + width]), _dot(a, wgu_ref[:, hidden + lo:hidden + lo + width]), width

    pending = [gate_up(lo) for lo in starts[:FFN_SKEW]]
    acc = None
    for n, lo in enumerate(starts):
        gate, up, width = pending.pop(0)
        if n + FFN_SKEW < len(starts):
            pending.append(gate_up(starts[n + FFN_SKEW]))
        act = (jax.nn.silu(gate) * up).astype(BF16)
        part = _dot(act, wd_ref[lo:lo + width, :])
        acc = part if acc is None else acc + part
    out = x + gate_ref[...] * acc
    if final_norm:
        out = out * lax.rsqrt(jnp.mean(out * out, axis=-1, keepdims=True) + NORM_EPS) * fg_ref[...]
    o_ref[...] = out


def _mix_ffn(h, d, gq, of, ob, z, ng, gate1, w_out, g, sh, sc, gate, w_gu, w_down, fg, final_norm):
    b, t, dm = h.shape
    tm = min(MIX_FFN_ROWS, t)
    nb = sh.shape[0]
    mod_map = (lambda i, k: (i, 0, 0)) if nb == b else (lambda i, k: (0, 0, 0))
    tok = lambda width: pl.BlockSpec((None, tm, width), lambda i, k: (i, k, 0))
    vec = lambda width: pl.BlockSpec((1, width), lambda i, k: (0, 0))
    mod = pl.BlockSpec((None, 1, dm), mod_map)
    resident = lambda w: pl.BlockSpec(w.shape, lambda i, k: (0, 0), pipeline_mode=pl.Buffered(1))
    return pl.pallas_call(
        functools.partial(_mix_ffn_kernel, final_norm=final_norm),
        grid=(b, t // tm),
        in_specs=[tok(dm), tok(DIFF_WIDTH), tok(GQA_WIDTH), tok(GDN_WIDTH), tok(GDN_WIDTH), tok(GDN_WIDTH),
                  vec(LANES), mod, resident(w_out),
                  vec(dm), mod, mod, mod, resident(w_gu), resident(w_down), vec(dm)],
        out_specs=tok(dm),
        out_shape=jax.ShapeDtypeStruct((b, t, dm), F32),
        compiler_params=_cparams(("parallel", "parallel")),
        name="mix_ffn",
    )(h, d, gq, of, ob, z, ng, gate1, w_out, g, sh, sc, gate, w_gu, w_down, fg)


def _rope_tables(t, rot_dim):
    nf = rot_dim // 4
    pos = jnp.arange(t)
    row = (pos // GRID_W).astype(F32)
    col = (pos % GRID_W).astype(F32)
    inv_freq = ROPE_THETA ** (-jnp.arange(nf, dtype=F32) / nf)
    ar, ac = row[:, None] * inv_freq, col[:, None] * inv_freq
    z = jnp.zeros_like(ar)
    cos = jnp.concatenate([jnp.cos(ar), jnp.cos(ar), jnp.cos(ac), jnp.cos(ac)], axis=1)
    sa = jnp.concatenate([-jnp.sin(ar), z, -jnp.sin(ac), z], axis=1)
    sb = jnp.concatenate([z, jnp.sin(ar), z, jnp.sin(ac)], axis=1)
    rep = LANES // rot_dim
    return tuple(jnp.tile(a, (1, rep)) for a in (cos, sa, sb))


def _pad_lanes(v):
    return jnp.pad(v.reshape(1, -1), ((0, 0), (0, LANES - v.size)))


def _relayout_w_in(w):
    o = 0
    pieces = []
    for width in (DIFF_WIDTH, DIFF_WIDTH, DIFF_WIDTH):
        pieces.append(w[:, o:o + width]); o += width
    gq = w[:, o:o + GQA_WIDTH].reshape(-1, GQA_Q_HEADS, GQA_HEAD_DIM); o += GQA_WIDTH
    pieces.append(jnp.take(gq, jnp.array(GQA_Q_ORDER), axis=1).reshape(-1, GQA_WIDTH))
    for width in (GQA_KV_WIDTH, GQA_KV_WIDTH, 3 * GDN_WIDTH, GDN_WIDTH):
        pieces.append(w[:, o:o + width]); o += width
    for width in (2 * GDN_HEADS, 2 * GDN_HEADS):
        pieces.append(jnp.pad(w[:, o:o + width], ((0, 0), (0, LANES - width)))); o += width
    return jnp.concatenate(pieces, axis=1).astype(BF16)


def _relayout_w_out(w):
    gq = w[DIFF_WIDTH:DIFF_WIDTH + GQA_WIDTH].reshape(GQA_Q_HEADS, GQA_HEAD_DIM, -1)
    gq = jnp.take(gq, jnp.array(GQA_Q_ORDER), axis=0).reshape(GQA_WIDTH, -1)
    return jnp.concatenate([w[:DIFF_WIDTH], gq, w[DIFF_WIDTH + GQA_WIDTH:]], axis=0).astype(BF16)


def kernel(x, c, ctx, c_ctx, norm1_g, ada_w, ada_b, w_in, diff_lambda, diff_norm_g, q_norm_g, k_norm_g,
           gdn_conv_w, gdn_a_log, gdn_dt_bias, gdn_norm_g, w_out, norm2_g, ffn_w_gu, ffn_w_down, final_norm_g):
    b, t, d = x.shape
    depth = w_in.shape[0]
    rope = _rope_tables(t, DIFF_QK_DIM) + _rope_tables(t, GQA_HEAD_DIM)

    cond = jnp.concatenate([c, c_ctx[None, :], jnp.zeros((16 - b - 1, d), F32)], axis=0)
    mod = _ada(cond, ada_w, ada_b).reshape(depth, 16, 6, d)

    tile2 = lambda v: jnp.tile(v.reshape(1, -1), (1, LANES // v.size))
    zeros_state = jnp.zeros((b, GDN_PAIRS, LANES, LANES), F32)

    h, hc = x, ctx
    for layer in range(depth):
        need_ctx = layer < depth - 1
        lambda_init = 0.8 - 0.6 * math.exp(-0.3 * layer)
        mod_l = [mod[layer, :b, k][:, None, :] for k in range(6)]
        mod_c = [mod[layer, b:b + 1, k][:, None, :] for k in range(6)]
        w_in_l = _relayout_w_in(w_in[layer])
        w_out_l = _relayout_w_out(w_out[layer])
        w_gu_l = ffn_w_gu[layer].astype(BF16)
        w_down_l = ffn_w_down[layer].astype(BF16)
        g1 = norm1_g[layer].reshape(1, d)
        g2 = norm2_g[layer].reshape(1, d)
        qg, kg = tile2(q_norm_g[layer]), tile2(k_norm_g[layer])
        dng, nng = tile2(diff_norm_g[layer]), tile2(gdn_norm_g[layer])
        alog, dtb = _pad_lanes(gdn_a_log[layer]), _pad_lanes(gdn_dt_bias[layer])
        lam = diff_lambda[layer]
        conv_w = gdn_conv_w[layer]

        pl_ = _in_proj(h, g1, mod_l[0], mod_l[1], w_in_l, qg, kg, rope)
        pc_ = _in_proj(hc, g1, mod_c[0], mod_c[1], w_in_l, qg, kg, None)
        dq_l, dk_l, dv_l, gq_l, gk_l, gv_l, nqkv_l, nz_l, na_l, nb_l = pl_
        dq_c, dk_c, dv_c, gq_c, gk_c, gv_c, nqkv_c, nz_c, na_c, nb_c = pc_

        d_l = _attention("diff", dq_l, dk_c, dv_c, dk_l, dv_l, (lam, dng), lambda_init)
        a_l = _attention("gqa", gq_l, gk_c, gv_c, gk_l, gv_l, ())

        fwd_c, bwd_c = _gdn_local(nqkv_c, na_c, nb_c, conv_w, alog, dtb)
        fwd_l, bwd_l = _gdn_local(nqkv_l, na_l, nb_l, conv_w, alog, dtb)
        ocf, ocb, scf, scb = _gdn_scan(fwd_c, bwd_c, zeros_state, zeros_state)
        olf, olb, _, _ = _gdn_scan(fwd_l, bwd_l, scf, scb)

        last = layer == depth - 1
        fg = final_norm_g.reshape(1, d)
        h = _mix_ffn(h, d_l, a_l, olf, olb, nz_l, nng, mod_l[2], w_out_l,
                     g2, mod_l[3], mod_l[4], mod_l[5], w_gu_l, w_down_l, fg, last)
        if need_ctx:
            d_c = _attention("diff", dq_c, dk_c, dv_c, None, None, (lam, dng), lambda_init)
            a_c = _attention("gqa", gq_c, gk_c, gv_c, None, None, ())
            hc = _mix_ffn(hc, d_c, a_c, ocf, ocb, nz_c, nng, mod_c[2], w_out_l,
                          g2, mod_c[3], mod_c[4], mod_c[5], w_gu_l, w_down_l, fg, False)
    return h
```

```python
import functools
import math

import jax
import jax.numpy as jnp
from jax import lax
from jax.experimental import pallas as pl
from jax.experimental.pallas import tpu as pltpu

F32 = jnp.float32
BF16 = jnp.bfloat16

LANES = 128
SUBLANES = 8
D_MODEL = 1024
GRID_W = 64
ROPE_THETA = 10000.0
NORM_EPS = 1e-6
L2_EPS = 1e-6
LOG2E = 1.4426950408889634

DIFF_HEADS = 4
DIFF_QK_DIM = 32
DIFF_V_DIM = 64
DIFF_WIDTH = DIFF_HEADS * DIFF_V_DIM
GQA_Q_HEADS = 6
GQA_KV_HEADS = 2
GQA_HEAD_DIM = 64
GQA_WIDTH = GQA_Q_HEADS * GQA_HEAD_DIM
GQA_KV_WIDTH = GQA_KV_HEADS * GQA_HEAD_DIM
GDN_HEADS = 6
GDN_HEAD_DIM = 64
GDN_WIDTH = GDN_HEADS * GDN_HEAD_DIM
GDN_CONV = 5
GDN_CHUNK = 64
GDN_PAIRS = GDN_HEADS // 2
FFN_HIDDEN = 2816

OFF_DQ = 0
OFF_DK = OFF_DQ + DIFF_WIDTH
OFF_DV = OFF_DK + DIFF_WIDTH
OFF_GQ = OFF_DV + DIFF_WIDTH
OFF_GK = OFF_GQ + GQA_WIDTH
OFF_GV = OFF_GK + GQA_KV_WIDTH
OFF_NQKV = OFF_GV + GQA_KV_WIDTH
OFF_NZ = OFF_NQKV + 3 * GDN_WIDTH
OFF_NA = OFF_NZ + GDN_WIDTH
OFF_NB = OFF_NA + LANES
IN_PAD = OFF_NB + LANES

GQA_Q_ORDER = (0, 3, 1, 4, 2, 5)
HEAD_LANES = 64
HEAD_V = HEAD_LANES
ONES_ROWS = 16
V_ROWS = HEAD_V + ONES_ROWS

COND_ROWS = 16
ADA_COL_TILE = 1536
GDN_INVERSE_BASE = 8
FFN_CHUNK = 512
FFN_SKEW = 1
GDN_SCAN_BATCH = 4
GDN_CHUNKS_PER_ITER = 4

ATTN_SKEW = {"diff": 3, "gqa": 4}
ATTN_CHUNKS_PER_ITER = 16
ATTN_KEY_CHUNK = 512
ATTN_QUERY_TILE = 256
ATTN_SUBTILES = {"diff": 2, "gqa": 1}
IN_PROJ_ROWS = 512
MIX_FFN_ROWS = 512
GDN_BLOCK_ROWS = 256

VMEM_LIMIT = 56 * 1024 * 1024


def _cparams(sem):
    return pltpu.CompilerParams(dimension_semantics=sem, vmem_limit_bytes=VMEM_LIMIT)


def _dot(a, b):
    return jnp.dot(a, b, preferred_element_type=F32)


def _dot_nt(a, b):
    return lax.dot_general(a, b, (((1,), (1,)), ((), ())), preferred_element_type=F32)


def _split_bf16(x):
    hi = x.astype(BF16)
    lo = (x - hi.astype(F32)).astype(BF16)
    return hi, lo


def _dot_rounded(a, b):
    return _dot(a.astype(BF16), b.astype(BF16))


def _group_matrix(scale):
    r = lax.broadcasted_iota(jnp.int32, (LANES, LANES), 0) // HEAD_LANES
    c = lax.broadcasted_iota(jnp.int32, (LANES, LANES), 1) // HEAD_LANES
    return jnp.where(r == c, scale, 0.0).astype(BF16)


def _group_sum(x, gm):
    hi, lo = _split_bf16(x)
    return _dot(hi, gm) + _dot(lo, gm)


def _rope(x, c, sa, sb, half):
    return x * c + pltpu.roll(x, LANES - half, 1) * sa + pltpu.roll(x, half, 1) * sb


def _ada_kernel(c_ref, w_ref, b_ref, o_ref):
    s = jax.nn.silu(c_ref[...]).astype(BF16)
    o_ref[...] = _dot(s, w_ref[...].astype(BF16)) + b_ref[...]


def _ada(cond, ada_w, ada_b):
    depth, d, n = ada_w.shape
    rows = cond.shape[0]
    tn = ADA_COL_TILE
    return pl.pallas_call(
        _ada_kernel,
        grid=(depth, n // tn),
        in_specs=[
            pl.BlockSpec((rows, d), lambda l, j: (0, 0)),
            pl.BlockSpec((None, d, tn), lambda l, j: (l, 0, j)),
            pl.BlockSpec((None, 1, tn), lambda l, j: (l, 0, j)),
        ],
        out_specs=pl.BlockSpec((None, rows, tn), lambda l, j: (l, 0, j)),
        out_shape=jax.ShapeDtypeStruct((depth, rows, n), F32),
        compiler_params=_cparams(("parallel", "parallel")),
        name="ada_mod",
    )(cond, ada_w, ada_b.reshape(depth, 1, n))


def _store_values(ref, first_head, v):
    vt = v.T.astype(BF16)
    n_chunks, _, tk = ref.shape
    ones = jnp.ones((ONES_ROWS, tk), BF16)
    for c in range(n_chunks):
        for s in range(2):
            r0 = (first_head + s) * V_ROWS
            ref[c, r0:r0 + HEAD_V, :] = vt[s * HEAD_V:(s + 1) * HEAD_V, c * tk:(c + 1) * tk]
            ref[c, r0 + HEAD_V:r0 + V_ROWS, :] = ones


def _in_proj_kernel(*refs, use_rope):
    if use_rope:
        (x_ref, g_ref, sh_ref, sc_ref, w_ref, qg_ref, kg_ref,
         cd_ref, sad_ref, sbd_ref, cg_ref, sag_ref, sbg_ref,
         dq_ref, dk_ref, dv_ref, gq_ref, gk_ref, gv_ref, nqkv_ref, nz_ref, na_ref, nb_ref) = refs
    else:
        (x_ref, g_ref, sh_ref, sc_ref, w_ref, qg_ref, kg_ref,
         dq_ref, dk_ref, dv_ref, gq_ref, gk_ref, gv_ref, nqkv_ref, nz_ref, na_ref, nb_ref) = refs
    x = x_ref[...]
    y = x * lax.rsqrt(jnp.mean(x * x, axis=-1, keepdims=True) + NORM_EPS) * g_ref[...]
    a = (y * (1.0 + sc_ref[...]) + sh_ref[...]).astype(BF16)

    bounds = (OFF_DQ, OFF_DK, OFF_DV, OFF_GQ, OFF_GK, OFF_NQKV, OFF_NZ, OFF_NA, IN_PAD)
    groups = {}

    def proj(lo, width):
        g0 = max(b for b in bounds if b <= lo)
        g1 = min(b for b in bounds if b > lo)
        assert lo + width <= g1
        if g0 not in groups:
            groups[g0] = _dot(a, w_ref[:, g0:g1])
        return groups[g0][:, lo - g0:lo - g0 + width]

    gm = _group_matrix(1.0 / GQA_HEAD_DIM)

    for off, out, scale in ((OFF_DQ, dq_ref, DIFF_QK_DIM ** -0.5 * LOG2E), (OFF_DK, dk_ref, 1.0)):
        for blk in range(DIFF_WIDTH // LANES):
            p = proj(off + blk * LANES, LANES)
            if use_rope:
                p = _rope(p, cd_ref[...], sad_ref[...], sbd_ref[...], DIFF_QK_DIM // 4)
            out[:, blk * LANES:(blk + 1) * LANES] = (p * scale).astype(BF16)
    for blk in range(DIFF_WIDTH // LANES):
        _store_values(dv_ref, 2 * blk, proj(OFF_DV + blk * LANES, LANES))

    def qk_prep(p, gain, scale):
        ms = _group_sum(p * p, gm)
        p = p * lax.rsqrt(ms + NORM_EPS) * gain
        if use_rope:
            p = _rope(p, cg_ref[...], sag_ref[...], sbg_ref[...], GQA_HEAD_DIM // 4)
        return (p * scale).astype(BF16)

    for blk in range(GQA_WIDTH // LANES):
        p = proj(OFF_GQ + blk * LANES, LANES)
        gq_ref[:, blk * LANES:(blk + 1) * LANES] = qk_prep(p, qg_ref[...], GQA_HEAD_DIM ** -0.5 * LOG2E)
    gk_ref[...] = qk_prep(proj(OFF_GK, LANES), kg_ref[...], 1.0)
    _store_values(gv_ref, 0, proj(OFF_GV, LANES))

    for part in range(3):
        nqkv_ref[:, part * GDN_WIDTH:(part + 1) * GDN_WIDTH] = proj(OFF_NQKV + part * GDN_WIDTH, GDN_WIDTH)
    nz_ref[...] = proj(OFF_NZ, GDN_WIDTH)
    na_ref[...] = proj(OFF_NA, LANES)
    nb_ref[...] = proj(OFF_NB, LANES)


def _in_proj(x, g, sh, sc, w, qg, kg, rope):
    b, t, d = x.shape
    tm = min(IN_PROJ_ROWS, t)
    nb = sh.shape[0]
    mod_map = (lambda i, j: (i, 0, 0)) if nb == b else (lambda i, j: (0, 0, 0))
    tok = lambda width: pl.BlockSpec((None, tm, width), lambda i, j: (i, j, 0))
    const2 = lambda shape: pl.BlockSpec(shape, lambda i, j: (0, 0))
    in_specs = [
        tok(d), const2((1, d)),
        pl.BlockSpec((None, 1, d), mod_map), pl.BlockSpec((None, 1, d), mod_map),
        const2((d, IN_PAD)), const2((1, LANES)), const2((1, LANES)),
    ]
    args = [x, g, sh, sc, w, qg, kg]
    if rope is not None:
        in_specs += [pl.BlockSpec((tm, LANES), lambda i, j: (j, 0))] * 6
        args += list(rope)
    widths = (DIFF_WIDTH, DIFF_WIDTH, DIFF_HEADS * V_ROWS, GQA_WIDTH, GQA_KV_WIDTH, GQA_KV_HEADS * V_ROWS,
              3 * GDN_WIDTH, GDN_WIDTH, LANES, LANES)
    dtypes = (BF16,) * 6 + (F32,) * 4
    transposed = (2, 5)
    tk = min(ATTN_KEY_CHUNK, tm)
    out_specs = [pl.BlockSpec((None, tm // tk, wd, tk), lambda i, j: (i, j, 0, 0)) if k in transposed else tok(wd)
                 for k, wd in enumerate(widths)]
    out_shape = [jax.ShapeDtypeStruct((b, t // tk, wd, tk) if k in transposed else (b, t, wd), dt)
                 for k, (wd, dt) in enumerate(zip(widths, dtypes))]
    return pl.pallas_call(
        functools.partial(_in_proj_kernel, use_rope=rope is not None),
        grid=(b, t // tm),
        in_specs=in_specs,
        out_specs=out_specs,
        out_shape=out_shape,
        compiler_params=_cparams(("parallel", "parallel")),
        name="in_proj",
    )(*args)


def _flash_all(cols, vrows, skew, qm_s, m_s, acc_s, kc_ref, vc_ref, kl_ref, vl_ref, n_lat):
    nh = len(cols)
    m_s[...] = jnp.full(m_s.shape, -jnp.inf, F32)
    acc_s[...] = jnp.zeros(acc_s.shape, F32)

    def pipeline(chunk_list):
        items = [(ck, h) for ck in chunk_list for h in range(nh)]

        def scores(item):
            (k_ref, _, tk, j), h = item
            if tk is None:
                rows = slice(None)
            elif isinstance(j, int):
                rows = slice(j * tk, (j + 1) * tk)
            else:
                rows = pl.ds(pl.multiple_of(j * tk, tk), tk)
            return _dot(k_ref[rows, cols[h]:cols[h] + LANES], qm_s[h])

        pending = [scores(it) for it in items[:skew]]
        for n, ((_, v_ref, _, j), h) in enumerate(items):
            s = pending.pop(0)
            if n + skew < len(items):
                pending.append(scores(items[n + skew]))
            m_old = m_s[h]
            m_new = jnp.maximum(m_old, jnp.max(s, axis=0, keepdims=True))
            alpha = jnp.exp2(m_old - m_new)
            p = jnp.exp2(s - m_new)
            acc_s[h] = alpha * acc_s[h] + _dot(v_ref[j, vrows[h]:vrows[h] + V_ROWS, :], p.astype(BF16))
            m_s[h] = m_new

    ctx_chunk = (kc_ref, vc_ref, None, 0)
    if not n_lat:
        pipeline([ctx_chunk])
        return
    tk = vl_ref.shape[-1]
    per_iter = math.gcd(ATTN_CHUNKS_PER_ITER, n_lat)
    if per_iter == n_lat:
        pipeline([ctx_chunk] + [(kl_ref, vl_ref, tk, j) for j in range(n_lat)])
        return
    pipeline([ctx_chunk])

    def body(i, carry):
        pipeline([(kl_ref, vl_ref, tk, i * per_iter + jj) for jj in range(per_iter)])
        return carry

    lax.fori_loop(0, n_lat // per_iter, body, 0)


def _transposed_q(q_ref, rows, blk):
    return q_ref[rows, blk * LANES:(blk + 1) * LANES].astype(F32).T


def _normalised(acc_s, h):
    acc = acc_s[h]
    return acc[0:HEAD_V] / acc[HEAD_V:HEAD_V + 1]


def _pair_out(oa, ob):
    return jnp.concatenate([oa, ob], axis=0).T


def _gqa_kernel(*refs, n_lat):
    if n_lat:
        q_ref, kc_ref, vc_ref, kl_ref, vl_ref, o_ref, qm_s, m_s, acc_s = refs
    else:
        q_ref, kc_ref, vc_ref, o_ref, qm_s, m_s, acc_s = refs
        kl_ref = vl_ref = None
    tq = qm_s.shape[-1]
    nsub = q_ref.shape[0] // tq
    nblk = GQA_WIDTH // LANES
    nh = 2 * nblk
    lo = lax.broadcasted_iota(jnp.int32, (LANES, tq), 0) < GQA_HEAD_DIM
    for sub in range(nsub):
        rows = slice(sub * tq, (sub + 1) * tq)
        for blk in range(nblk):
            qt = _transposed_q(q_ref, rows, blk)
            qm_s[sub * nh + 2 * blk] = jnp.where(lo, qt, 0.0).astype(BF16)
            qm_s[sub * nh + 2 * blk + 1] = jnp.where(lo, 0.0, qt).astype(BF16)
    _flash_all((0,) * (nh * nsub), (0, V_ROWS) * (nblk * nsub), ATTN_SKEW["gqa"], qm_s, m_s, acc_s,
               kc_ref, vc_ref, kl_ref, vl_ref, n_lat)
    for sub in range(nsub):
        for blk in range(nblk):
            h0 = sub * nh + 2 * blk
            o = _pair_out(_normalised(acc_s, h0), _normalised(acc_s, h0 + 1))
            o_ref[sub * tq:(sub + 1) * tq, blk * LANES:(blk + 1) * LANES] = o.astype(BF16)


def _diff_kernel(*refs, n_lat, lambda_init):
    if n_lat:
        q_ref, kc_ref, vc_ref, kl_ref, vl_ref, lam_ref, ng_ref, o_ref, qm_s, m_s, acc_s = refs
    else:
        q_ref, kc_ref, vc_ref, lam_ref, ng_ref, o_ref, qm_s, m_s, acc_s = refs
        kl_ref = vl_ref = None
    tq = qm_s.shape[-1]
    nsub = q_ref.shape[0] // tq
    nblk = DIFF_WIDTH // LANES
    nh = 4 * nblk
    lf = lam_ref[...]
    lam = (jnp.exp(jnp.sum(lf[0:1] * lf[1:2], axis=-1, keepdims=True))
           - jnp.exp(jnp.sum(lf[2:3] * lf[3:4], axis=-1, keepdims=True)) + lambda_init)
    row = lax.broadcasted_iota(jnp.int32, (LANES, tq), 0)
    gm = _group_matrix(1.0 / DIFF_V_DIM)
    for sub in range(nsub):
        for blk in range(nblk):
            qt = _transposed_q(q_ref, slice(sub * tq, (sub + 1) * tq), blk)
            for sc in range(4):
                qm_s[sub * nh + 4 * blk + sc] = jnp.where((row // DIFF_QK_DIM) == sc, qt, 0.0).astype(BF16)
    cols = tuple(blk * LANES for blk in range(nblk) for _ in range(4)) * nsub
    vrows = tuple((2 * blk + s) * V_ROWS for blk in range(nblk) for s in range(2) for _ in range(2)) * nsub
    _flash_all(cols, vrows, ATTN_SKEW["diff"], qm_s, m_s, acc_s, kc_ref, vc_ref, kl_ref, vl_ref, n_lat)
    for sub in range(nsub):
        for blk in range(nblk):
            halves = []
            for s in range(2):
                i0 = sub * nh + 4 * blk + 2 * s
                halves.append(_normalised(acc_s, i0) - lam * _normalised(acc_s, i0 + 1))
            o = _pair_out(halves[0], halves[1])
            ms = _group_sum(o * o, gm)
            o = o * lax.rsqrt(ms + NORM_EPS) * ng_ref[...] * (1.0 - lambda_init)
            o_ref[sub * tq:(sub + 1) * tq, blk * LANES:(blk + 1) * LANES] = o.astype(BF16)


def _attention(kind, q, kc, vc, kl, vl, extra, lambda_init=None):
    b, t, w = q.shape
    tq = min(ATTN_QUERY_TILE, t)
    nsub = math.gcd(ATTN_SUBTILES[kind], t // tq)
    rows = tq * nsub
    full = lambda a: pl.BlockSpec((None,) + a.shape[1:], lambda i, j: (i,) + (0,) * (a.ndim - 1))
    in_specs = [pl.BlockSpec((None, rows, w), lambda i, j: (i, j, 0)), full(kc), full(vc)]
    args = [q, kc, vc]
    n_lat = 0
    if kl is not None:
        n_lat = vl.shape[1]
        in_specs += [full(kl), full(vl)]
        args += [kl, vl]
    for e in extra:
        in_specs.append(pl.BlockSpec(e.shape, lambda i, j: (0, 0)))
        args.append(e)
    if kind == "gqa":
        body = functools.partial(_gqa_kernel, n_lat=n_lat)
        nh = GQA_Q_HEADS * nsub
    else:
        body = functools.partial(_diff_kernel, n_lat=n_lat, lambda_init=lambda_init)
        nh = 2 * DIFF_HEADS * nsub
    return pl.pallas_call(
        body,
        grid=(b, t // rows),
        in_specs=in_specs,
        out_specs=pl.BlockSpec((None, rows, w), lambda i, j: (i, j, 0)),
        out_shape=jax.ShapeDtypeStruct((b, t, w), BF16),
        scratch_shapes=[pltpu.VMEM((nh, LANES, tq), BF16), pltpu.VMEM((nh, 1, tq), F32),
                        pltpu.VMEM((nh, V_ROWS, tq), F32)],
        compiler_params=_cparams(("parallel", "parallel")),
        name=kind + "_attn",
    )(*args)


def _gdn_local_kernel(x_ref, xp_ref, xn_ref, cw_ref, na_ref, nb_ref, alog_ref, dtb_ref,
                      uf_ref, wf_ref, qdf_ref, kdf_ref, qkf_ref, glf_ref,
                      ub_ref, wb_ref, qdb_ref, kdb_ref, qkb_ref, glb_ref,
                      xbuf, q_s, k_s, v_s, g_s, b_s):
    tm = x_ref.shape[0]
    cs = GDN_CHUNK
    i = pl.program_id(1)
    nblk = pl.num_programs(1)
    halo = xp_ref.shape[0]
    xbuf[halo:halo + tm, :] = x_ref[...]
    xbuf[0:halo, :] = jnp.where(i > 0, xp_ref[...], 0.0)
    xbuf[halo + tm:2 * halo + tm, :] = jnp.where(i < nblk - 1, xn_ref[...], 0.0)
    gm = _group_matrix(1.0)
    for part, dst in enumerate((q_s, k_s, v_s)):
        cols = slice(part * GDN_WIDTH, (part + 1) * GDN_WIDTH)
        acc = None
        xall = xbuf[:, cols]
        for j in range(GDN_CONV):
            d = j - GDN_CONV // 2
            xs = xall if d == 0 else pltpu.roll(xall, (-d) % (tm + 2 * halo), 0)
            term = xs[halo:halo + tm] * cw_ref[j:j + 1, cols]
            acc = term if acc is None else acc + term
        y = jax.nn.silu(acc)
        if part < 2:
            scale = GDN_HEAD_DIM ** -0.5 if part == 0 else 1.0
            for blk in range(GDN_PAIRS):
                yb = y[:, blk * LANES:(blk + 1) * LANES]
                ss = _group_sum(yb * yb, gm)
                dst[:, blk * LANES:(blk + 1) * LANES] = yb * lax.rsqrt(ss + L2_EPS) * scale
        else:
            dst[...] = y
    xa = na_ref[...] + dtb_ref[...]
    softplus = jnp.maximum(xa, 0.0) + jnp.log(1.0 + jnp.exp(-jnp.abs(xa)))
    g_s[...] = -jnp.exp(alog_ref[...]) * softplus
    b_s[...] = jax.nn.sigmoid(nb_ref[...])

    ri = lax.broadcasted_iota(jnp.int32, (cs, LANES), 0)
    li = lax.broadcasted_iota(jnp.int32, (cs, LANES), 1)
    lo = li < HEAD_LANES
    tj = li % HEAD_LANES
    r64 = lax.broadcasted_iota(jnp.int32, (cs, cs), 0)
    c64 = lax.broadcasted_iota(jnp.int32, (cs, cs), 1)
    tri_lo = jnp.where(r64 >= c64, 1.0, 0.0).astype(BF16)
    tri_up = jnp.where(r64 <= c64, 1.0, 0.0).astype(BF16)
    r128 = lax.broadcasted_iota(jnp.int32, (LANES, LANES), 0)
    c128 = lax.broadcasted_iota(jnp.int32, (LANES, LANES), 1)
    eye = jnp.where(r128 == c128, 1.0, 0.0)
    base = GDN_INVERSE_BASE
    assert base == 8, "the finite product below is written for 8x8 diagonal blocks (D^8 = 0)"
    same_base = (r128 // base) == (c128 // base)
    level_sizes = [base * 2 ** k for k in range(int(math.log2(cs // base)))]
    level_masks = [((r128 // (2 * m)) == (c128 // (2 * m))) & ((r128 // m) != (c128 // m)) for m in level_sizes]
    lo1 = lax.broadcasted_iota(jnp.int32, (1, LANES), 1) < HEAD_LANES

    def stack(x, zero):
        return jnp.concatenate([jnp.where(lo, x, zero), jnp.where(lo, zero, x)], axis=0)

    dirs = ((uf_ref, wf_ref, qdf_ref, kdf_ref, qkf_ref, glf_ref),
            (ub_ref, wb_ref, qdb_ref, kdb_ref, qkb_ref, glb_ref))

    def setup(c, chains):
        r0 = pl.multiple_of(c * cs, cs)
        rows = pl.ds(r0, cs)
        g = g_s[rows, :]
        be = b_s[rows, :]
        gh = g.astype(BF16)
        r1 = g - gh.astype(F32)
        gmid = r1.astype(BF16)
        glo = (r1 - gmid.astype(F32)).astype(BF16)
        cum_f = _dot(tri_lo, gh) + _dot(tri_lo, gmid) + _dot(tri_lo, glo)
        cum_b = _dot(tri_up, gh) + _dot(tri_up, gmid) + _dot(tri_up, glo)
        gc = jnp.where(li < GDN_HEADS, cum_f, cum_b)
        gt = jnp.concatenate([gc, gc], axis=0).T
        for p in range(GDN_PAIRS):
            cols = slice(p * LANES, (p + 1) * LANES)
            q128 = q_s[rows, cols]
            k128 = k_s[rows, cols]
            v128 = v_s[rows, cols]
            kb = k128.astype(BF16)
            kstack = stack(kb, jnp.zeros_like(kb))
            kk = _dot_nt(kb, kstack)
            qk = _dot_nt(q128.astype(BF16), kstack)
            for rev in range(2):
                la = rev * GDN_HEADS + 2 * p
                gca, gcb = gc[:, la:la + 1], gc[:, la + 1:la + 2]
                bca, bcb = be[:, la:la + 1], be[:, la + 1:la + 2]
                gcol = jnp.where(lo, gca, gcb)
                grow = jnp.where(lo1, gt[la:la + 1, :], gt[la + 1:la + 2, :])
                bcol = jnp.where(lo, bca, bcb)
                if rev:
                    incl, strict = ri <= tj, ri < tj
                    last = 0
                else:
                    incl, strict = ri >= tj, ri > tj
                    last = cs - 1
                decay = jnp.exp(jnp.where(incl, gcol - grow, -jnp.inf))
                a128 = jnp.where(strict, kk * decay * bcol, 0.0)
                n = -stack(a128, 0.0)
                ea, eb = jnp.exp(gca), jnp.exp(gcb)
                rhs = jnp.concatenate([
                    jnp.concatenate([v128 * bca, v128 * bcb], axis=0),
                    jnp.concatenate([k128 * (bca * ea), k128 * (bcb * eb)], axis=0)], axis=1)
                glast = jnp.where(lo1, gc[last:last + 1, la:la + 1], gc[last:last + 1, la + 1:la + 2])
                u_ref, w_ref, qd_ref, kd_ref, qk_ref, gl_ref = dirs[rev]
                qd_ref[rows, cols] = (q128 * jnp.exp(gcol)).astype(BF16)
                kd_ref[rows, cols] = (k128 * jnp.exp(glast - gcol)).astype(BF16)
                qk_ref[rows, cols] = (qk * decay).astype(BF16)
                gl_ref[c, :, cols] = jnp.broadcast_to(jnp.exp(glast), (SUBLANES, LANES))
                chains.append(dict(n=n, rhs=rhs, rows=rows, cols=cols, rev=rev))

    def group(i, carry):
        chains = []
        for gi in range(per_iter):
            setup(i * per_iter + gi, chains)
        for ch in chains:
            d0 = jnp.where(same_base, ch["n"], 0.0)
            ch["t"] = eye + d0
            ch["pw"] = _dot_rounded(d0, d0)
        for ch in chains:
            both = _dot_rounded(jnp.concatenate([ch["t"], ch["pw"]], axis=0), ch["pw"])
            ch["t"] = ch["t"] + both[0:LANES]
            ch["pw"] = both[LANES:2 * LANES]
        for ch in chains:
            ch["t"] = ch["t"] + _dot_rounded(ch["t"], ch["pw"])
        for off_mask in level_masks:
            for ch in chains:
                ch["x"] = _dot_rounded(jnp.where(off_mask, ch["n"], 0.0), ch["t"])
            for ch in chains:
                ch["t"] = ch["t"] + _dot_rounded(ch["t"], ch["x"])
        for ch in chains:
            sol = _dot_rounded(ch["t"], ch["rhs"])
            u_ref, w_ref = dirs[ch["rev"]][0:2]
            u_ref[ch["rows"], ch["cols"]] = jnp.where(lo, sol[0:cs, 0:LANES], sol[cs:2 * cs, 0:LANES])
            w_ref[ch["rows"], ch["cols"]] = jnp.where(lo, sol[0:cs, LANES:], sol[cs:2 * cs, LANES:]).astype(BF16)
        return carry

    per_iter = math.gcd(GDN_CHUNKS_PER_ITER, tm // cs)
    lax.fori_loop(0, tm // (cs * per_iter), group, 0)


def _gdn_local(nqkv, na, nb, conv_w, alog, dtb):
    b, t, w3 = nqkv.shape
    tm = min(GDN_BLOCK_ROWS, t)
    nblk = t // tm
    cpb = tm // GDN_CHUNK
    halo = SUBLANES
    hb = tm // halo
    tok = lambda width: pl.BlockSpec((None, tm, width), lambda i, j: (i, j, 0))
    const2 = lambda shape: pl.BlockSpec(shape, lambda i, j: (0, 0))
    in_specs = [
        tok(w3),
        pl.BlockSpec((None, halo, w3), lambda i, j: (i, jnp.maximum(j * hb - 1, 0), 0)),
        pl.BlockSpec((None, halo, w3), lambda i, j: (i, jnp.minimum((j + 1) * hb, t // halo - 1), 0)),
        const2((GDN_CONV, w3)), tok(LANES), tok(LANES), const2((1, LANES)), const2((1, LANES)),
    ]
    per_dir_specs = [tok(GDN_WIDTH)] * 5 + [pl.BlockSpec((None, cpb, SUBLANES, GDN_WIDTH), lambda i, j: (i, j, 0, 0))]
    per_dir_shapes = ([jax.ShapeDtypeStruct((b, t, GDN_WIDTH), F32)]
                      + [jax.ShapeDtypeStruct((b, t, GDN_WIDTH), BF16)] * 4
                      + [jax.ShapeDtypeStruct((b, t // GDN_CHUNK, SUBLANES, GDN_WIDTH), F32)])
    outs = pl.pallas_call(
        _gdn_local_kernel,
        grid=(b, nblk),
        in_specs=in_specs,
        out_specs=per_dir_specs * 2,
        out_shape=per_dir_shapes * 2,
        scratch_shapes=[
            pltpu.VMEM((tm + 2 * halo, w3), F32),
            pltpu.VMEM((tm, GDN_WIDTH), F32), pltpu.VMEM((tm, GDN_WIDTH), F32), pltpu.VMEM((tm, GDN_WIDTH), F32),
            pltpu.VMEM((tm, LANES), F32), pltpu.VMEM((tm, LANES), F32),
        ],
        compiler_params=_cparams(("parallel", "parallel")),
        name="gdn_local",
    )(nqkv, nqkv, nqkv, conv_w, na, nb, alog, dtb)
    return outs[:6], outs[6:]


def _gdn_scan_kernel(uf_ref, wf_ref, qdf_ref, kdf_ref, qkf_ref, glf_ref,
                     ub_ref, wb_ref, qdb_ref, kdb_ref, qkb_ref, glb_ref,
                     s0f_ref, s0b_ref,
                     of_ref, ob_ref, sff_ref, sfb_ref,
                     stf, stb):
    n = pl.program_id(1)
    cs = GDN_CHUNK
    nbat = uf_ref.shape[0]
    cpb = uf_ref.shape[1] // cs

    @pl.when(n == 0)
    def _():
        stf[...] = s0f_ref[...]
        stb[...] = s0b_ref[...]

    lo = lax.broadcasted_iota(jnp.int32, (cs, LANES), 1) < HEAD_LANES
    r128 = lax.broadcasted_iota(jnp.int32, (LANES, LANES), 0) // HEAD_LANES
    c128 = lax.broadcasted_iota(jnp.int32, (LANES, LANES), 1) // HEAD_LANES
    bd = r128 == c128

    fwd = (uf_ref, wf_ref, qdf_ref, kdf_ref, qkf_ref, glf_ref, of_ref, stf)
    bwd = (ub_ref, wb_ref, qdb_ref, kdb_ref, qkb_ref, glb_ref, ob_ref, stb)

    for cc in range(cpb):
        chains = []
        for c, (u_ref, w_ref, qd_ref, kd_ref, qk_ref, gl_ref, o_ref, st) in ((cc, fwd), (cpb - 1 - cc, bwd)):
            rows = slice(c * cs, (c + 1) * cs)
            for bi in range(nbat):
                for p in range(GDN_PAIRS):
                    cols = slice(p * LANES, (p + 1) * LANES)
                    s = st[bi, p]
                    sb = s.astype(BF16)
                    ws = _dot(w_ref[bi, rows, cols], sb)
                    qs = _dot(qd_ref[bi, rows, cols], sb)
                    chains.append((c, bi, rows, cols, p, s, ws, qs, u_ref, kd_ref, qk_ref, gl_ref, o_ref, st))
        for c, bi, rows, cols, p, s, ws, qs, u_ref, kd_ref, qk_ref, gl_ref, o_ref, st in chains:
            vnew = u_ref[bi, rows, cols] - ws
            vb = vnew.astype(BF16)
            zero = jnp.zeros_like(vb)
            vstack = jnp.concatenate([jnp.where(lo, vb, zero), jnp.where(lo, zero, vb)], axis=0)
            o_ref[bi, rows, cols] = qs + _dot(qk_ref[bi, rows, cols], vstack)
            upd = _dot(kd_ref[bi, rows, cols].astype(F32).T.astype(BF16), vb)
            st[bi, p] = s * gl_ref[bi, c, 0:1, cols] + jnp.where(bd, upd, 0.0)

    @pl.when(n == pl.num_programs(1) - 1)
    def _():
        sff_ref[...] = stf[...]
        sfb_ref[...] = stb[...]


def _gdn_scan(fwd, bwd, s0f, s0b):
    b, t, _ = fwd[0].shape
    tm = min(GDN_BLOCK_ROWS, t)
    nblk = t // tm
    cpb = tm // GDN_CHUNK
    bb = math.gcd(GDN_SCAN_BATCH, b)
    f_tok = pl.BlockSpec((bb, tm, GDN_WIDTH), lambda i, j: (i, j, 0))
    b_tok = pl.BlockSpec((bb, tm, GDN_WIDTH), lambda i, j: (i, nblk - 1 - j, 0))
    f_gl = pl.BlockSpec((bb, cpb, SUBLANES, GDN_WIDTH), lambda i, j: (i, j, 0, 0))
    b_gl = pl.BlockSpec((bb, cpb, SUBLANES, GDN_WIDTH), lambda i, j: (i, nblk - 1 - j, 0, 0))
    st_spec = pl.BlockSpec((bb, GDN_PAIRS, LANES, LANES), lambda i, j: (i, 0, 0, 0))
    st_shape = jax.ShapeDtypeStruct((b, GDN_PAIRS, LANES, LANES), F32)
    o_shape = jax.ShapeDtypeStruct((b, t, GDN_WIDTH), F32)
    return pl.pallas_call(
        _gdn_scan_kernel,
        grid=(b // bb, nblk),
        in_specs=[f_tok] * 5 + [f_gl] + [b_tok] * 5 + [b_gl] + [st_spec, st_spec],
        out_specs=[f_tok, b_tok, st_spec, st_spec],
        out_shape=[o_shape, o_shape, st_shape, st_shape],
        scratch_shapes=[pltpu.VMEM((bb, GDN_PAIRS, LANES, LANES), F32)] * 2,
        compiler_params=_cparams(("parallel", "arbitrary")),
        name="gdn_scan",
    )(*fwd, *bwd, s0f, s0b)


def _mix_ffn_kernel(h_ref, d_ref, gq_ref, of_ref, ob_ref, z_ref, ng_ref, gate1_ref, wo_ref,
                    g_ref, sh_ref, sc_ref, gate_ref, wgu_ref, wd_ref, fg_ref, o_ref, *, final_norm):
    gm = _group_matrix(1.0 / GDN_HEAD_DIM)
    pieces = [d_ref[...], gq_ref[...]]
    for blk in range(GDN_PAIRS):
        cols = slice(blk * LANES, (blk + 1) * LANES)
        o = of_ref[:, cols] + ob_ref[:, cols]
        ms = _group_sum(o * o, gm)
        r = o * lax.rsqrt(ms + NORM_EPS) * ng_ref[...] * jax.nn.silu(z_ref[:, cols])
        pieces.append(r.astype(BF16))
    x = h_ref[...] + gate1_ref[...] * _dot(jnp.concatenate(pieces, axis=1), wo_ref[...])

    y = x * lax.rsqrt(jnp.mean(x * x, axis=-1, keepdims=True) + NORM_EPS) * g_ref[...]
    a = (y * (1.0 + sc_ref[...]) + sh_ref[...]).astype(BF16)
    hidden = wd_ref.shape[0]
    starts = list(range(0, hidden, FFN_CHUNK))

    def gate_up(lo):
        width = min(FFN_CHUNK, hidden - lo)
        return _dot(a, wgu_ref[:, lo:lo + width]), _dot(a, wgu_ref[:, hidden + lo:hidden + lo + width]), width

    pending = [gate_up(lo) for lo in starts[:FFN_SKEW]]
    acc = None
    for n, lo in enumerate(starts):
        gate, up, width = pending.pop(0)
        if n + FFN_SKEW < len(starts):
            pending.append(gate_up(starts[n + FFN_SKEW]))
        act = (jax.nn.silu(gate) * up).astype(BF16)
        part = _dot(act, wd_ref[lo:lo + width, :])
        acc = part if acc is None else acc + part
    out = x + gate_ref[...] * acc
    if final_norm:
        out = out * lax.rsqrt(jnp.mean(out * out, axis=-1, keepdims=True) + NORM_EPS) * fg_ref[...]
    o_ref[...] = out


def _mix_ffn(h, d, gq, of, ob, z, ng, gate1, w_out, g, sh, sc, gate, w_gu, w_down, fg, final_norm):
    b, t, dm = h.shape
    tm = min(MIX_FFN_ROWS, t)
    nb = sh.shape[0]
    mod_map = (lambda i, k: (i, 0, 0)) if nb == b else (lambda i, k: (0, 0, 0))
    tok = lambda width: pl.BlockSpec((None, tm, width), lambda i, k: (i, k, 0))
    vec = lambda width: pl.BlockSpec((1, width), lambda i, k: (0, 0))
    mod = pl.BlockSpec((None, 1, dm), mod_map)
    resident = lambda w: pl.BlockSpec(w.shape, lambda i, k: (0, 0), pipeline_mode=pl.Buffered(1))
    return pl.pallas_call(
        functools.partial(_mix_ffn_kernel, final_norm=final_norm),
        grid=(b, t // tm),
        in_specs=[tok(dm), tok(DIFF_WIDTH), tok(GQA_WIDTH), tok(GDN_WIDTH), tok(GDN_WIDTH), tok(GDN_WIDTH),
                  vec(LANES), mod, resident(w_out),
                  vec(dm), mod, mod, mod, resident(w_gu), resident(w_down), vec(dm)],
        out_specs=tok(dm),
        out_shape=jax.ShapeDtypeStruct((b, t, dm), F32),
        compiler_params=_cparams(("parallel", "parallel")),
        name="mix_ffn",
    )(h, d, gq, of, ob, z, ng, gate1, w_out, g, sh, sc, gate, w_gu, w_down, fg)


def _rope_tables(t, rot_dim):
    nf = rot_dim // 4
    pos = jnp.arange(t)
    row = (pos // GRID_W).astype(F32)
    col = (pos % GRID_W).astype(F32)
    inv_freq = ROPE_THETA ** (-jnp.arange(nf, dtype=F32) / nf)
    ar, ac = row[:, None] * inv_freq, col[:, None] * inv_freq
    z = jnp.zeros_like(ar)
    cos = jnp.concatenate([jnp.cos(ar), jnp.cos(ar), jnp.cos(ac), jnp.cos(ac)], axis=1)
    sa = jnp.concatenate([-jnp.sin(ar), z, -jnp.sin(ac), z], axis=1)
    sb = jnp.concatenate([z, jnp.sin(ar), z, jnp.sin(ac)], axis=1)
    rep = LANES // rot_dim
    return tuple(jnp.tile(a, (1, rep)) for a in (cos, sa, sb))


def _pad_lanes(v):
    return jnp.pad(v.reshape(1, -1), ((0, 0), (0, LANES - v.size)))


def _relayout_w_in(w):
    o = 0
    pieces = []
    for width in (DIFF_WIDTH, DIFF_WIDTH, DIFF_WIDTH):
        pieces.append(w[:, o:o + width]); o += width
    gq = w[:, o:o + GQA_WIDTH].reshape(-1, GQA_Q_HEADS, GQA_HEAD_DIM); o += GQA_WIDTH
    pieces.append(jnp.take(gq, jnp.array(GQA_Q_ORDER), axis=1).reshape(-1, GQA_WIDTH))
    for width in (GQA_KV_WIDTH, GQA_KV_WIDTH, 3 * GDN_WIDTH, GDN_WIDTH):
        pieces.append(w[:, o:o + width]); o += width
    for width in (2 * GDN_HEADS, 2 * GDN_HEADS):
        pieces.append(jnp.pad(w[:, o:o + width], ((0, 0), (0, LANES - width)))); o += width
    return jnp.concatenate(pieces, axis=1).astype(BF16)


def _relayout_w_out(w):
    gq = w[DIFF_WIDTH:DIFF_WIDTH + GQA_WIDTH].reshape(GQA_Q_HEADS, GQA_HEAD_DIM, -1)
    gq = jnp.take(gq, jnp.array(GQA_Q_ORDER), axis=0).reshape(GQA_WIDTH, -1)
    return jnp.concatenate([w[:DIFF_WIDTH], gq, w[DIFF_WIDTH + GQA_WIDTH:]], axis=0).astype(BF16)


def kernel(x, c, ctx, c_ctx, norm1_g, ada_w, ada_b, w_in, diff_lambda, diff_norm_g, q_norm_g, k_norm_g,
           gdn_conv_w, gdn_a_log, gdn_dt_bias, gdn_norm_g, w_out, norm2_g, ffn_w_gu, ffn_w_down, final_norm_g):
    b, t, d = x.shape
    depth = w_in.shape[0]
    rope = _rope_tables(t, DIFF_QK_DIM) + _rope_tables(t, GQA_HEAD_DIM)

    cond = jnp.concatenate([c, c_ctx[None, :], jnp.zeros((COND_ROWS - b - 1, d), F32)], axis=0)
    mod = _ada(cond, ada_w, ada_b).reshape(depth, COND_ROWS, 6, d)

    tile2 = lambda v: jnp.tile(v.reshape(1, -1), (1, LANES // v.size))
    zeros_state = jnp.zeros((b, GDN_PAIRS, LANES, LANES), F32)

    h, hc = x, ctx
    for layer in range(depth):
        need_ctx = layer < depth - 1
        lambda_init = 0.8 - 0.6 * math.exp(-0.3 * layer)
        mod_l = [mod[layer, :b, k][:, None, :] for k in range(6)]
        mod_c = [mod[layer, b:b + 1, k][:, None, :] for k in range(6)]
        w_in_l = _relayout_w_in(w_in[layer])
        w_out_l = _relayout_w_out(w_out[layer])
        w_gu_l = ffn_w_gu[layer].astype(BF16)
        w_down_l = ffn_w_down[layer].astype(BF16)
        g1 = norm1_g[layer].reshape(1, d)
        g2 = norm2_g[layer].reshape(1, d)
        qg, kg = tile2(q_norm_g[layer]), tile2(k_norm_g[layer])
        dng, nng = tile2(diff_norm_g[layer]), tile2(gdn_norm_g[layer])
        alog, dtb = _pad_lanes(gdn_a_log[layer]), _pad_lanes(gdn_dt_bias[layer])
        lam = diff_lambda[layer]
        conv_w = gdn_conv_w[layer]

        pl_ = _in_proj(h, g1, mod_l[0], mod_l[1], w_in_l, qg, kg, rope)
        pc_ = _in_proj(hc, g1, mod_c[0], mod_c[1], w_in_l, qg, kg, None)
        dq_l, dk_l, dv_l, gq_l, gk_l, gv_l, nqkv_l, nz_l, na_l, nb_l = pl_
        dq_c, dk_c, dv_c, gq_c, gk_c, gv_c, nqkv_c, nz_c, na_c, nb_c = pc_

        d_l = _attention("diff", dq_l, dk_c, dv_c, dk_l, dv_l, (lam, dng), lambda_init)
        a_l = _attention("gqa", gq_l, gk_c, gv_c, gk_l, gv_l, ())

        fwd_c, bwd_c = _gdn_local(nqkv_c, na_c, nb_c, conv_w, alog, dtb)
        fwd_l, bwd_l = _gdn_local(nqkv_l, na_l, nb_l, conv_w, alog, dtb)
        ocf, ocb, scf, scb = _gdn_scan(fwd_c, bwd_c, zeros_state, zeros_state)
        olf, olb, _, _ = _gdn_scan(fwd_l, bwd_l, scf, scb)

        last = layer == depth - 1
        fg = final_norm_g.reshape(1, d)
        h = _mix_ffn(h, d_l, a_l, olf, olb, nz_l, nng, mod_l[2], w_out_l,
                     g2, mod_l[3], mod_l[4], mod_l[5], w_gu_l, w_down_l, fg, last)
        if need_ctx:
            d_c = _attention("diff", dq_c, dk_c, dv_c, None, None, (lam, dng), lambda_init)
            a_c = _attention("gqa", gq_c, gk_c, gv_c, None, None, ())
            hc = _mix_ffn(hc, d_c, a_c, ocf, ocb, nz_c, nng, mod_c[2], w_out_l,
                          g2, mod_c[3], mod_c[4], mod_c[5], w_gu_l, w_down_l, fg, False)
    return h
```

```python
import functools
import math

import jax
import jax.numpy as jnp
from jax import lax
from jax.experimental import pallas as pl
from jax.experimental.pallas import tpu as pltpu

F32 = jnp.float32
BF16 = jnp.bfloat16

LANES = 128
SUBLANES = 8
D_MODEL = 1024
GRID_W = 64
ROPE_THETA = 10000.0
NORM_EPS = 1e-6
L2_EPS = 1e-6
LOG2E = 1.4426950408889634

DIFF_HEADS = 4
DIFF_QK_DIM = 32
DIFF_V_DIM = 64
DIFF_WIDTH = DIFF_HEADS * DIFF_V_DIM
GQA_Q_HEADS = 6
GQA_KV_HEADS = 2
GQA_HEAD_DIM = 64
GQA_WIDTH = GQA_Q_HEADS * GQA_HEAD_DIM
GQA_KV_WIDTH = GQA_KV_HEADS * GQA_HEAD_DIM
GDN_HEADS = 6
GDN_HEAD_DIM = 64
GDN_WIDTH = GDN_HEADS * GDN_HEAD_DIM
GDN_CONV = 5
GDN_CHUNK = 64
GDN_PAIRS = GDN_HEADS // 2
FFN_HIDDEN = 2816

OFF_DQ = 0
OFF_DK = OFF_DQ + DIFF_WIDTH
OFF_DV = OFF_DK + DIFF_WIDTH
OFF_GQ = OFF_DV + DIFF_WIDTH
OFF_GK = OFF_GQ + GQA_WIDTH
OFF_GV = OFF_GK + GQA_KV_WIDTH
OFF_NQKV = OFF_GV + GQA_KV_WIDTH
OFF_NZ = OFF_NQKV + 3 * GDN_WIDTH
OFF_NA = OFF_NZ + GDN_WIDTH
OFF_NB = OFF_NA + LANES
IN_PAD = OFF_NB + LANES

GQA_Q_ORDER = (0, 3, 1, 4, 2, 5)
HEAD_LANES = 64
HEAD_V = HEAD_LANES
ONES_ROWS = 16
V_ROWS = HEAD_V + ONES_ROWS

COND_ROWS = 16
ADA_COL_TILE = 1536
GDN_INVERSE_BASE = 8
FFN_CHUNK = 512
FFN_SKEW = 1
GDN_SCAN_BATCH = 4
GDN_CHUNKS_PER_ITER = 2

ATTN_SKEW = {"diff": 3, "gqa": 4}
ATTN_CHUNKS_PER_ITER = 16
ATTN_KEY_CHUNK = 512
ATTN_QUERY_TILE = 256
ATTN_SUBTILES = {"diff": 2, "gqa": 1}
IN_PROJ_ROWS = 512
MIX_FFN_ROWS = 512
GDN_BLOCK_ROWS = 256

VMEM_LIMIT = 56 * 1024 * 1024


def _cparams(sem):
    return pltpu.CompilerParams(dimension_semantics=sem, vmem_limit_bytes=VMEM_LIMIT)


def _dot(a, b):
    return jnp.dot(a, b, preferred_element_type=F32)


def _dot_nt(a, b):
    return lax.dot_general(a, b, (((1,), (1,)), ((), ())), preferred_element_type=F32)


def _split_bf16(x):
    hi = x.astype(BF16)
    lo = (x - hi.astype(F32)).astype(BF16)
    return hi, lo


def _dot_rounded(a, b):
    return _dot(a.astype(BF16), b.astype(BF16))


def _group_matrix(scale):
    r = lax.broadcasted_iota(jnp.int32, (LANES, LANES), 0) // HEAD_LANES
    c = lax.broadcasted_iota(jnp.int32, (LANES, LANES), 1) // HEAD_LANES
    return jnp.where(r == c, scale, 0.0).astype(BF16)


def _group_sum(x, gm):
    hi, lo = _split_bf16(x)
    return _dot(hi, gm) + _dot(lo, gm)


def _rope(x, c, sa, sb, half):
    return x * c + pltpu.roll(x, LANES - half, 1) * sa + pltpu.roll(x, half, 1) * sb


def _ada_kernel(c_ref, w_ref, b_ref, o_ref):
    s = jax.nn.silu(c_ref[...]).astype(BF16)
    o_ref[...] = _dot(s, w_ref[...].astype(BF16)) + b_ref[...]


def _ada(cond, ada_w, ada_b):
    depth, d, n = ada_w.shape
    rows = cond.shape[0]
    tn = ADA_COL_TILE
    return pl.pallas_call(
        _ada_kernel,
        grid=(depth, n // tn),
        in_specs=[
            pl.BlockSpec((rows, d), lambda l, j: (0, 0)),
            pl.BlockSpec((None, d, tn), lambda l, j: (l, 0, j)),
            pl.BlockSpec((None, 1, tn), lambda l, j: (l, 0, j)),
        ],
        out_specs=pl.BlockSpec((None, rows, tn), lambda l, j: (l, 0, j)),
        out_shape=jax.ShapeDtypeStruct((depth, rows, n), F32),
        compiler_params=_cparams(("parallel", "parallel")),
        name="ada_mod",
    )(cond, ada_w, ada_b.reshape(depth, 1, n))


def _store_values(ref, first_head, v):
    vt = v.T.astype(BF16)
    n_chunks, _, tk = ref.shape
    ones = jnp.ones((ONES_ROWS, tk), BF16)
    for c in range(n_chunks):
        for s in range(2):
            r0 = (first_head + s) * V_ROWS
            ref[c, r0:r0 + HEAD_V, :] = vt[s * HEAD_V:(s + 1) * HEAD_V, c * tk:(c + 1) * tk]
            ref[c, r0 + HEAD_V:r0 + V_ROWS, :] = ones


def _in_proj_kernel(*refs, use_rope):
    if use_rope:
        (x_ref, g_ref, sh_ref, sc_ref, w_ref, qg_ref, kg_ref,
         cd_ref, sad_ref, sbd_ref, cg_ref, sag_ref, sbg_ref,
         dq_ref, dk_ref, dv_ref, gq_ref, gk_ref, gv_ref, nqkv_ref, nz_ref, na_ref, nb_ref) = refs
    else:
        (x_ref, g_ref, sh_ref, sc_ref, w_ref, qg_ref, kg_ref,
         dq_ref, dk_ref, dv_ref, gq_ref, gk_ref, gv_ref, nqkv_ref, nz_ref, na_ref, nb_ref) = refs
    x = x_ref[...]
    y = x * lax.rsqrt(jnp.mean(x * x, axis=-1, keepdims=True) + NORM_EPS) * g_ref[...]
    a = (y * (1.0 + sc_ref[...]) + sh_ref[...]).astype(BF16)

    bounds = (OFF_DQ, OFF_DK, OFF_DV, OFF_GQ, OFF_GK, OFF_NQKV, OFF_NZ, OFF_NA, IN_PAD)
    groups = {}

    def proj(lo, width):
        g0 = max(b for b in bounds if b <= lo)
        g1 = min(b for b in bounds if b > lo)
        assert lo + width <= g1
        if g0 not in groups:
            groups[g0] = _dot(a, w_ref[:, g0:g1])
        return groups[g0][:, lo - g0:lo - g0 + width]

    gm = _group_matrix(1.0 / GQA_HEAD_DIM)

    for off, out, scale in ((OFF_DQ, dq_ref, DIFF_QK_DIM ** -0.5 * LOG2E), (OFF_DK, dk_ref, 1.0)):
        for blk in range(DIFF_WIDTH // LANES):
            p = proj(off + blk * LANES, LANES)
            if use_rope:
                p = _rope(p, cd_ref[...], sad_ref[...], sbd_ref[...], DIFF_QK_DIM // 4)
            out[:, blk * LANES:(blk + 1) * LANES] = (p * scale).astype(BF16)
    for blk in range(DIFF_WIDTH // LANES):
        _store_values(dv_ref, 2 * blk, proj(OFF_DV + blk * LANES, LANES))

    def qk_prep(p, gain, scale):
        ms = _group_sum(p * p, gm)
        p = p * lax.rsqrt(ms + NORM_EPS) * gain
        if use_rope:
            p = _rope(p, cg_ref[...], sag_ref[...], sbg_ref[...], GQA_HEAD_DIM // 4)
        return (p * scale).astype(BF16)

    for blk in range(GQA_WIDTH // LANES):
        p = proj(OFF_GQ + blk * LANES, LANES)
        gq_ref[:, blk * LANES:(blk + 1) * LANES] = qk_prep(p, qg_ref[...], GQA_HEAD_DIM ** -0.5 * LOG2E)
    gk_ref[...] = qk_prep(proj(OFF_GK, LANES), kg_ref[...], 1.0)
    _store_values(gv_ref, 0, proj(OFF_GV, LANES))

    for part in range(3):
        nqkv_ref[:, part * GDN_WIDTH:(part + 1) * GDN_WIDTH] = proj(OFF_NQKV + part * GDN_WIDTH, GDN_WIDTH)
    nz_ref[...] = proj(OFF_NZ, GDN_WIDTH)
    na_ref[...] = proj(OFF_NA, LANES)
    nb_ref[...] = proj(OFF_NB, LANES)


def _in_proj(x, g, sh, sc, w, qg, kg, rope):
    b, t, d = x.shape
    tm = min(IN_PROJ_ROWS, t)
    nb = sh.shape[0]
    mod_map = (lambda i, j: (i, 0, 0)) if nb == b else (lambda i, j: (0, 0, 0))
    tok = lambda width: pl.BlockSpec((None, tm, width), lambda i, j: (i, j, 0))
    const2 = lambda shape: pl.BlockSpec(shape, lambda i, j: (0, 0))
    in_specs = [
        tok(d), const2((1, d)),
        pl.BlockSpec((None, 1, d), mod_map), pl.BlockSpec((None, 1, d), mod_map),
        const2((d, IN_PAD)), const2((1, LANES)), const2((1, LANES)),
    ]
    args = [x, g, sh, sc, w, qg, kg]
    if rope is not None:
        in_specs += [pl.BlockSpec((tm, LANES), lambda i, j: (j, 0))] * 6
        args += list(rope)
    widths = (DIFF_WIDTH, DIFF_WIDTH, DIFF_HEADS * V_ROWS, GQA_WIDTH, GQA_KV_WIDTH, GQA_KV_HEADS * V_ROWS,
              3 * GDN_WIDTH, GDN_WIDTH, LANES, LANES)
    dtypes = (BF16,) * 6 + (F32,) * 4
    transposed = (2, 5)
    tk = min(ATTN_KEY_CHUNK, tm)
    out_specs = [pl.BlockSpec((None, tm // tk, wd, tk), lambda i, j: (i, j, 0, 0)) if k in transposed else tok(wd)
                 for k, wd in enumerate(widths)]
    out_shape = [jax.ShapeDtypeStruct((b, t // tk, wd, tk) if k in transposed else (b, t, wd), dt)
                 for k, (wd, dt) in enumerate(zip(widths, dtypes))]
    return pl.pallas_call(
        functools.partial(_in_proj_kernel, use_rope=rope is not None),
        grid=(b, t // tm),
        in_specs=in_specs,
        out_specs=out_specs,
        out_shape=out_shape,
        compiler_params=_cparams(("parallel", "parallel")),
        name="in_proj",
    )(*args)


def _flash_all(cols, vrows, skew, qm_s, m_s, acc_s, kc_ref, vc_ref, kl_ref, vl_ref, n_lat):
    nh = len(cols)
    m_s[...] = jnp.full(m_s.shape, -jnp.inf, F32)
    acc_s[...] = jnp.zeros(acc_s.shape, F32)

    def pipeline(chunk_list):
        items = [(ck, h) for ck in chunk_list for h in range(nh)]

        def scores(item):
            (k_ref, _, tk, j), h = item
            if tk is None:
                rows = slice(None)
            elif isinstance(j, int):
                rows = slice(j * tk, (j + 1) * tk)
            else:
                rows = pl.ds(pl.multiple_of(j * tk, tk), tk)
            return _dot(k_ref[rows, cols[h]:cols[h] + LANES], qm_s[h])

        pending = [scores(it) for it in items[:skew]]
        for n, ((_, v_ref, _, j), h) in enumerate(items):
            s = pending.pop(0)
            if n + skew < len(items):
                pending.append(scores(items[n + skew]))
            m_old = m_s[h]
            m_new = jnp.maximum(m_old, jnp.max(s, axis=0, keepdims=True))
            alpha = jnp.exp2(m_old - m_new)
            p = jnp.exp2(s - m_new)
            acc_s[h] = alpha * acc_s[h] + _dot(v_ref[j, vrows[h]:vrows[h] + V_ROWS, :], p.astype(BF16))
            m_s[h] = m_new

    ctx_chunk = (kc_ref, vc_ref, None, 0)
    if not n_lat:
        pipeline([ctx_chunk])
        return
    tk = vl_ref.shape[-1]
    per_iter = math.gcd(ATTN_CHUNKS_PER_ITER, n_lat)
    if per_iter == n_lat:
        pipeline([ctx_chunk] + [(kl_ref, vl_ref, tk, j) for j in range(n_lat)])
        return
    pipeline([ctx_chunk])

    def body(i, carry):
        pipeline([(kl_ref, vl_ref, tk, i * per_iter + jj) for jj in range(per_iter)])
        return carry

    lax.fori_loop(0, n_lat // per_iter, body, 0)


def _transposed_q(q_ref, rows, blk):
    return q_ref[rows, blk * LANES:(blk + 1) * LANES].astype(F32).T


def _normalised(acc_s, h):
    acc = acc_s[h]
    return acc[0:HEAD_V] / acc[HEAD_V:HEAD_V + 1]


def _pair_out(oa, ob):
    return jnp.concatenate([oa, ob], axis=0).T


def _gqa_kernel(*refs, n_lat):
    if n_lat:
        q_ref, kc_ref, vc_ref, kl_ref, vl_ref, o_ref, qm_s, m_s, acc_s = refs
    else:
        q_ref, kc_ref, vc_ref, o_ref, qm_s, m_s, acc_s = refs
        kl_ref = vl_ref = None
    tq = qm_s.shape[-1]
    nsub = q_ref.shape[0] // tq
    nblk = GQA_WIDTH // LANES
    nh = 2 * nblk
    lo = lax.broadcasted_iota(jnp.int32, (LANES, tq), 0) < GQA_HEAD_DIM
    for sub in range(nsub):
        rows = slice(sub * tq, (sub + 1) * tq)
        for blk in range(nblk):
            qt = _transposed_q(q_ref, rows, blk)
            qm_s[sub * nh + 2 * blk] = jnp.where(lo, qt, 0.0).astype(BF16)
            qm_s[sub * nh + 2 * blk + 1] = jnp.where(lo, 0.0, qt).astype(BF16)
    _flash_all((0,) * (nh * nsub), (0, V_ROWS) * (nblk * nsub), ATTN_SKEW["gqa"], qm_s, m_s, acc_s,
               kc_ref, vc_ref, kl_ref, vl_ref, n_lat)
    for sub in range(nsub):
        for blk in range(nblk):
            h0 = sub * nh + 2 * blk
            o = _pair_out(_normalised(acc_s, h0), _normalised(acc_s, h0 + 1))
            o_ref[sub * tq:(sub + 1) * tq, blk * LANES:(blk + 1) * LANES] = o.astype(BF16)


def _diff_kernel(*refs, n_lat, lambda_init):
    if n_lat:
        q_ref, kc_ref, vc_ref, kl_ref, vl_ref, lam_ref, ng_ref, o_ref, qm_s, m_s, acc_s = refs
    else:
        q_ref, kc_ref, vc_ref, lam_ref, ng_ref, o_ref, qm_s, m_s, acc_s = refs
        kl_ref = vl_ref = None
    tq = qm_s.shape[-1]
    nsub = q_ref.shape[0] // tq
    nblk = DIFF_WIDTH // LANES
    nh = 4 * nblk
    lf = lam_ref[...]
    lam = (jnp.exp(jnp.sum(lf[0:1] * lf[1:2], axis=-1, keepdims=True))
           - jnp.exp(jnp.sum(lf[2:3] * lf[3:4], axis=-1, keepdims=True)) + lambda_init)
    row = lax.broadcasted_iota(jnp.int32, (LANES, tq), 0)
    gm = _group_matrix(1.0 / DIFF_V_DIM)
    for sub in range(nsub):
        for blk in range(nblk):
            qt = _transposed_q(q_ref, slice(sub * tq, (sub + 1) * tq), blk)
            for sc in range(4):
                qm_s[sub * nh + 4 * blk + sc] = jnp.where((row // DIFF_QK_DIM) == sc, qt, 0.0).astype(BF16)
    cols = tuple(blk * LANES for blk in range(nblk) for _ in range(4)) * nsub
    vrows = tuple((2 * blk + s) * V_ROWS for blk in range(nblk) for s in range(2) for _ in range(2)) * nsub
    _flash_all(cols, vrows, ATTN_SKEW["diff"], qm_s, m_s, acc_s, kc_ref, vc_ref, kl_ref, vl_ref, n_lat)
    for sub in range(nsub):
        for blk in range(nblk):
            halves = []
            for s in range(2):
                i0 = sub * nh + 4 * blk + 2 * s
                halves.append(_normalised(acc_s, i0) - lam * _normalised(acc_s, i0 + 1))
            o = _pair_out(halves[0], halves[1])
            ms = _group_sum(o * o, gm)
            o = o * lax.rsqrt(ms + NORM_EPS) * ng_ref[...] * (1.0 - lambda_init)
            o_ref[sub * tq:(sub + 1) * tq, blk * LANES:(blk + 1) * LANES] = o.astype(BF16)


def _attention(kind, q, kc, vc, kl, vl, extra, lambda_init=None):
    b, t, w = q.shape
    tq = min(ATTN_QUERY_TILE, t)
    nsub = math.gcd(ATTN_SUBTILES[kind], t // tq)
    rows = tq * nsub
    full = lambda a: pl.BlockSpec((None,) + a.shape[1:], lambda i, j: (i,) + (0,) * (a.ndim - 1))
    in_specs = [pl.BlockSpec((None, rows, w), lambda i, j: (i, j, 0)), full(kc), full(vc)]
    args = [q, kc, vc]
    n_lat = 0
    if kl is not None:
        n_lat = vl.shape[1]
        in_specs += [full(kl), full(vl)]
        args += [kl, vl]
    for e in extra:
        in_specs.append(pl.BlockSpec(e.shape, lambda i, j: (0, 0)))
        args.append(e)
    if kind == "gqa":
        body = functools.partial(_gqa_kernel, n_lat=n_lat)
        nh = GQA_Q_HEADS * nsub
    else:
        body = functools.partial(_diff_kernel, n_lat=n_lat, lambda_init=lambda_init)
        nh = 2 * DIFF_HEADS * nsub
    return pl.pallas_call(
        body,
        grid=(b, t // rows),
        in_specs=in_specs,
        out_specs=pl.BlockSpec((None, rows, w), lambda i, j: (i, j, 0)),
        out_shape=jax.ShapeDtypeStruct((b, t, w), BF16),
        scratch_shapes=[pltpu.VMEM((nh, LANES, tq), BF16), pltpu.VMEM((nh, 1, tq), F32),
                        pltpu.VMEM((nh, V_ROWS, tq), F32)],
        compiler_params=_cparams(("parallel", "parallel")),
        name=kind + "_attn",
    )(*args)


def _gdn_local_kernel(x_ref, xp_ref, xn_ref, cw_ref, na_ref, nb_ref, alog_ref, dtb_ref,
                      uf_ref, wf_ref, qdf_ref, kdf_ref, qkf_ref, glf_ref,
                      ub_ref, wb_ref, qdb_ref, kdb_ref, qkb_ref, glb_ref,
                      xbuf, q_s, k_s, v_s, g_s, b_s):
    tm = x_ref.shape[0]
    cs = GDN_CHUNK
    i = pl.program_id(1)
    nblk = pl.num_programs(1)
    halo = xp_ref.shape[0]
    xbuf[halo:halo + tm, :] = x_ref[...]
    xbuf[0:halo, :] = jnp.where(i > 0, xp_ref[...], 0.0)
    xbuf[halo + tm:2 * halo + tm, :] = jnp.where(i < nblk - 1, xn_ref[...], 0.0)
    gm = _group_matrix(1.0)
    for part, dst in enumerate((q_s, k_s, v_s)):
        cols = slice(part * GDN_WIDTH, (part + 1) * GDN_WIDTH)
        acc = None
        xall = xbuf[:, cols]
        for j in range(GDN_CONV):
            d = j - GDN_CONV // 2
            xs = xall if d == 0 else pltpu.roll(xall, (-d) % (tm + 2 * halo), 0)
            term = xs[halo:halo + tm] * cw_ref[j:j + 1, cols]
            acc = term if acc is None else acc + term
        y = jax.nn.silu(acc)
        if part < 2:
            scale = GDN_HEAD_DIM ** -0.5 if part == 0 else 1.0
            for blk in range(GDN_PAIRS):
                yb = y[:, blk * LANES:(blk + 1) * LANES]
                ss = _group_sum(yb * yb, gm)
                dst[:, blk * LANES:(blk + 1) * LANES] = yb * lax.rsqrt(ss + L2_EPS) * scale
        else:
            dst[...] = y
    xa = na_ref[...] + dtb_ref[...]
    softplus = jnp.maximum(xa, 0.0) + jnp.log(1.0 + jnp.exp(-jnp.abs(xa)))
    g_s[...] = -jnp.exp(alog_ref[...]) * softplus
    b_s[...] = jax.nn.sigmoid(nb_ref[...])

    ri = lax.broadcasted_iota(jnp.int32, (cs, LANES), 0)
    li = lax.broadcasted_iota(jnp.int32, (cs, LANES), 1)
    lo = li < HEAD_LANES
    tj = li % HEAD_LANES
    r64 = lax.broadcasted_iota(jnp.int32, (cs, cs), 0)
    c64 = lax.broadcasted_iota(jnp.int32, (cs, cs), 1)
    tri_lo = jnp.where(r64 >= c64, 1.0, 0.0).astype(BF16)
    tri_up = jnp.where(r64 <= c64, 1.0, 0.0).astype(BF16)
    r128 = lax.broadcasted_iota(jnp.int32, (LANES, LANES), 0)
    c128 = lax.broadcasted_iota(jnp.int32, (LANES, LANES), 1)
    eye = jnp.where(r128 == c128, 1.0, 0.0)
    base = GDN_INVERSE_BASE
    assert base == 8, "the finite product below is written for 8x8 diagonal blocks (D^8 = 0)"
    same_base = (r128 // base) == (c128 // base)
    level_sizes = [base * 2 ** k for k in range(int(math.log2(cs // base)))]
    level_masks = [((r128 // (2 * m)) == (c128 // (2 * m))) & ((r128 // m) != (c128 // m)) for m in level_sizes]
    lo1 = lax.broadcasted_iota(jnp.int32, (1, LANES), 1) < HEAD_LANES

    def stack(x, zero):
        return jnp.concatenate([jnp.where(lo, x, zero), jnp.where(lo, zero, x)], axis=0)

    dirs = ((uf_ref, wf_ref, qdf_ref, kdf_ref, qkf_ref, glf_ref),
            (ub_ref, wb_ref, qdb_ref, kdb_ref, qkb_ref, glb_ref))

    def setup_matmuls(c):
        rows = slice(c * cs, (c + 1) * cs)
        g = g_s[rows, :]
        gh = g.astype(BF16)
        r1 = g - gh.astype(F32)
        gmid = r1.astype(BF16)
        glo = (r1 - gmid.astype(F32)).astype(BF16)
        cum_f = _dot(tri_lo, gh) + _dot(tri_lo, gmid) + _dot(tri_lo, glo)
        cum_b = _dot(tri_up, gh) + _dot(tri_up, gmid) + _dot(tri_up, glo)
        gc = jnp.where(li < GDN_HEADS, cum_f, cum_b)
        pairs = []
        for p in range(GDN_PAIRS):
            cols = slice(p * LANES, (p + 1) * LANES)
            kb = k_s[rows, cols].astype(BF16)
            kstack = stack(kb, jnp.zeros_like(kb))
            kk = _dot_nt(kb, kstack)
            qk = _dot_nt(q_s[rows, cols].astype(BF16), kstack)
            pairs.append((kk, qk))
        return c, rows, gc, pairs

    def setup(c, rows, gc, pairs, chains):
        be = b_s[rows, :]
        gt = jnp.concatenate([gc, gc], axis=0).T
        for p in range(GDN_PAIRS):
            cols = slice(p * LANES, (p + 1) * LANES)
            q128 = q_s[rows, cols]
            k128 = k_s[rows, cols]
            v128 = v_s[rows, cols]
            kk, qk = pairs[p]
            for rev in range(2):
                la = rev * GDN_HEADS + 2 * p
                gca, gcb = gc[:, la:la + 1], gc[:, la + 1:la + 2]
                bca, bcb = be[:, la:la + 1], be[:, la + 1:la + 2]
                gcol = jnp.where(lo, gca, gcb)
                grow = jnp.where(lo1, gt[la:la + 1, :], gt[la + 1:la + 2, :])
                bcol = jnp.where(lo, bca, bcb)
                if rev:
                    incl, strict = ri <= tj, ri < tj
                    last = 0
                else:
                    incl, strict = ri >= tj, ri > tj
                    last = cs - 1
                decay = jnp.exp(jnp.where(incl, gcol - grow, -jnp.inf))
                a128 = jnp.where(strict, kk * decay * bcol, 0.0)
                n = -stack(a128, 0.0)
                ea, eb = jnp.exp(gca), jnp.exp(gcb)
                rhs = jnp.concatenate([
                    jnp.concatenate([v128 * bca, v128 * bcb], axis=0),
                    jnp.concatenate([k128 * (bca * ea), k128 * (bcb * eb)], axis=0)], axis=1)
                glast = jnp.where(lo1, gc[last:last + 1, la:la + 1], gc[last:last + 1, la + 1:la + 2])
                u_ref, w_ref, qd_ref, kd_ref, qk_ref, gl_ref = dirs[rev]
                qd_ref[rows, cols] = (q128 * jnp.exp(gcol)).astype(BF16)
                kd_ref[rows, cols] = (k128 * jnp.exp(glast - gcol)).astype(BF16)
                qk_ref[rows, cols] = (qk * decay).astype(BF16)
                gl_ref[c, :, cols] = jnp.broadcast_to(jnp.exp(glast), (SUBLANES, LANES))
                chains.append(dict(n=n, rhs=rhs, rows=rows, cols=cols, rev=rev))

    def group(i):
        chains = []
        for gi in range(per_iter):
            setup(*prepared[i * per_iter + gi], chains)
        for ch in chains:
            d0 = jnp.where(same_base, ch["n"], 0.0)
            ch["t"] = eye + d0
            ch["pw"] = _dot_rounded(d0, d0)
        for ch in chains:
            both = _dot_rounded(jnp.concatenate([ch["t"], ch["pw"]], axis=0), ch["pw"])
            ch["t"] = ch["t"] + both[0:LANES]
            ch["pw"] = both[LANES:2 * LANES]
        for ch in chains:
            ch["t"] = ch["t"] + _dot_rounded(ch["t"], ch["pw"])
        for off_mask in level_masks:
            for ch in chains:
                ch["x"] = _dot_rounded(jnp.where(off_mask, ch["n"], 0.0), ch["t"])
            for ch in chains:
                ch["t"] = ch["t"] + _dot_rounded(ch["t"], ch["x"])
        for ch in chains:
            sol = _dot_rounded(ch["t"], ch["rhs"])
            u_ref, w_ref = dirs[ch["rev"]][0:2]
            u_ref[ch["rows"], ch["cols"]] = jnp.where(lo, sol[0:cs, 0:LANES], sol[cs:2 * cs, 0:LANES])
            w_ref[ch["rows"], ch["cols"]] = jnp.where(lo, sol[0:cs, LANES:], sol[cs:2 * cs, LANES:]).astype(BF16)

    prepared = [setup_matmuls(c) for c in range(tm // cs)]
    per_iter = math.gcd(GDN_CHUNKS_PER_ITER, tm // cs)
    for gi in range(tm // (cs * per_iter)):
        group(gi)


def _gdn_local(nqkv, na, nb, conv_w, alog, dtb):
    b, t, w3 = nqkv.shape
    tm = min(GDN_BLOCK_ROWS, t)
    nblk = t // tm
    cpb = tm // GDN_CHUNK
    halo = SUBLANES
    hb = tm // halo
    tok = lambda width: pl.BlockSpec((None, tm, width), lambda i, j: (i, j, 0))
    const2 = lambda shape: pl.BlockSpec(shape, lambda i, j: (0, 0))
    in_specs = [
        tok(w3),
        pl.BlockSpec((None, halo, w3), lambda i, j: (i, jnp.maximum(j * hb - 1, 0), 0)),
        pl.BlockSpec((None, halo, w3), lambda i, j: (i, jnp.minimum((j + 1) * hb, t // halo - 1), 0)),
        const2((GDN_CONV, w3)), tok(LANES), tok(LANES), const2((1, LANES)), const2((1, LANES)),
    ]
    per_dir_specs = [tok(GDN_WIDTH)] * 5 + [pl.BlockSpec((None, cpb, SUBLANES, GDN_WIDTH), lambda i, j: (i, j, 0, 0))]
    per_dir_shapes = ([jax.ShapeDtypeStruct((b, t, GDN_WIDTH), F32)]
                      + [jax.ShapeDtypeStruct((b, t, GDN_WIDTH), BF16)] * 4
                      + [jax.ShapeDtypeStruct((b, t // GDN_CHUNK, SUBLANES, GDN_WIDTH), F32)])
    outs = pl.pallas_call(
        _gdn_local_kernel,
        grid=(b, nblk),
        in_specs=in_specs,
        out_specs=per_dir_specs * 2,
        out_shape=per_dir_shapes * 2,
        scratch_shapes=[
            pltpu.VMEM((tm + 2 * halo, w3), F32),
            pltpu.VMEM((tm, GDN_WIDTH), F32), pltpu.VMEM((tm, GDN_WIDTH), F32), pltpu.VMEM((tm, GDN_WIDTH), F32),
            pltpu.VMEM((tm, LANES), F32), pltpu.VMEM((tm, LANES), F32),
        ],
        compiler_params=_cparams(("parallel", "parallel")),
        name="gdn_local",
    )(nqkv, nqkv, nqkv, conv_w, na, nb, alog, dtb)
    return outs[:6], outs[6:]


def _gdn_scan_kernel(uf_ref, wf_ref, qdf_ref, kdf_ref, qkf_ref, glf_ref,
                     ub_ref, wb_ref, qdb_ref, kdb_ref, qkb_ref, glb_ref,
                     s0f_ref, s0b_ref,
                     of_ref, ob_ref, sff_ref, sfb_ref,
                     stf, stb):
    n = pl.program_id(1)
    cs = GDN_CHUNK
    nbat = uf_ref.shape[0]
    cpb = uf_ref.shape[1] // cs

    @pl.when(n == 0)
    def _():
        stf[...] = s0f_ref[...]
        stb[...] = s0b_ref[...]

    lo = lax.broadcasted_iota(jnp.int32, (cs, LANES), 1) < HEAD_LANES
    r128 = lax.broadcasted_iota(jnp.int32, (LANES, LANES), 0) // HEAD_LANES
    c128 = lax.broadcasted_iota(jnp.int32, (LANES, LANES), 1) // HEAD_LANES
    bd = r128 == c128

    fwd = (uf_ref, wf_ref, qdf_ref, kdf_ref, qkf_ref, glf_ref, of_ref, stf)
    bwd = (ub_ref, wb_ref, qdb_ref, kdb_ref, qkb_ref, glb_ref, ob_ref, stb)

    for cc in range(cpb):
        chains = []
        for c, (u_ref, w_ref, qd_ref, kd_ref, qk_ref, gl_ref, o_ref, st) in ((cc, fwd), (cpb - 1 - cc, bwd)):
            rows = slice(c * cs, (c + 1) * cs)
            for bi in range(nbat):
                for p in range(GDN_PAIRS):
                    cols = slice(p * LANES, (p + 1) * LANES)
                    s = st[bi, p]
                    sb = s.astype(BF16)
                    ws = _dot(w_ref[bi, rows, cols], sb)
                    qs = _dot(qd_ref[bi, rows, cols], sb)
                    chains.append((c, bi, rows, cols, p, s, ws, qs, u_ref, kd_ref, qk_ref, gl_ref, o_ref, st))
        for c, bi, rows, cols, p, s, ws, qs, u_ref, kd_ref, qk_ref, gl_ref, o_ref, st in chains:
            vnew = u_ref[bi, rows, cols] - ws
            vb = vnew.astype(BF16)
            zero = jnp.zeros_like(vb)
            vstack = jnp.concatenate([jnp.where(lo, vb, zero), jnp.where(lo, zero, vb)], axis=0)
            o_ref[bi, rows, cols] = qs + _dot(qk_ref[bi, rows, cols], vstack)
            upd = _dot(kd_ref[bi, rows, cols].astype(F32).T.astype(BF16), vb)
            st[bi, p] = s * gl_ref[bi, c, 0:1, cols] + jnp.where(bd, upd, 0.0)

    @pl.when(n == pl.num_programs(1) - 1)
    def _():
        sff_ref[...] = stf[...]
        sfb_ref[...] = stb[...]


def _gdn_scan(fwd, bwd, s0f, s0b):
    b, t, _ = fwd[0].shape
    tm = min(GDN_BLOCK_ROWS, t)
    nblk = t // tm
    cpb = tm // GDN_CHUNK
    bb = math.gcd(GDN_SCAN_BATCH, b)
    f_tok = pl.BlockSpec((bb, tm, GDN_WIDTH), lambda i, j: (i, j, 0))
    b_tok = pl.BlockSpec((bb, tm, GDN_WIDTH), lambda i, j: (i, nblk - 1 - j, 0))
    f_gl = pl.BlockSpec((bb, cpb, SUBLANES, GDN_WIDTH), lambda i, j: (i, j, 0, 0))
    b_gl = pl.BlockSpec((bb, cpb, SUBLANES, GDN_WIDTH), lambda i, j: (i, nblk - 1 - j, 0, 0))
    st_spec = pl.BlockSpec((bb, GDN_PAIRS, LANES, LANES), lambda i, j: (i, 0, 0, 0))
    st_shape = jax.ShapeDtypeStruct((b, GDN_PAIRS, LANES, LANES), F32)
    o_shape = jax.ShapeDtypeStruct((b, t, GDN_WIDTH), F32)
    return pl.pallas_call(
        _gdn_scan_kernel,
        grid=(b // bb, nblk),
        in_specs=[f_tok] * 5 + [f_gl] + [b_tok] * 5 + [b_gl] + [st_spec, st_spec],
        out_specs=[f_tok, b_tok, st_spec, st_spec],
        out_shape=[o_shape, o_shape, st_shape, st_shape],
        scratch_shapes=[pltpu.VMEM((bb, GDN_PAIRS, LANES, LANES), F32)] * 2,
        compiler_params=_cparams(("parallel", "arbitrary")),
        name="gdn_scan",
    )(*fwd, *bwd, s0f, s0b)


def _mix_ffn_kernel(h_ref, d_ref, gq_ref, of_ref, ob_ref, z_ref, ng_ref, gate1_ref, wo_ref,
                    g_ref, sh_ref, sc_ref, gate_ref, wgu_ref, wd_ref, fg_ref, o_ref, *, final_norm):
    gm = _group_matrix(1.0 / GDN_HEAD_DIM)
    pieces = [d_ref[...], gq_ref[...]]
    for blk in range(GDN_PAIRS):
        cols = slice(blk * LANES, (blk + 1) * LANES)
        o = of_ref[:, cols] + ob_ref[:, cols]
        ms = _group_sum(o * o, gm)
        r = o * lax.rsqrt(ms + NORM_EPS) * ng_ref[...] * jax.nn.silu(z_ref[:, cols])
        pieces.append(r.astype(BF16))
    x = h_ref[...] + gate1_ref[...] * _dot(jnp.concatenate(pieces, axis=1), wo_ref[...])

    y = x * lax.rsqrt(jnp.mean(x * x, axis=-1, keepdims=True) + NORM_EPS) * g_ref[...]
    a = (y * (1.0 + sc_ref[...]) + sh_ref[...]).astype(BF16)
    hidden = wd_ref.shape[0]
    starts = list(range(0, hidden, FFN_CHUNK))

    def gate_up(lo):
        width = min(FFN_CHUNK, hidden - lo)
        return _dot(a, wgu_ref[:, lo:lo + width]), _dot(a, wgu_ref[:, hidden + lo:hidden + lo + width]), width

    pending = [gate_up(lo) for lo in starts[:FFN_SKEW]]
    acc = None
    for n, lo in enumerate(starts):
        gate, up, width = pending.pop(0)
        if n + FFN_SKEW < len(starts):
            pending.append(gate_up(starts[n + FFN_SKEW]))
        act = (jax.nn.silu(gate) * up).astype(BF16)
        part = _dot(act, wd_ref[lo:lo + width, :])
        acc = part if acc is None else acc + part
    out = x + gate_ref[...] * acc
    if final_norm:
        out = out * lax.rsqrt(jnp.mean(out * out, axis=-1, keepdims=True) + NORM_EPS) * fg_ref[...]
    o_ref[...] = out


def _mix_ffn(h, d, gq, of, ob, z, ng, gate1, w_out, g, sh, sc, gate, w_gu, w_down, fg, final_norm):
    b, t, dm = h.shape
    tm = min(MIX_FFN_ROWS, t)
    nb = sh.shape[0]
    mod_map = (lambda i, k: (i, 0, 0)) if nb == b else (lambda i, k: (0, 0, 0))
    tok = lambda width: pl.BlockSpec((None, tm, width), lambda i, k: (i, k, 0))
    vec = lambda width: pl.BlockSpec((1, width), lambda i, k: (0, 0))
    mod = pl.BlockSpec((None, 1, dm), mod_map)
    resident = lambda w: pl.BlockSpec(w.shape, lambda i, k: (0, 0), pipeline_mode=pl.Buffered(1))
    return pl.pallas_call(
        functools.partial(_mix_ffn_kernel, final_norm=final_norm),
        grid=(b, t // tm),
        in_specs=[tok(dm), tok(DIFF_WIDTH), tok(GQA_WIDTH), tok(GDN_WIDTH), tok(GDN_WIDTH), tok(GDN_WIDTH),
                  vec(LANES), mod, resident(w_out),
                  vec(dm), mod, mod, mod, resident(w_gu), resident(w_down), vec(dm)],
        out_specs=tok(dm),
        out_shape=jax.ShapeDtypeStruct((b, t, dm), F32),
        compiler_params=_cparams(("parallel", "parallel")),
        name="mix_ffn",
    )(h, d, gq, of, ob, z, ng, gate1, w_out, g, sh, sc, gate, w_gu, w_down, fg)


def _rope_tables(t, rot_dim):
    nf = rot_dim // 4
    pos = jnp.arange(t)
    row = (pos // GRID_W).astype(F32)
    col = (pos % GRID_W).astype(F32)
    inv_freq = ROPE_THETA ** (-jnp.arange(nf, dtype=F32) / nf)
    ar, ac = row[:, None] * inv_freq, col[:, None] * inv_freq
    z = jnp.zeros_like(ar)
    cos = jnp.concatenate([jnp.cos(ar), jnp.cos(ar), jnp.cos(ac), jnp.cos(ac)], axis=1)
    sa = jnp.concatenate([-jnp.sin(ar), z, -jnp.sin(ac), z], axis=1)
    sb = jnp.concatenate([z, jnp.sin(ar), z, jnp.sin(ac)], axis=1)
    rep = LANES // rot_dim
    return tuple(jnp.tile(a, (1, rep)) for a in (cos, sa, sb))


def _pad_lanes(v):
    return jnp.pad(v.reshape(1, -1), ((0, 0), (0, LANES - v.size)))


def _relayout_w_in(w):
    o = 0
    pieces = []
    for width in (DIFF_WIDTH, DIFF_WIDTH, DIFF_WIDTH):
        pieces.append(w[:, o:o + width]); o += width
    gq = w[:, o:o + GQA_WIDTH].reshape(-1, GQA_Q_HEADS, GQA_HEAD_DIM); o += GQA_WIDTH
    pieces.append(jnp.take(gq, jnp.array(GQA_Q_ORDER), axis=1).reshape(-1, GQA_WIDTH))
    for width in (GQA_KV_WIDTH, GQA_KV_WIDTH, 3 * GDN_WIDTH, GDN_WIDTH):
        pieces.append(w[:, o:o + width]); o += width
    for width in (2 * GDN_HEADS, 2 * GDN_HEADS):
        pieces.append(jnp.pad(w[:, o:o + width], ((0, 0), (0, LANES - width)))); o += width
    return jnp.concatenate(pieces, axis=1).astype(BF16)


def _relayout_w_out(w):
    gq = w[DIFF_WIDTH:DIFF_WIDTH + GQA_WIDTH].reshape(GQA_Q_HEADS, GQA_HEAD_DIM, -1)
    gq = jnp.take(gq, jnp.array(GQA_Q_ORDER), axis=0).reshape(GQA_WIDTH, -1)
    return jnp.concatenate([w[:DIFF_WIDTH], gq, w[DIFF_WIDTH + GQA_WIDTH:]], axis=0).astype(BF16)


def kernel(x, c, ctx, c_ctx, norm1_g, ada_w, ada_b, w_in, diff_lambda, diff_norm_g, q_norm_g, k_norm_g,
           gdn_conv_w, gdn_a_log, gdn_dt_bias, gdn_norm_g, w_out, norm2_g, ffn_w_gu, ffn_w_down, final_norm_g):
    b, t, d = x.shape
    depth = w_in.shape[0]
    rope = _rope_tables(t, DIFF_QK_DIM) + _rope_tables(t, GQA_HEAD_DIM)

    cond = jnp.concatenate([c, c_ctx[None, :], jnp.zeros((COND_ROWS - b - 1, d), F32)], axis=0)
    mod = _ada(cond, ada_w, ada_b).reshape(depth, COND_ROWS, 6, d)

    tile2 = lambda v: jnp.tile(v.reshape(1, -1), (1, LANES // v.size))
    zeros_state = jnp.zeros((b, GDN_PAIRS, LANES, LANES), F32)

    h, hc = x, ctx
    for layer in range(depth):
        need_ctx = layer < depth - 1
        lambda_init = 0.8 - 0.6 * math.exp(-0.3 * layer)
        mod_l = [mod[layer, :b, k][:, None, :] for k in range(6)]
        mod_c = [mod[layer, b:b + 1, k][:, None, :] for k in range(6)]
        w_in_l = _relayout_w_in(w_in[layer])
        w_out_l = _relayout_w_out(w_out[layer])
        w_gu_l = ffn_w_gu[layer].astype(BF16)
        w_down_l = ffn_w_down[layer].astype(BF16)
        g1 = norm1_g[layer].reshape(1, d)
        g2 = norm2_g[layer].reshape(1, d)
        qg, kg = tile2(q_norm_g[layer]), tile2(k_norm_g[layer])
        dng, nng = tile2(diff_norm_g[layer]), tile2(gdn_norm_g[layer])
        alog, dtb = _pad_lanes(gdn_a_log[layer]), _pad_lanes(gdn_dt_bias[layer])
        lam = diff_lambda[layer]
        conv_w = gdn_conv_w[layer]

        pl_ = _in_proj(h, g1, mod_l[0], mod_l[1], w_in_l, qg, kg, rope)
        pc_ = _in_proj(hc, g1, mod_c[0], mod_c[1], w_in_l, qg, kg, None)
        dq_l, dk_l, dv_l, gq_l, gk_l, gv_l, nqkv_l, nz_l, na_l, nb_l = pl_
        dq_c, dk_c, dv_c, gq_c, gk_c, gv_c, nqkv_c, nz_c, na_c, nb_c = pc_

        d_l = _attention("diff", dq_l, dk_c, dv_c, dk_l, dv_l, (lam, dng), lambda_init)
        a_l = _attention("gqa", gq_l, gk_c, gv_c, gk_l, gv_l, ())

        fwd_c, bwd_c = _gdn_local(nqkv_c, na_c, nb_c, conv_w, alog, dtb)
        fwd_l, bwd_l = _gdn_local(nqkv_l, na_l, nb_l, conv_w, alog, dtb)
        ocf, ocb, scf, scb = _gdn_scan(fwd_c, bwd_c, zeros_state, zeros_state)
        olf, olb, _, _ = _gdn_scan(fwd_l, bwd_l, scf, scb)

        last = layer == depth - 1
        fg = final_norm_g.reshape(1, d)
        h = _mix_ffn(h, d_l, a_l, olf, olb, nz_l, nng, mod_l[2], w_out_l,
                     g2, mod_l[3], mod_l[4], mod_l[5], w_gu_l, w_down_l, fg, last)
        if need_ctx:
            d_c = _attention("diff", dq_c, dk_c, dv_c, None, None, (lam, dng), lambda_init)
            a_c = _attention("gqa", gq_c, gk_c, gv_c, None, None, ())
            hc = _mix_ffn(hc, d_c, a_c, ocf, ocb, nz_c, nng, mod_c[2], w_out_l,
                          g2, mod_c[3], mod_c[4], mod_c[5], w_gu_l, w_down_l, fg, False)
    return h
```

```python
import functools
import math

import jax
import jax.numpy as jnp
from jax import lax
from jax.experimental import pallas as pl
from jax.experimental.pallas import tpu as pltpu

F32 = jnp.float32
BF16 = jnp.bfloat16

LANES = 128
SUBLANES = 8
D_MODEL = 1024
GRID_W = 64
ROPE_THETA = 10000.0
NORM_EPS = 1e-6
L2_EPS = 1e-6
LOG2E = 1.4426950408889634

DIFF_HEADS = 4
DIFF_QK_DIM = 32
DIFF_V_DIM = 64
DIFF_WIDTH = DIFF_HEADS * DIFF_V_DIM
GQA_Q_HEADS = 6
GQA_KV_HEADS = 2
GQA_HEAD_DIM = 64
GQA_WIDTH = GQA_Q_HEADS * GQA_HEAD_DIM
GQA_KV_WIDTH = GQA_KV_HEADS * GQA_HEAD_DIM
GDN_HEADS = 6
GDN_HEAD_DIM = 64
GDN_WIDTH = GDN_HEADS * GDN_HEAD_DIM
GDN_CONV = 5
GDN_CHUNK = 64
GDN_PAIRS = GDN_HEADS // 2
FFN_HIDDEN = 2816

OFF_DQ = 0
OFF_DK = OFF_DQ + DIFF_WIDTH
OFF_DV = OFF_DK + DIFF_WIDTH
OFF_GQ = OFF_DV + DIFF_WIDTH
OFF_GK = OFF_GQ + GQA_WIDTH
OFF_GV = OFF_GK + GQA_KV_WIDTH
OFF_NQKV = OFF_GV + GQA_KV_WIDTH
OFF_NZ = OFF_NQKV + 3 * GDN_WIDTH
OFF_NA = OFF_NZ + GDN_WIDTH
OFF_NB = OFF_NA + LANES
IN_PAD = OFF_NB + LANES

GQA_Q_ORDER = (0, 3, 1, 4, 2, 5)
HEAD_LANES = 64
HEAD_V = HEAD_LANES
ONES_ROWS = 16
V_ROWS = HEAD_V + ONES_ROWS

COND_ROWS = 16
ADA_COL_TILE = 1536
GDN_INVERSE_BASE = 8
FFN_CHUNK = 512
FFN_SKEW = 1
GDN_SCAN_BATCH = 4
GDN_CHUNKS_PER_ITER = 4

ATTN_SKEW = {"diff": 3, "gqa": 4}
ATTN_CHUNKS_PER_ITER = 16
ATTN_KEY_CHUNK = 512
ATTN_QUERY_TILE = 256
ATTN_SUBTILES = {"diff": 2, "gqa": 1}
IN_PROJ_ROWS = 512
MIX_FFN_ROWS = 512
GDN_BLOCK_ROWS = 256

VMEM_LIMIT = 56 * 1024 * 1024


def _cparams(sem):
    return pltpu.CompilerParams(dimension_semantics=sem, vmem_limit_bytes=VMEM_LIMIT)


def _dot(a, b):
    return jnp.dot(a, b, preferred_element_type=F32)


def _dot_nt(a, b):
    return lax.dot_general(a, b, (((1,), (1,)), ((), ())), preferred_element_type=F32)


def _split_bf16(x):
    hi = x.astype(BF16)
    lo = (x - hi.astype(F32)).astype(BF16)
    return hi, lo


def _dot_rounded(a, b):
    return _dot(a.astype(BF16), b.astype(BF16))


def _group_matrix(scale):
    r = lax.broadcasted_iota(jnp.int32, (LANES, LANES), 0) // HEAD_LANES
    c = lax.broadcasted_iota(jnp.int32, (LANES, LANES), 1) // HEAD_LANES
    return jnp.where(r == c, scale, 0.0).astype(BF16)


def _group_sum(x, gm):
    hi, lo = _split_bf16(x)
    return _dot(hi, gm) + _dot(lo, gm)


def _rope(x, c, sa, sb, half):
    return x * c + pltpu.roll(x, LANES - half, 1) * sa + pltpu.roll(x, half, 1) * sb


def _ada_kernel(c_ref, w_ref, b_ref, o_ref):
    s = jax.nn.silu(c_ref[...]).astype(BF16)
    o_ref[...] = _dot(s, w_ref[...].astype(BF16)) + b_ref[...]


def _ada(cond, ada_w, ada_b):
    depth, d, n = ada_w.shape
    rows = cond.shape[0]
    tn = ADA_COL_TILE
    return pl.pallas_call(
        _ada_kernel,
        grid=(depth, n // tn),
        in_specs=[
            pl.BlockSpec((rows, d), lambda l, j: (0, 0)),
            pl.BlockSpec((None, d, tn), lambda l, j: (l, 0, j)),
            pl.BlockSpec((None, 1, tn), lambda l, j: (l, 0, j)),
        ],
        out_specs=pl.BlockSpec((None, rows, tn), lambda l, j: (l, 0, j)),
        out_shape=jax.ShapeDtypeStruct((depth, rows, n), F32),
        compiler_params=_cparams(("parallel", "parallel")),
        name="ada_mod",
    )(cond, ada_w, ada_b.reshape(depth, 1, n))


def _store_values(ref, first_head, v):
    vt = v.T.astype(BF16)
    n_chunks, _, tk = ref.shape
    ones = jnp.ones((ONES_ROWS, tk), BF16)
    for c in range(n_chunks):
        for s in range(2):
            r0 = (first_head + s) * V_ROWS
            ref[c, r0:r0 + HEAD_V, :] = vt[s * HEAD_V:(s + 1) * HEAD_V, c * tk:(c + 1) * tk]
            ref[c, r0 + HEAD_V:r0 + V_ROWS, :] = ones


def _in_proj_kernel(*refs, use_rope):
    if use_rope:
        (x_ref, g_ref, sh_ref, sc_ref, w_ref, qg_ref, kg_ref,
         cd_ref, sad_ref, sbd_ref, cg_ref, sag_ref, sbg_ref,
         dq_ref, dk_ref, dv_ref, gq_ref, gk_ref, gv_ref, nqkv_ref, nz_ref, na_ref, nb_ref) = refs
    else:
        (x_ref, g_ref, sh_ref, sc_ref, w_ref, qg_ref, kg_ref,
         dq_ref, dk_ref, dv_ref, gq_ref, gk_ref, gv_ref, nqkv_ref, nz_ref, na_ref, nb_ref) = refs
    x = x_ref[...]
    y = x * lax.rsqrt(jnp.mean(x * x, axis=-1, keepdims=True) + NORM_EPS) * g_ref[...]
    a = (y * (1.0 + sc_ref[...]) + sh_ref[...]).astype(BF16)

    bounds = (OFF_DQ, OFF_DK, OFF_DV, OFF_GQ, OFF_GK, OFF_NQKV, OFF_NZ, OFF_NA, IN_PAD)
    groups = {}

    def proj(lo, width):
        g0 = max(b for b in bounds if b <= lo)
        g1 = min(b for b in bounds if b > lo)
        assert lo + width <= g1
        if g0 not in groups:
            groups[g0] = _dot(a, w_ref[:, g0:g1])
        return groups[g0][:, lo - g0:lo - g0 + width]

    gm = _group_matrix(1.0 / GQA_HEAD_DIM)

    for off, out, scale in ((OFF_DQ, dq_ref, DIFF_QK_DIM ** -0.5 * LOG2E), (OFF_DK, dk_ref, 1.0)):
        for blk in range(DIFF_WIDTH // LANES):
            p = proj(off + blk * LANES, LANES)
            if use_rope:
                p = _rope(p, cd_ref[...], sad_ref[...], sbd_ref[...], DIFF_QK_DIM // 4)
            out[:, blk * LANES:(blk + 1) * LANES] = (p * scale).astype(BF16)
    for blk in range(DIFF_WIDTH // LANES):
        _store_values(dv_ref, 2 * blk, proj(OFF_DV + blk * LANES, LANES))

    def qk_prep(p, gain, scale):
        ms = _group_sum(p * p, gm)
        p = p * lax.rsqrt(ms + NORM_EPS) * gain
        if use_rope:
            p = _rope(p, cg_ref[...], sag_ref[...], sbg_ref[...], GQA_HEAD_DIM // 4)
        return (p * scale).astype(BF16)

    for blk in range(GQA_WIDTH // LANES):
        p = proj(OFF_GQ + blk * LANES, LANES)
        gq_ref[:, blk * LANES:(blk + 1) * LANES] = qk_prep(p, qg_ref[...], GQA_HEAD_DIM ** -0.5 * LOG2E)
    gk_ref[...] = qk_prep(proj(OFF_GK, LANES), kg_ref[...], 1.0)
    _store_values(gv_ref, 0, proj(OFF_GV, LANES))

    for part in range(3):
        nqkv_ref[:, part * GDN_WIDTH:(part + 1) * GDN_WIDTH] = proj(OFF_NQKV + part * GDN_WIDTH, GDN_WIDTH)
    nz_ref[...] = proj(OFF_NZ, GDN_WIDTH)
    na_ref[...] = proj(OFF_NA, LANES)
    nb_ref[...] = proj(OFF_NB, LANES)


def _in_proj(x, g, sh, sc, w, qg, kg, rope):
    b, t, d = x.shape
    tm = min(IN_PROJ_ROWS, t)
    nb = sh.shape[0]
    mod_map = (lambda i, j: (i, 0, 0)) if nb == b else (lambda i, j: (0, 0, 0))
    tok = lambda width: pl.BlockSpec((None, tm, width), lambda i, j: (i, j, 0))
    const2 = lambda shape: pl.BlockSpec(shape, lambda i, j: (0, 0))
    in_specs = [
        tok(d), const2((1, d)),
        pl.BlockSpec((None, 1, d), mod_map), pl.BlockSpec((None, 1, d), mod_map),
        const2((d, IN_PAD)), const2((1, LANES)), const2((1, LANES)),
    ]
    args = [x, g, sh, sc, w, qg, kg]
    if rope is not None:
        in_specs += [pl.BlockSpec((tm, LANES), lambda i, j: (j, 0))] * 6
        args += list(rope)
    widths = (DIFF_WIDTH, DIFF_WIDTH, DIFF_HEADS * V_ROWS, GQA_WIDTH, GQA_KV_WIDTH, GQA_KV_HEADS * V_ROWS,
              3 * GDN_WIDTH, GDN_WIDTH, LANES, LANES)
    dtypes = (BF16,) * 6 + (F32,) * 4
    transposed = (2, 5)
    tk = min(ATTN_KEY_CHUNK, tm)
    out_specs = [pl.BlockSpec((None, tm // tk, wd, tk), lambda i, j: (i, j, 0, 0)) if k in transposed else tok(wd)
                 for k, wd in enumerate(widths)]
    out_shape = [jax.ShapeDtypeStruct((b, t // tk, wd, tk) if k in transposed else (b, t, wd), dt)
                 for k, (wd, dt) in enumerate(zip(widths, dtypes))]
    return pl.pallas_call(
        functools.partial(_in_proj_kernel, use_rope=rope is not None),
        grid=(b, t // tm),
        in_specs=in_specs,
        out_specs=out_specs,
        out_shape=out_shape,
        compiler_params=_cparams(("parallel", "parallel")),
        name="in_proj",
    )(*args)


def _flash_all(cols, vrows, skew, qm_s, m_s, acc_s, kc_ref, vc_ref, kl_ref, vl_ref, n_lat):
    nh = len(cols)
    m_s[...] = jnp.full(m_s.shape, -jnp.inf, F32)
    acc_s[...] = jnp.zeros(acc_s.shape, F32)

    def pipeline(chunk_list):
        items = [(ck, h) for ck in chunk_list for h in range(nh)]

        def scores(item):
            (k_ref, _, tk, j), h = item
            if tk is None:
                rows = slice(None)
            elif isinstance(j, int):
                rows = slice(j * tk, (j + 1) * tk)
            else:
                rows = pl.ds(pl.multiple_of(j * tk, tk), tk)
            return _dot(k_ref[rows, cols[h]:cols[h] + LANES], qm_s[h])

        pending = [scores(it) for it in items[:skew]]
        for n, ((_, v_ref, _, j), h) in enumerate(items):
            s = pending.pop(0)
            if n + skew < len(items):
                pending.append(scores(items[n + skew]))
            m_old = m_s[h]
            m_new = jnp.maximum(m_old, jnp.max(s, axis=0, keepdims=True))
            alpha = jnp.exp2(m_old - m_new)
            p = jnp.exp2(s - m_new)
            acc_s[h] = alpha * acc_s[h] + _dot(v_ref[j, vrows[h]:vrows[h] + V_ROWS, :], p.astype(BF16))
            m_s[h] = m_new

    ctx_chunk = (kc_ref, vc_ref, None, 0)
    if not n_lat:
        pipeline([ctx_chunk])
        return
    tk = vl_ref.shape[-1]
    per_iter = math.gcd(ATTN_CHUNKS_PER_ITER, n_lat)
    if per_iter == n_lat:
        pipeline([ctx_chunk] + [(kl_ref, vl_ref, tk, j) for j in range(n_lat)])
        return
    pipeline([ctx_chunk])

    def body(i, carry):
        pipeline([(kl_ref, vl_ref, tk, i * per_iter + jj) for jj in range(per_iter)])
        return carry

    lax.fori_loop(0, n_lat // per_iter, body, 0)


def _transposed_q(q_ref, rows, blk):
    return q_ref[rows, blk * LANES:(blk + 1) * LANES].astype(F32).T


def _normalised(acc_s, h):
    acc = acc_s[h]
    return acc[0:HEAD_V] / acc[HEAD_V:HEAD_V + 1]


def _pair_out(oa, ob):
    return jnp.concatenate([oa, ob], axis=0).T


def _gqa_kernel(*refs, n_lat):
    if n_lat:
        q_ref, kc_ref, vc_ref, kl_ref, vl_ref, o_ref, qm_s, m_s, acc_s = refs
    else:
        q_ref, kc_ref, vc_ref, o_ref, qm_s, m_s, acc_s = refs
        kl_ref = vl_ref = None
    tq = qm_s.shape[-1]
    nsub = q_ref.shape[0] // tq
    nblk = GQA_WIDTH // LANES
    nh = 2 * nblk
    lo = lax.broadcasted_iota(jnp.int32, (LANES, tq), 0) < GQA_HEAD_DIM
    for sub in range(nsub):
        rows = slice(sub * tq, (sub + 1) * tq)
        for blk in range(nblk):
            qt = _transposed_q(q_ref, rows, blk)
            qm_s[sub * nh + 2 * blk] = jnp.where(lo, qt, 0.0).astype(BF16)
            qm_s[sub * nh + 2 * blk + 1] = jnp.where(lo, 0.0, qt).astype(BF16)
    _flash_all((0,) * (nh * nsub), (0, V_ROWS) * (nblk * nsub), ATTN_SKEW["gqa"], qm_s, m_s, acc_s,
               kc_ref, vc_ref, kl_ref, vl_ref, n_lat)
    for sub in range(nsub):
        for blk in range(nblk):
            h0 = sub * nh + 2 * blk
            o = _pair_out(_normalised(acc_s, h0), _normalised(acc_s, h0 + 1))
            o_ref[sub * tq:(sub + 1) * tq, blk * LANES:(blk + 1) * LANES] = o.astype(BF16)


def _diff_kernel(*refs, n_lat, lambda_init):
    if n_lat:
        q_ref, kc_ref, vc_ref, kl_ref, vl_ref, lam_ref, ng_ref, o_ref, qm_s, m_s, acc_s = refs
    else:
        q_ref, kc_ref, vc_ref, lam_ref, ng_ref, o_ref, qm_s, m_s, acc_s = refs
        kl_ref = vl_ref = None
    tq = qm_s.shape[-1]
    nsub = q_ref.shape[0] // tq
    nblk = DIFF_WIDTH // LANES
    nh = 4 * nblk
    lf = lam_ref[...]
    lam = (jnp.exp(jnp.sum(lf[0:1] * lf[1:2], axis=-1, keepdims=True))
           - jnp.exp(jnp.sum(lf[2:3] * lf[3:4], axis=-1, keepdims=True)) + lambda_init)
    row = lax.broadcasted_iota(jnp.int32, (LANES, tq), 0)
    gm = _group_matrix(1.0 / DIFF_V_DIM)
    for sub in range(nsub):
        for blk in range(nblk):
            qt = _transposed_q(q_ref, slice(sub * tq, (sub + 1) * tq), blk)
            for sc in range(4):
                qm_s[sub * nh + 4 * blk + sc] = jnp.where((row // DIFF_QK_DIM) == sc, qt, 0.0).astype(BF16)
    cols = tuple(blk * LANES for blk in range(nblk) for _ in range(4)) * nsub
    vrows = tuple((2 * blk + s) * V_ROWS for blk in range(nblk) for s in range(2) for _ in range(2)) * nsub
    _flash_all(cols, vrows, ATTN_SKEW["diff"], qm_s, m_s, acc_s, kc_ref, vc_ref, kl_ref, vl_ref, n_lat)
    for sub in range(nsub):
        for blk in range(nblk):
            halves = []
            for s in range(2):
                i0 = sub * nh + 4 * blk + 2 * s
                halves.append(_normalised(acc_s, i0) - lam * _normalised(acc_s, i0 + 1))
            o = _pair_out(halves[0], halves[1])
            ms = _group_sum(o * o, gm)
            o = o * lax.rsqrt(ms + NORM_EPS) * ng_ref[...] * (1.0 - lambda_init)
            o_ref[sub * tq:(sub + 1) * tq, blk * LANES:(blk + 1) * LANES] = o.astype(BF16)


def _attention(kind, q, kc, vc, kl, vl, extra, lambda_init=None):
    b, t, w = q.shape
    tq = min(ATTN_QUERY_TILE, t)
    nsub = math.gcd(ATTN_SUBTILES[kind], t // tq)
    rows = tq * nsub
    full = lambda a: pl.BlockSpec((None,) + a.shape[1:], lambda i, j: (i,) + (0,) * (a.ndim - 1))
    in_specs = [pl.BlockSpec((None, rows, w), lambda i, j: (i, j, 0)), full(kc), full(vc)]
    args = [q, kc, vc]
    n_lat = 0
    if kl is not None:
        n_lat = vl.shape[1]
        in_specs += [full(kl), full(vl)]
        args += [kl, vl]
    for e in extra:
        in_specs.append(pl.BlockSpec(e.shape, lambda i, j: (0, 0)))
        args.append(e)
    if kind == "gqa":
        body = functools.partial(_gqa_kernel, n_lat=n_lat)
        nh = GQA_Q_HEADS * nsub
    else:
        body = functools.partial(_diff_kernel, n_lat=n_lat, lambda_init=lambda_init)
        nh = 2 * DIFF_HEADS * nsub
    return pl.pallas_call(
        body,
        grid=(b, t // rows),
        in_specs=in_specs,
        out_specs=pl.BlockSpec((None, rows, w), lambda i, j: (i, j, 0)),
        out_shape=jax.ShapeDtypeStruct((b, t, w), BF16),
        scratch_shapes=[pltpu.VMEM((nh, LANES, tq), BF16), pltpu.VMEM((nh, 1, tq), F32),
                        pltpu.VMEM((nh, V_ROWS, tq), F32)],
        compiler_params=_cparams(("parallel", "parallel")),
        name=kind + "_attn",
    )(*args)


def _gdn_local_kernel(x_ref, xp_ref, xn_ref, cw_ref, na_ref, nb_ref, alog_ref, dtb_ref,
                      uf_ref, wf_ref, qdf_ref, kdf_ref, qkf_ref, glf_ref,
                      ub_ref, wb_ref, qdb_ref, kdb_ref, qkb_ref, glb_ref,
                      xbuf, q_s, k_s, v_s, g_s, b_s):
    tm = x_ref.shape[0]
    cs = GDN_CHUNK
    i = pl.program_id(1)
    nblk = pl.num_programs(1)
    halo = xp_ref.shape[0]
    xbuf[halo:halo + tm, :] = x_ref[...]
    xbuf[0:halo, :] = jnp.where(i > 0, xp_ref[...], 0.0)
    xbuf[halo + tm:2 * halo + tm, :] = jnp.where(i < nblk - 1, xn_ref[...], 0.0)
    gm = _group_matrix(1.0)
    for part, dst in enumerate((q_s, k_s, v_s)):
        cols = slice(part * GDN_WIDTH, (part + 1) * GDN_WIDTH)
        acc = None
        xall = xbuf[:, cols]
        for j in range(GDN_CONV):
            d = j - GDN_CONV // 2
            xs = xall if d == 0 else pltpu.roll(xall, (-d) % (tm + 2 * halo), 0)
            term = xs[halo:halo + tm] * cw_ref[j:j + 1, cols]
            acc = term if acc is None else acc + term
        y = jax.nn.silu(acc)
        if part < 2:
            scale = GDN_HEAD_DIM ** -0.5 if part == 0 else 1.0
            for blk in range(GDN_PAIRS):
                yb = y[:, blk * LANES:(blk + 1) * LANES]
                ss = _group_sum(yb * yb, gm)
                dst[:, blk * LANES:(blk + 1) * LANES] = yb * lax.rsqrt(ss + L2_EPS) * scale
        else:
            dst[...] = y
    xa = na_ref[...] + dtb_ref[...]
    softplus = jnp.maximum(xa, 0.0) + jnp.log(1.0 + jnp.exp(-jnp.abs(xa)))
    g_s[...] = -jnp.exp(alog_ref[...]) * softplus
    b_s[...] = jax.nn.sigmoid(nb_ref[...])

    ri = lax.broadcasted_iota(jnp.int32, (cs, LANES), 0)
    li = lax.broadcasted_iota(jnp.int32, (cs, LANES), 1)
    lo = li < HEAD_LANES
    tj = li % HEAD_LANES
    r64 = lax.broadcasted_iota(jnp.int32, (cs, cs), 0)
    c64 = lax.broadcasted_iota(jnp.int32, (cs, cs), 1)
    tri_lo = jnp.where(r64 >= c64, 1.0, 0.0).astype(BF16)
    tri_up = jnp.where(r64 <= c64, 1.0, 0.0).astype(BF16)
    r128 = lax.broadcasted_iota(jnp.int32, (LANES, LANES), 0)
    c128 = lax.broadcasted_iota(jnp.int32, (LANES, LANES), 1)
    eye = jnp.where(r128 == c128, 1.0, 0.0)
    base = GDN_INVERSE_BASE
    assert base == 8, "the finite product below is written for 8x8 diagonal blocks (D^8 = 0)"
    same_base = (r128 // base) == (c128 // base)
    level_sizes = [base * 2 ** k for k in range(int(math.log2(cs // base)))]
    level_masks = [((r128 // (2 * m)) == (c128 // (2 * m))) & ((r128 // m) != (c128 // m)) for m in level_sizes]
    lo1 = lax.broadcasted_iota(jnp.int32, (1, LANES), 1) < HEAD_LANES

    def stack(x, zero):
        return jnp.concatenate([jnp.where(lo, x, zero), jnp.where(lo, zero, x)], axis=0)

    dirs = ((uf_ref, wf_ref, qdf_ref, kdf_ref, qkf_ref, glf_ref),
            (ub_ref, wb_ref, qdb_ref, kdb_ref, qkb_ref, glb_ref))

    def setup(c, chains):
        r0 = pl.multiple_of(c * cs, cs)
        rows = pl.ds(r0, cs)
        g = g_s[rows, :]
        be = b_s[rows, :]
        gh = g.astype(BF16)
        r1 = g - gh.astype(F32)
        gmid = r1.astype(BF16)
        glo = (r1 - gmid.astype(F32)).astype(BF16)
        cum_f = _dot(tri_lo, gh) + _dot(tri_lo, gmid) + _dot(tri_lo, glo)
        cum_b = _dot(tri_up, gh) + _dot(tri_up, gmid) + _dot(tri_up, glo)
        gc = jnp.where(li < GDN_HEADS, cum_f, cum_b)
        gt = jnp.concatenate([gc, gc], axis=0).T
        for p in range(GDN_PAIRS):
            cols = slice(p * LANES, (p + 1) * LANES)
            q128 = q_s[rows, cols]
            k128 = k_s[rows, cols]
            v128 = v_s[rows, cols]
            kb = k128.astype(BF16)
            kstack = stack(kb, jnp.zeros_like(kb))
            kk = _dot_nt(kb, kstack)
            qk = _dot_nt(q128.astype(BF16), kstack)
            for rev in range(2):
                la = rev * GDN_HEADS + 2 * p
                gca, gcb = gc[:, la:la + 1], gc[:, la + 1:la + 2]
                bca, bcb = be[:, la:la + 1], be[:, la + 1:la + 2]
                gcol = jnp.where(lo, gca, gcb)
                grow = jnp.where(lo1, gt[la:la + 1, :], gt[la + 1:la + 2, :])
                bcol = jnp.where(lo, bca, bcb)
                if rev:
                    incl, strict = ri <= tj, ri < tj
                    last = 0
                else:
                    incl, strict = ri >= tj, ri > tj
                    last = cs - 1
                decay = jnp.exp(jnp.where(incl, gcol - grow, -jnp.inf))
                a128 = jnp.where(strict, kk * decay * bcol, 0.0)
                n = -stack(a128, 0.0)
                ea, eb = jnp.exp(gca), jnp.exp(gcb)
                rhs = jnp.concatenate([
                    jnp.concatenate([v128 * bca, v128 * bcb], axis=0),
                    jnp.concatenate([k128 * (bca * ea), k128 * (bcb * eb)], axis=0)], axis=1)
                glast = jnp.where(lo1, gc[last:last + 1, la:la + 1], gc[last:last + 1, la + 1:la + 2])
                u_ref, w_ref, qd_ref, kd_ref, qk_ref, gl_ref = dirs[rev]
                qd_ref[rows, cols] = (q128 * jnp.exp(gcol)).astype(BF16)
                kd_ref[rows, cols] = (k128 * jnp.exp(glast - gcol)).astype(BF16)
                qk_ref[rows, cols] = (qk * decay).astype(BF16)
                gl_ref[c, :, cols] = jnp.broadcast_to(jnp.exp(glast), (SUBLANES, LANES))
                chains.append(dict(n=n, rhs=rhs, rows=rows, cols=cols, rev=rev))

    def group(i, carry):
        chains = []
        for gi in range(per_iter):
            setup(i * per_iter + gi, chains)
        for ch in chains:
            d0 = jnp.where(same_base, ch["n"], 0.0)
            ch["t"] = eye + d0
            ch["pw"] = _dot_rounded(d0, d0)
        for ch in chains:
            both = _dot_rounded(jnp.concatenate([ch["t"], ch["pw"]], axis=0), ch["pw"])
            ch["t"] = ch["t"] + both[0:LANES]
            ch["pw"] = both[LANES:2 * LANES]
        for ch in chains:
            ch["t"] = ch["t"] + _dot_rounded(ch["t"], ch["pw"])
        for off_mask in level_masks:
            for ch in chains:
                ch["x"] = _dot_rounded(jnp.where(off_mask, ch["n"], 0.0), ch["t"])
            for ch in chains:
                ch["t"] = ch["t"] + _dot_rounded(ch["t"], ch["x"])
        for ch in chains:
            sol = _dot_rounded(ch["t"], ch["rhs"])
            u_ref, w_ref = dirs[ch["rev"]][0:2]
            u_ref[ch["rows"], ch["cols"]] = jnp.where(lo, sol[0:cs, 0:LANES], sol[cs:2 * cs, 0:LANES]).astype(BF16)
            w_ref[ch["rows"], ch["cols"]] = jnp.where(lo, sol[0:cs, LANES:], sol[cs:2 * cs, LANES:]).astype(BF16)
        return carry

    per_iter = math.gcd(GDN_CHUNKS_PER_ITER, tm // cs)
    lax.fori_loop(0, tm // (cs * per_iter), group, 0)


def _gdn_local(nqkv, na, nb, conv_w, alog, dtb):
    b, t, w3 = nqkv.shape
    tm = min(GDN_BLOCK_ROWS, t)
    nblk = t // tm
    cpb = tm // GDN_CHUNK
    halo = SUBLANES
    hb = tm // halo
    tok = lambda width: pl.BlockSpec((None, tm, width), lambda i, j: (i, j, 0))
    const2 = lambda shape: pl.BlockSpec(shape, lambda i, j: (0, 0))
    in_specs = [
        tok(w3),
        pl.BlockSpec((None, halo, w3), lambda i, j: (i, jnp.maximum(j * hb - 1, 0), 0)),
        pl.BlockSpec((None, halo, w3), lambda i, j: (i, jnp.minimum((j + 1) * hb, t // halo - 1), 0)),
        const2((GDN_CONV, w3)), tok(LANES), tok(LANES), const2((1, LANES)), const2((1, LANES)),
    ]
    per_dir_specs = [tok(GDN_WIDTH)] * 5 + [pl.BlockSpec((None, cpb, SUBLANES, GDN_WIDTH), lambda i, j: (i, j, 0, 0))]
    per_dir_shapes = ([jax.ShapeDtypeStruct((b, t, GDN_WIDTH), BF16)] * 5
                      + [jax.ShapeDtypeStruct((b, t // GDN_CHUNK, SUBLANES, GDN_WIDTH), F32)])
    outs = pl.pallas_call(
        _gdn_local_kernel,
        grid=(b, nblk),
        in_specs=in_specs,
        out_specs=per_dir_specs * 2,
        out_shape=per_dir_shapes * 2,
        scratch_shapes=[
            pltpu.VMEM((tm + 2 * halo, w3), F32),
            pltpu.VMEM((tm, GDN_WIDTH), F32), pltpu.VMEM((tm, GDN_WIDTH), F32), pltpu.VMEM((tm, GDN_WIDTH), F32),
            pltpu.VMEM((tm, LANES), F32), pltpu.VMEM((tm, LANES), F32),
        ],
        compiler_params=_cparams(("parallel", "parallel")),
        name="gdn_local",
    )(nqkv, nqkv, nqkv, conv_w, na, nb, alog, dtb)
    return outs[:6], outs[6:]


def _gdn_scan_kernel(uf_ref, wf_ref, qdf_ref, kdf_ref, qkf_ref, glf_ref,
                     ub_ref, wb_ref, qdb_ref, kdb_ref, qkb_ref, glb_ref,
                     s0f_ref, s0b_ref,
                     of_ref, ob_ref, sff_ref, sfb_ref,
                     stf, stb):
    n = pl.program_id(1)
    cs = GDN_CHUNK
    nbat = uf_ref.shape[0]
    cpb = uf_ref.shape[1] // cs

    @pl.when(n == 0)
    def _():
        stf[...] = s0f_ref[...]
        stb[...] = s0b_ref[...]

    lo = lax.broadcasted_iota(jnp.int32, (cs, LANES), 1) < HEAD_LANES
    r128 = lax.broadcasted_iota(jnp.int32, (LANES, LANES), 0) // HEAD_LANES
    c128 = lax.broadcasted_iota(jnp.int32, (LANES, LANES), 1) // HEAD_LANES
    bd = r128 == c128

    fwd = (uf_ref, wf_ref, qdf_ref, kdf_ref, qkf_ref, glf_ref, of_ref, stf)
    bwd = (ub_ref, wb_ref, qdb_ref, kdb_ref, qkb_ref, glb_ref, ob_ref, stb)

    for cc in range(cpb):
        chains = []
        for c, (u_ref, w_ref, qd_ref, kd_ref, qk_ref, gl_ref, o_ref, st) in ((cc, fwd), (cpb - 1 - cc, bwd)):
            rows = slice(c * cs, (c + 1) * cs)
            for bi in range(nbat):
                for p in range(GDN_PAIRS):
                    cols = slice(p * LANES, (p + 1) * LANES)
                    s = st[bi, p]
                    sb = s.astype(BF16)
                    ws = _dot(w_ref[bi, rows, cols], sb)
                    qs = _dot(qd_ref[bi, rows, cols], sb)
                    chains.append((c, bi, rows, cols, p, s, ws, qs, u_ref, kd_ref, qk_ref, gl_ref, o_ref, st))
        for c, bi, rows, cols, p, s, ws, qs, u_ref, kd_ref, qk_ref, gl_ref, o_ref, st in chains:
            vnew = u_ref[bi, rows, cols].astype(F32) - ws
            vb = vnew.astype(BF16)
            zero = jnp.zeros_like(vb)
            vstack = jnp.concatenate([jnp.where(lo, vb, zero), jnp.where(lo, zero, vb)], axis=0)
            o_ref[bi, rows, cols] = qs + _dot(qk_ref[bi, rows, cols], vstack)
            upd = _dot(kd_ref[bi, rows, cols].astype(F32).T.astype(BF16), vb)
            st[bi, p] = s * gl_ref[bi, c, 0:1, cols] + jnp.where(bd, upd, 0.0)

    @pl.when(n == pl.num_programs(1) - 1)
    def _():
        sff_ref[...] = stf[...]
        sfb_ref[...] = stb[...]


def _gdn_scan(fwd, bwd, s0f, s0b):
    b, t, _ = fwd[0].shape
    tm = min(GDN_BLOCK_ROWS, t)
    nblk = t // tm
    cpb = tm // GDN_CHUNK
    bb = math.gcd(GDN_SCAN_BATCH, b)
    f_tok = pl.BlockSpec((bb, tm, GDN_WIDTH), lambda i, j: (i, j, 0))
    b_tok = pl.BlockSpec((bb, tm, GDN_WIDTH), lambda i, j: (i, nblk - 1 - j, 0))
    f_gl = pl.BlockSpec((bb, cpb, SUBLANES, GDN_WIDTH), lambda i, j: (i, j, 0, 0))
    b_gl = pl.BlockSpec((bb, cpb, SUBLANES, GDN_WIDTH), lambda i, j: (i, nblk - 1 - j, 0, 0))
    st_spec = pl.BlockSpec((bb, GDN_PAIRS, LANES, LANES), lambda i, j: (i, 0, 0, 0))
    st_shape = jax.ShapeDtypeStruct((b, GDN_PAIRS, LANES, LANES), F32)
    o_shape = jax.ShapeDtypeStruct((b, t, GDN_WIDTH), F32)
    return pl.pallas_call(
        _gdn_scan_kernel,
        grid=(b // bb, nblk),
        in_specs=[f_tok] * 5 + [f_gl] + [b_tok] * 5 + [b_gl] + [st_spec, st_spec],
        out_specs=[f_tok, b_tok, st_spec, st_spec],
        out_shape=[o_shape, o_shape, st_shape, st_shape],
        scratch_shapes=[pltpu.VMEM((bb, GDN_PAIRS, LANES, LANES), F32)] * 2,
        compiler_params=_cparams(("parallel", "arbitrary")),
        name="gdn_scan",
    )(*fwd, *bwd, s0f, s0b)


def _mix_ffn_kernel(h_ref, d_ref, gq_ref, of_ref, ob_ref, z_ref, ng_ref, gate1_ref, wo_ref,
                    g_ref, sh_ref, sc_ref, gate_ref, wgu_ref, wd_ref, fg_ref, o_ref, *, final_norm):
    gm = _group_matrix(1.0 / GDN_HEAD_DIM)
    pieces = [d_ref[...], gq_ref[...]]
    for blk in range(GDN_PAIRS):
        cols = slice(blk * LANES, (blk + 1) * LANES)
        o = of_ref[:, cols] + ob_ref[:, cols]
        ms = _group_sum(o * o, gm)
        r = o * lax.rsqrt(ms + NORM_EPS) * ng_ref[...] * jax.nn.silu(z_ref[:, cols])
        pieces.append(r.astype(BF16))
    x = h_ref[...] + gate1_ref[...] * _dot(jnp.concatenate(pieces, axis=1), wo_ref[...])

    y = x * lax.rsqrt(jnp.mean(x * x, axis=-1, keepdims=True) + NORM_EPS) * g_ref[...]
    a = (y * (1.0 + sc_ref[...]) + sh_ref[...]).astype(BF16)
    hidden = wd_ref.shape[0]
    starts = list(range(0, hidden, FFN_CHUNK))

    def gate_up(lo):
        width = min(FFN_CHUNK, hidden - lo)
        return _dot(a, wgu_ref[:, lo:lo + width]), _dot(a, wgu_ref[:, hidden + lo:hidden + lo + width]), width

    pending = [gate_up(lo) for lo in starts[:FFN_SKEW]]
    acc = None
    for n, lo in enumerate(starts):
        gate, up, width = pending.pop(0)
        if n + FFN_SKEW < len(starts):
            pending.append(gate_up(starts[n + FFN_SKEW]))
        act = (jax.nn.silu(gate) * up).astype(BF16)
        part = _dot(act, wd_ref[lo:lo + width, :])
        acc = part if acc is None else acc + part
    out = x + gate_ref[...] * acc
    if final_norm:
        out = out * lax.rsqrt(jnp.mean(out * out, axis=-1, keepdims=True) + NORM_EPS) * fg_ref[...]
    o_ref[...] = out


def _mix_ffn(h, d, gq, of, ob, z, ng, gate1, w_out, g, sh, sc, gate, w_gu, w_down, fg, final_norm):
    b, t, dm = h.shape
    tm = min(MIX_FFN_ROWS, t)
    nb = sh.shape[0]
    mod_map = (lambda i, k: (i, 0, 0)) if nb == b else (lambda i, k: (0, 0, 0))
    tok = lambda width: pl.BlockSpec((None, tm, width), lambda i, k: (i, k, 0))
    vec = lambda width: pl.BlockSpec((1, width), lambda i, k: (0, 0))
    mod = pl.BlockSpec((None, 1, dm), mod_map)
    resident = lambda w: pl.BlockSpec(w.shape, lambda i, k: (0, 0), pipeline_mode=pl.Buffered(1))
    return pl.pallas_call(
        functools.partial(_mix_ffn_kernel, final_norm=final_norm),
        grid=(b, t // tm),
        in_specs=[tok(dm), tok(DIFF_WIDTH), tok(GQA_WIDTH), tok(GDN_WIDTH), tok(GDN_WIDTH), tok(GDN_WIDTH),
                  vec(LANES), mod, resident(w_out),
                  vec(dm), mod, mod, mod, resident(w_gu), resident(w_down), vec(dm)],
        out_specs=tok(dm),
        out_shape=jax.ShapeDtypeStruct((b, t, dm), F32),
        compiler_params=_cparams(("parallel", "parallel")),
        name="mix_ffn",
    )(h, d, gq, of, ob, z, ng, gate1, w_out, g, sh, sc, gate, w_gu, w_down, fg)


def _rope_tables(t, rot_dim):
    nf = rot_dim // 4
    pos = jnp.arange(t)
    row = (pos // GRID_W).astype(F32)
    col = (pos % GRID_W).astype(F32)
    inv_freq = ROPE_THETA ** (-jnp.arange(nf, dtype=F32) / nf)
    ar, ac = row[:, None] * inv_freq, col[:, None] * inv_freq
    z = jnp.zeros_like(ar)
    cos = jnp.concatenate([jnp.cos(ar), jnp.cos(ar), jnp.cos(ac), jnp.cos(ac)], axis=1)
    sa = jnp.concatenate([-jnp.sin(ar), z, -jnp.sin(ac), z], axis=1)
    sb = jnp.concatenate([z, jnp.sin(ar), z, jnp.sin(ac)], axis=1)
    rep = LANES // rot_dim
    return tuple(jnp.tile(a, (1, rep)) for a in (cos, sa, sb))


def _pad_lanes(v):
    return jnp.pad(v.reshape(1, -1), ((0, 0), (0, LANES - v.size)))


def _relayout_w_in(w):
    o = 0
    pieces = []
    for width in (DIFF_WIDTH, DIFF_WIDTH, DIFF_WIDTH):
        pieces.append(w[:, o:o + width]); o += width
    gq = w[:, o:o + GQA_WIDTH].reshape(-1, GQA_Q_HEADS, GQA_HEAD_DIM); o += GQA_WIDTH
    pieces.append(jnp.take(gq, jnp.array(GQA_Q_ORDER), axis=1).reshape(-1, GQA_WIDTH))
    for width in (GQA_KV_WIDTH, GQA_KV_WIDTH, 3 * GDN_WIDTH, GDN_WIDTH):
        pieces.append(w[:, o:o + width]); o += width
    for width in (2 * GDN_HEADS, 2 * GDN_HEADS):
        pieces.append(jnp.pad(w[:, o:o + width], ((0, 0), (0, LANES - width)))); o += width
    return jnp.concatenate(pieces, axis=1).astype(BF16)


def _relayout_w_out(w):
    gq = w[DIFF_WIDTH:DIFF_WIDTH + GQA_WIDTH].reshape(GQA_Q_HEADS, GQA_HEAD_DIM, -1)
    gq = jnp.take(gq, jnp.array(GQA_Q_ORDER), axis=0).reshape(GQA_WIDTH, -1)
    return jnp.concatenate([w[:DIFF_WIDTH], gq, w[DIFF_WIDTH + GQA_WIDTH:]], axis=0).astype(BF16)


def kernel(x, c, ctx, c_ctx, norm1_g, ada_w, ada_b, w_in, diff_lambda, diff_norm_g, q_norm_g, k_norm_g,
           gdn_conv_w, gdn_a_log, gdn_dt_bias, gdn_norm_g, w_out, norm2_g, ffn_w_gu, ffn_w_down, final_norm_g):
    b, t, d = x.shape
    depth = w_in.shape[0]
    rope = _rope_tables(t, DIFF_QK_DIM) + _rope_tables(t, GQA_HEAD_DIM)

    cond = jnp.concatenate([c, c_ctx[None, :], jnp.zeros((COND_ROWS - b - 1, d), F32)], axis=0)
    mod = _ada(cond, ada_w, ada_b).reshape(depth, COND_ROWS, 6, d)

    tile2 = lambda v: jnp.tile(v.reshape(1, -1), (1, LANES // v.size))
    zeros_state = jnp.zeros((b, GDN_PAIRS, LANES, LANES), F32)

    h, hc = x, ctx
    for layer in range(depth):
        need_ctx = layer < depth - 1
        lambda_init = 0.8 - 0.6 * math.exp(-0.3 * layer)
        mod_l = [mod[layer, :b, k][:, None, :] for k in range(6)]
        mod_c = [mod[layer, b:b + 1, k][:, None, :] for k in range(6)]
        w_in_l = _relayout_w_in(w_in[layer])
        w_out_l = _relayout_w_out(w_out[layer])
        w_gu_l = ffn_w_gu[layer].astype(BF16)
        w_down_l = ffn_w_down[layer].astype(BF16)
        g1 = norm1_g[layer].reshape(1, d)
        g2 = norm2_g[layer].reshape(1, d)
        qg, kg = tile2(q_norm_g[layer]), tile2(k_norm_g[layer])
        dng, nng = tile2(diff_norm_g[layer]), tile2(gdn_norm_g[layer])
        alog, dtb = _pad_lanes(gdn_a_log[layer]), _pad_lanes(gdn_dt_bias[layer])
        lam = diff_lambda[layer]
        conv_w = gdn_conv_w[layer]

        pl_ = _in_proj(h, g1, mod_l[0], mod_l[1], w_in_l, qg, kg, rope)
        pc_ = _in_proj(hc, g1, mod_c[0], mod_c[1], w_in_l, qg, kg, None)
        dq_l, dk_l, dv_l, gq_l, gk_l, gv_l, nqkv_l, nz_l, na_l, nb_l = pl_
        dq_c, dk_c, dv_c, gq_c, gk_c, gv_c, nqkv_c, nz_c, na_c, nb_c = pc_

        d_l = _attention("diff", dq_l, dk_c, dv_c, dk_l, dv_l, (lam, dng), lambda_init)
        a_l = _attention("gqa", gq_l, gk_c, gv_c, gk_l, gv_l, ())

        fwd_c, bwd_c = _gdn_local(nqkv_c, na_c, nb_c, conv_w, alog, dtb)
        fwd_l, bwd_l = _gdn_local(nqkv_l, na_l, nb_l, conv_w, alog, dtb)
        ocf, ocb, scf, scb = _gdn_scan(fwd_c, bwd_c, zeros_state, zeros_state)
        olf, olb, _, _ = _gdn_scan(fwd_l, bwd_l, scf, scb)

        last = layer == depth - 1
        fg = final_norm_g.reshape(1, d)
        h = _mix_ffn(h, d_l, a_l, olf, olb, nz_l, nng, mod_l[2], w_out_l,
                     g2, mod_l[3], mod_l[4], mod_l[5], w_gu_l, w_down_l, fg, last)
        if need_ctx:
            d_c = _attention("diff", dq_c, dk_c, dv_c, None, None, (lam, dng), lambda_init)
            a_c = _attention("gqa", gq_c, gk_c, gv_c, None, None, ())
            hc = _mix_ffn(hc, d_c, a_c, ocf, ocb, nz_c, nng, mod_c[2], w_out_l,
                          g2, mod_c[3], mod_c[4], mod_c[5], w_gu_l, w_down_l, fg, False)
    return h
```
